```python
import math
import jax, jax.numpy as jnp
from jax import lax
import numpy as np


D_MODEL = 2048
BATCH = 8
SEQ = 2048
DEPTH = 2

CHUNK = 64
Q_BLOCK = 128
FOX_HEADS = 8
FOX_HEAD_DIM = 128
FOX_WIDTH = FOX_HEADS * FOX_HEAD_DIM
RET_HEADS = 8
RET_QK_DIM = 128
RET_V_DIM = 256
RET_QK_WIDTH = RET_HEADS * RET_QK_DIM
RET_V_WIDTH = RET_HEADS * RET_V_DIM
ROPE_BASE = 10000.0
N_GROUPS = 4
EXPERTS_PER_GROUP = 8
N_EXPERTS = N_GROUPS * EXPERTS_PER_GROUP
TOP_K = 2
D_EXPERT = 512
DEEPNORM_ALPHA = (2.0 * DEPTH) ** 0.25
DEEPNORM_BETA = (8.0 * DEPTH) ** -0.25
LN_EPS = 1e-5
RMS_EPS = 1e-6
IN_SIZES = (FOX_WIDTH, FOX_WIDTH, FOX_WIDTH, FOX_HEADS,
            RET_QK_WIDTH, RET_QK_WIDTH, RET_V_WIDTH, RET_V_WIDTH,
            D_MODEL, D_MODEL)
IN_WIDTH = sum(IN_SIZES)
VALUE_SLOTS = (2, 6)

kernel_name = 'hybrid_fox_retention_hmoe_deepnorm'


def layer_norm(x, g, b):
    xf = x.astype(jnp.float32)
    mu = jnp.mean(xf, axis=-1, keepdims=True)
    var = jnp.mean(jnp.square(xf - mu), axis=-1, keepdims=True)
    y = (xf - mu) * lax.rsqrt(var + LN_EPS) * g.astype(jnp.float32) + b.astype(jnp.float32)
    return y.astype(x.dtype)


def head_rms_norm(y):
    yf = y.astype(jnp.float32)
    return (yf * lax.rsqrt(jnp.mean(yf * yf, axis=-1, keepdims=True) + RMS_EPS)).astype(y.dtype)


def split_columns(proj):
    outs = []
    start = 0
    for size in IN_SIZES:
        outs.append(proj[..., start:start + size])
        start += size
    return outs


def to_heads(t, n_heads):
    b, s, w = t.shape
    return t.reshape(b, s, n_heads, w // n_heads).transpose(0, 2, 1, 3)


def from_heads(t):
    b, h, s, d = t.shape
    return t.transpose(0, 2, 1, 3).reshape(b, s, h * d)


def rotary(x, pos):
    half = x.shape[-1] // 2
    inv_freq = ROPE_BASE ** (-jnp.arange(half, dtype=jnp.float32) / half)
    ang = pos.astype(jnp.float32)[:, None] * inv_freq[None, :]
    cos = jnp.cos(ang).astype(x.dtype)
    sin = jnp.sin(ang).astype(x.dtype)
    x1, x2 = x[..., :half], x[..., half:]
    return jnp.concatenate([x1 * cos - x2 * sin, x1 * sin + x2 * cos], axis=-1)


def forgetting_attention(q, k, v, log_f):
    seq = q.shape[2]
    scale = FOX_HEAD_DIM ** -0.5
    cum = jnp.cumsum(log_f, axis=-1)
    outs = []
    for i in range(seq // Q_BLOCK):
        q0, q1 = i * Q_BLOCK, (i + 1) * Q_BLOCK
        logits = jnp.einsum('bhqd,bhkd->bhqk', q[:, :, q0:q1], k[:, :, :q1],
                            preferred_element_type=jnp.float32) * scale
        logits = logits + cum[:, :, q0:q1, None] - cum[:, :, None, :q1]
        t_idx = jnp.arange(q0, q1)[:, None]
        s_idx = jnp.arange(q1)[None, :]
        logits = jnp.where(s_idx <= t_idx, logits, -jnp.inf)
        p = jax.nn.softmax(logits, axis=-1)
        outs.append(jnp.einsum('bhqk,bhkd->bhqd', p.astype(v.dtype), v[:, :, :q1]))
    return jnp.concatenate(outs, axis=2)


def retention(q, k, v):
    b, h, seq, dk = q.shape
    dv = v.shape[-1]
    n_chunks = seq // CHUNK
    log_gamma = jnp.log1p(-(2.0 ** (-5.0 - jnp.arange(h, dtype=jnp.float32))))
    pos = jnp.arange(CHUNK, dtype=jnp.float32)
    intra_decay = jnp.exp(log_gamma[:, None, None] * jnp.abs(pos[:, None] - pos[None, :]))
    key_decay = jnp.exp(log_gamma[:, None] * (CHUNK - 1 - pos))[:, :, None]
    query_decay = jnp.exp(log_gamma[:, None] * (pos + 1.0))[:, :, None]
    chunk_decay = jnp.exp(log_gamma * CHUNK)[:, None, None]
    k = k * (dk ** -0.5)

    def to_chunks(t):
        return jnp.moveaxis(t.reshape(b, h, n_chunks, CHUNK, t.shape[-1]), 2, 0)

    def step(state, inp):
        qc, kc, vc = inp
        scores = jnp.einsum('bhcd,bhmd->bhcm', qc, kc) * intra_decay
        intra = jnp.einsum('bhcm,bhmv->bhcv', scores, vc)
        cross = jnp.einsum('bhcd,bhdv->bhcv', qc, state) * query_decay
        new_state = state * chunk_decay + jnp.einsum('bhmd,bhmv->bhdv', kc * key_decay, vc)
        return new_state, intra + cross

    state0 = jnp.zeros((b, h, dk, dv), jnp.float32)
    _, out = lax.scan(step, state0, (to_chunks(q), to_chunks(k), to_chunks(v)))
    out = jnp.moveaxis(out, 0, 2).reshape(b, h, seq, dv)
    return out.astype(v.dtype)


def token_mixers(h, w_in, b_forget, w_branch_fox, w_branch_ret, w_out):
    seq = h.shape[1]
    proj = h @ w_in
    fq, fk, fv, ff, rq, rk, rv, rg, gate_fox, gate_ret = split_columns(proj)
    log_f = jax.nn.log_sigmoid(ff.astype(jnp.float32) + b_forget.astype(jnp.float32))
    y_fox = forgetting_attention(to_heads(fq, FOX_HEADS), to_heads(fk, FOX_HEADS),
                                 to_heads(fv, FOX_HEADS), log_f.transpose(0, 2, 1))
    y_fox = from_heads(y_fox)
    pos = jnp.arange(seq, dtype=jnp.int32)
    y_ret = retention(rotary(to_heads(rq, RET_HEADS), pos), rotary(to_heads(rk, RET_HEADS), pos),
                      to_heads(rv, RET_HEADS))
    y_ret = jax.nn.silu(rg) * from_heads(head_rms_norm(y_ret))
    merged = jax.nn.sigmoid(gate_fox) * (y_fox @ w_branch_fox) + jax.nn.sigmoid(gate_ret) * (y_ret @ w_branch_ret)
    return merged @ w_out


def hierarchical_moe(h, w_router_group, b_router_group, w_router_expert, b_router_expert,
                     w_gate, w_up, w_down):
    b, s, d = h.shape
    n = b * s
    t = h.reshape(n, d)
    grp_prob = jax.nn.softmax((t @ w_router_group).astype(jnp.float32) + b_router_group.astype(jnp.float32), axis=-1)
    grp_w, grp_idx = lax.top_k(grp_prob, 1)
    exp_logits = ((t @ w_router_expert).astype(jnp.float32)
                  + b_router_expert.astype(jnp.float32)).reshape(n, N_GROUPS, EXPERTS_PER_GROUP)
    chosen = exp_logits[jnp.arange(n), grp_idx[:, 0]]
    top_w, top_idx = lax.top_k(jax.nn.softmax(chosen, axis=-1), TOP_K)
    top_w = top_w / jnp.sum(top_w, axis=-1, keepdims=True) * grp_w
    flat_idx = grp_idx * EXPERTS_PER_GROUP + top_idx
    combine = jnp.einsum('nk,nkx->nx', top_w, jax.nn.one_hot(flat_idx, N_EXPERTS, dtype=jnp.float32))
    combine = combine.reshape(n, N_GROUPS, EXPERTS_PER_GROUP).astype(h.dtype)
    out = jnp.zeros((n, d), h.dtype)
    for g in range(N_GROUPS):
        hid = jax.nn.silu(jnp.einsum('nd,edf->nef', t, w_gate[g])) * jnp.einsum('nd,edf->nef', t, w_up[g])
        out = out + jnp.einsum('nef,efd->nd', hid * combine[:, g, :, None], w_down[g])
    return out.reshape(b, s, d)


def setup_inputs(seed: int = 0) -> dict:
    key = jax.random.key(seed)
    ks = jax.random.split(key, 24)
    f32 = jnp.float32
    nrm = lambda k, shape, scale: jax.random.normal(k, shape, f32) * scale
    x = jax.random.normal(ks[0], (BATCH, SEQ, D_MODEL), f32)
    ln_in_g = 1.0 + nrm(ks[1], (D_MODEL,), 0.02)
    ln_in_b = nrm(ks[2], (D_MODEL,), 0.02)
    col_scale = jnp.concatenate([jnp.full((size,), DEEPNORM_BETA if i in VALUE_SLOTS else 1.0, f32)
                                 for i, size in enumerate(IN_SIZES)])
    w_in = nrm(ks[3], (DEPTH, D_MODEL, IN_WIDTH), D_MODEL ** -0.5) * col_scale
    b_forget = 2.0 + 4.0 * jax.random.uniform(ks[4], (DEPTH, FOX_HEADS), f32)
    w_branch_fox = nrm(ks[5], (DEPTH, FOX_WIDTH, D_MODEL), FOX_WIDTH ** -0.5 * DEEPNORM_BETA)
    w_branch_ret = nrm(ks[6], (DEPTH, RET_V_WIDTH, D_MODEL), RET_V_WIDTH ** -0.5 * DEEPNORM_BETA)
    w_out = nrm(ks[7], (DEPTH, D_MODEL, D_MODEL), D_MODEL ** -0.5 * DEEPNORM_BETA)
    ln1_g = 1.0 + nrm(ks[8], (DEPTH, D_MODEL), 0.02)
    ln1_b = nrm(ks[9], (DEPTH, D_MODEL), 0.02)
    w_router_group = nrm(ks[10], (DEPTH, D_MODEL, N_GROUPS), D_MODEL ** -0.5)
    b_router_group = nrm(ks[11], (DEPTH, N_GROUPS), 0.01)
    w_router_expert = nrm(ks[12], (DEPTH, D_MODEL, N_EXPERTS), D_MODEL ** -0.5)
    b_router_expert = nrm(ks[13], (DEPTH, N_EXPERTS), 0.01)
    w_gate = nrm(ks[14], (DEPTH, N_GROUPS, EXPERTS_PER_GROUP, D_MODEL, D_EXPERT), D_MODEL ** -0.5 * DEEPNORM_BETA)
    w_up = nrm(ks[15], (DEPTH, N_GROUPS, EXPERTS_PER_GROUP, D_MODEL, D_EXPERT), D_MODEL ** -0.5 * DEEPNORM_BETA)
    w_down = nrm(ks[16], (DEPTH, N_GROUPS, EXPERTS_PER_GROUP, D_EXPERT, D_MODEL), D_EXPERT ** -0.5 * DEEPNORM_BETA)
    ln2_g = 1.0 + nrm(ks[17], (DEPTH, D_MODEL), 0.02)
    ln2_b = nrm(ks[18], (DEPTH, D_MODEL), 0.02)
    return {'x': x, 'ln_in_g': ln_in_g, 'ln_in_b': ln_in_b, 'w_in': w_in, 'b_forget': b_forget,
            'w_branch_fox': w_branch_fox, 'w_branch_ret': w_branch_ret, 'w_out': w_out,
            'ln1_g': ln1_g, 'ln1_b': ln1_b,
            'w_router_group': w_router_group, 'b_router_group': b_router_group,
            'w_router_expert': w_router_expert, 'b_router_expert': b_router_expert,
            'w_gate': w_gate, 'w_up': w_up, 'w_down': w_down, 'ln2_g': ln2_g, 'ln2_b': ln2_b}


def reference(x, ln_in_g, ln_in_b, w_in, b_forget, w_branch_fox, w_branch_ret, w_out,
              ln1_g, ln1_b, w_router_group, b_router_group, w_router_expert, b_router_expert,
              w_gate, w_up, w_down, ln2_g, ln2_b):
    h = layer_norm(x, ln_in_g, ln_in_b)
    for l in range(DEPTH):
        mix = token_mixers(h, w_in[l], b_forget[l], w_branch_fox[l], w_branch_ret[l], w_out[l])
        h = layer_norm(DEEPNORM_ALPHA * h + mix, ln1_g[l], ln1_b[l])
        ffn = hierarchical_moe(h, w_router_group[l], b_router_group[l], w_router_expert[l],
                               b_router_expert[l], w_gate[l], w_up[l], w_down[l])
        h = layer_norm(DEEPNORM_ALPHA * h + ffn, ln2_g[l], ln2_b[l])
    return h
```

```python
import functools

import jax
import jax.numpy as jnp
from jax import lax
from jax.experimental import pallas as pl
from jax.experimental.pallas import tpu as pltpu

F32 = jnp.float32
BF16 = jnp.bfloat16

CHUNK = 64
FOX_HEADS = 8
FOX_HEAD_DIM = 128
RET_HEADS = 8
RET_QK_DIM = 128
RET_V_DIM = 256
ROPE_BASE = 10000.0
N_GROUPS = 4
EXPERTS_PER_GROUP = 8
N_EXPERTS = N_GROUPS * EXPERTS_PER_GROUP
LOG2_EXPERTS_PER_GROUP = EXPERTS_PER_GROUP.bit_length() - 1
LN_EPS = 1e-5
RMS_EPS = 1e-6

LANES = 128
EXPERT_LANE0 = N_GROUPS
VMEM_LIMIT = 56 * 1024 * 1024

ROW_TILE = 512
ATTN_TILE = 256
RET_TILE = 256
EXPERT_TILE = 256
GATHER_TILE = 256


def _params(*sem):
    return pltpu.CompilerParams(dimension_semantics=sem, vmem_limit_bytes=VMEM_LIMIT)


def _layer_norm_rows(x, g, b):
    mu = jnp.mean(x, axis=-1, keepdims=True)
    xc = x - mu
    var = jnp.mean(xc * xc, axis=-1, keepdims=True)
    return xc * lax.rsqrt(var + LN_EPS) * g + b


def _ln_kernel(x_ref, g_ref, b_ref, o32_ref, o16_ref):
    y = _layer_norm_rows(x_ref[...], g_ref[...], b_ref[...])
    o32_ref[...] = y
    o16_ref[...] = y.astype(BF16)


def _layer_norm(x, g, b):
    n, d = x.shape
    tm = min(ROW_TILE, n)
    row = pl.BlockSpec((tm, d), lambda i: (i, 0))
    vec = pl.BlockSpec((1, d), lambda i: (0, 0))
    return pl.pallas_call(
        _ln_kernel, grid=(n // tm,), in_specs=[row, vec, vec], out_specs=[row, row],
        out_shape=[jax.ShapeDtypeStruct((n, d), F32), jax.ShapeDtypeStruct((n, d), BF16)],
        compiler_params=_params("parallel"), name="ln_in",
    )(x, g.reshape(1, d), b.reshape(1, d))


def _proj_kernel(x_ref, w_ref, s_ref, *rest, act):
    acc = jnp.dot(x_ref[...], w_ref[...], preferred_element_type=F32) * s_ref[...]
    if act == "rotary":
        cos_ref, sin_ref, o_ref = rest
        cos, sin = cos_ref[...], sin_ref[...]
        for c in range(acc.shape[1] // LANES):
            blk = acc[:, c * LANES:(c + 1) * LANES]
            rot = blk * cos + pltpu.roll(blk, LANES // 2, 1) * sin
            o_ref[:, c * LANES:(c + 1) * LANES] = rot.astype(o_ref.dtype)
        return
    (o_ref,) = rest
    if act == "silu":
        acc = acc * jax.nn.sigmoid(acc)
    elif act == "sigmoid":
        acc = jax.nn.sigmoid(acc)
    o_ref[...] = acc.astype(o_ref.dtype)


def _proj(x16, w16, colscale, act, seq, tables=None, tn=512):
    n, k = x16.shape
    m = w16.shape[1]
    tm = min(ROW_TILE, n)
    in_specs = [pl.BlockSpec((tm, k), lambda i, j: (i, 0)),
                pl.BlockSpec((k, tn), lambda i, j: (0, j)),
                pl.BlockSpec((1, tn), lambda i, j: (0, j))]
    args = [x16, w16, colscale.reshape(1, m)]
    if act == "rotary":
        per_seq = seq // tm
        tab = pl.BlockSpec((tm, LANES), lambda i, j: (i % per_seq, 0))
        in_specs += [tab, tab]
        args += list(tables)
    return pl.pallas_call(
        functools.partial(_proj_kernel, act=act), grid=(n // tm, m // tn),
        in_specs=in_specs, out_specs=pl.BlockSpec((tm, tn), lambda i, j: (i, j)),
        out_shape=jax.ShapeDtypeStruct((n, m), BF16),
        compiler_params=_params("parallel", "parallel"), name="proj_" + act,
    )(*args)


def _forget_kernel(h_ref, w_ref, b_ref, o_ref):
    z = lax.dot_general(w_ref[...], h_ref[...], (((1,), (1,)), ((), ())), preferred_element_type=F32)
    z = z + b_ref[...]
    lf = jnp.minimum(z, 0.0) - jnp.log1p(jnp.exp(-jnp.abs(z)))
    seq = lf.shape[1]
    lane = lax.broadcasted_iota(jnp.int32, lf.shape, 1)
    k = 1
    while k < seq:
        lf = lf + jnp.where(lane >= k, pltpu.roll(lf, k, 1), 0.0)
        k *= 2
    o_ref[0] = lf


def _forget_cumsum(h16, w_t16, b_forget, batch, seq):
    d = h16.shape[1]
    heads = w_t16.shape[0]
    return pl.pallas_call(
        _forget_kernel, grid=(batch,),
        in_specs=[pl.BlockSpec((seq, d), lambda b: (b, 0)),
                  pl.BlockSpec((heads, d), lambda b: (0, 0)),
                  pl.BlockSpec((heads, 1), lambda b: (0, 0))],
        out_specs=pl.BlockSpec((1, heads, seq), lambda b: (b, 0, 0)),
        out_shape=jax.ShapeDtypeStruct((batch, heads, seq), F32),
        compiler_params=_params("parallel"), name="forget_cumsum",
    )(h16, w_t16, b_forget.reshape(heads, 1).astype(F32))


def _fox_kernel(q_ref, k_ref, v_ref, c_ref, o_ref, m_sc, l_sc, acc_sc, *, tile):
    qi = pl.program_id(2)
    q = q_ref[...]
    m_sc[...] = jnp.full(m_sc.shape, -jnp.inf, F32)
    l_sc[...] = jnp.zeros(l_sc.shape, F32)
    acc_sc[...] = jnp.zeros(acc_sc.shape, F32)

    def block(j, diagonal):
        off = pl.multiple_of(j * tile, tile)
        k = k_ref[pl.ds(off, tile), :]
        v = v_ref[pl.ds(off, tile), :]
        s = lax.dot_general(q, k, (((1,), (1,)), ((), ())), preferred_element_type=F32)
        s = s - c_ref[0, 0, j]
        if diagonal:
            row = lax.broadcasted_iota(jnp.int32, s.shape, 0)
            col = lax.broadcasted_iota(jnp.int32, s.shape, 1)
            s = jnp.where(col <= row, s, -jnp.inf)
        m_prev = m_sc[...]
        m_new = jnp.maximum(m_prev, jnp.max(s, axis=1, keepdims=True))
        a = jnp.exp(m_prev - m_new)
        p = jnp.exp(s - m_new)
        l_sc[...] = a * l_sc[...] + jnp.sum(p, axis=1, keepdims=True)
        acc_sc[...] = a * acc_sc[...] + jnp.dot(p.astype(BF16), v, preferred_element_type=F32)
        m_sc[...] = m_new

    def body(j, carry):
        block(j, False)
        return carry

    lax.fori_loop(0, qi, body, 0)
    block(qi, True)
    o_ref[...] = (acc_sc[...] / l_sc[...]).astype(o_ref.dtype)


def _fox_attention(qkv16, cum, batch, seq):
    heads, dh = FOX_HEADS, FOX_HEAD_DIM
    tile = min(ATTN_TILE, seq)
    nq = seq // tile
    cum5 = cum.reshape(batch, heads, nq, 1, tile)
    return pl.pallas_call(
        functools.partial(_fox_kernel, tile=tile), grid=(batch, heads, nq),
        in_specs=[pl.BlockSpec((tile, dh), lambda b, h, i: (b * nq + i, h)),
                  pl.BlockSpec((seq, dh), lambda b, h, i: (b, heads + h)),
                  pl.BlockSpec((seq, dh), lambda b, h, i: (b, 2 * heads + h)),
                  pl.BlockSpec((1, 1, nq, 1, tile), lambda b, h, i: (b, h, 0, 0, 0))],
        out_specs=pl.BlockSpec((tile, dh), lambda b, h, i: (b * nq + i, h)),
        out_shape=jax.ShapeDtypeStruct((batch * seq, heads * dh), BF16),
        scratch_shapes=[pltpu.VMEM((tile, 1), F32), pltpu.VMEM((tile, 1), F32), pltpu.VMEM((tile, dh), F32)],
        compiler_params=_params("parallel", "parallel", "parallel"), name="fox_attention",
    )(qkv16, qkv16, qkv16, cum5)


def _ret_kernel(q_ref, k_ref, v_ref, g_ref, dm_ref, qd_ref, kd_ref, cd_ref, o_ref, st_sc):
    @pl.when(pl.program_id(2) == 0)
    def _():
        st_sc[...] = jnp.zeros(st_sc.shape, F32)

    q, k, v = q_ref[...], k_ref[...], v_ref[...]
    scores = lax.dot_general(q, k, (((1,), (1,)), ((), ())), preferred_element_type=F32) * dm_ref[0]
    intra = jnp.dot(scores.astype(BF16), v, preferred_element_type=F32)
    state = st_sc[...]
    cross = jnp.dot(q, state.astype(BF16), preferred_element_type=F32) * qd_ref[0]
    k_dec = (k.astype(F32) * kd_ref[0]).astype(BF16)
    st_sc[...] = state * cd_ref[0] + lax.dot_general(
        k_dec, v, (((0,), (0,)), ((), ())), preferred_element_type=F32)
    o = intra + cross
    o = o * lax.rsqrt(jnp.mean(o * o, axis=-1, keepdims=True) + RMS_EPS)
    o_ref[...] = (g_ref[...].astype(F32) * o).astype(o_ref.dtype)


def _retention_tables(tile):
    heads = RET_HEADS
    log_gamma = jnp.log1p(-(2.0 ** (-5.0 - jnp.arange(heads, dtype=F32))))
    pos = jnp.arange(tile, dtype=F32)
    chunk = jnp.arange(tile, dtype=jnp.int32) // CHUNK
    dist = pos[:, None] - pos[None, :]
    same = chunk[:, None] == chunk[None, :]
    earlier = chunk[None, :] < chunk[:, None]
    expo = jnp.where(same, jnp.abs(dist), dist)
    dmask = jnp.where((same | earlier)[None], jnp.exp(log_gamma[:, None, None] * expo[None]), 0.0)
    q_dec = jnp.exp(log_gamma[:, None] * (pos + 1.0))[:, :, None]
    k_dec = jnp.exp(log_gamma[:, None] * (tile - 1 - pos))[:, :, None]
    c_dec = jnp.broadcast_to(jnp.exp(log_gamma * tile)[:, None, None], (heads, 1, RET_V_DIM))
    return dmask, q_dec, k_dec, c_dec


def _retention(rqk16, rv16, rg16, batch, seq):
    heads, dk, dv = RET_HEADS, RET_QK_DIM, RET_V_DIM
    tile = min(RET_TILE, seq)
    nt = seq // tile
    dmask, q_dec, k_dec, c_dec = _retention_tables(tile)
    val = pl.BlockSpec((tile, dv), lambda b, h, t: (b * nt + t, h))
    return pl.pallas_call(
        _ret_kernel, grid=(batch, heads, nt),
        in_specs=[pl.BlockSpec((tile, dk), lambda b, h, t: (b * nt + t, h)),
                  pl.BlockSpec((tile, dk), lambda b, h, t: (b * nt + t, heads + h)),
                  val, val,
                  pl.BlockSpec((1, tile, tile), lambda b, h, t: (h, 0, 0)),
                  pl.BlockSpec((1, tile, 1), lambda b, h, t: (h, 0, 0)),
                  pl.BlockSpec((1, tile, 1), lambda b, h, t: (h, 0, 0)),
                  pl.BlockSpec((1, 1, dv), lambda b, h, t: (h, 0, 0))],
        out_specs=val,
        out_shape=jax.ShapeDtypeStruct((batch * seq, heads * dv), BF16),
        scratch_shapes=[pltpu.VMEM((dk, dv), F32)],
        compiler_params=_params("parallel", "parallel", "arbitrary"), name="retention",
    )(rqk16, rqk16, rv16, rg16, dmask, q_dec, k_dec, c_dec)


def _merge_kernel(yf_ref, yr_ref, wf_ref, wr_ref, gf_ref, gr_ref, o_ref):
    fox = jnp.dot(yf_ref[...], wf_ref[...], preferred_element_type=F32)
    ret = jnp.dot(yr_ref[...], wr_ref[...], preferred_element_type=F32)
    o_ref[...] = (gf_ref[...].astype(F32) * fox + gr_ref[...].astype(F32) * ret).astype(o_ref.dtype)


def _merge(y_fox, y_ret, w_fox16, w_ret16, gates16, tn=512):
    n, kf = y_fox.shape
    kr = y_ret.shape[1]
    d = w_fox16.shape[1]
    tm = min(ROW_TILE, n)
    nj = d // tn
    return pl.pallas_call(
        _merge_kernel, grid=(n // tm, nj),
        in_specs=[pl.BlockSpec((tm, kf), lambda i, j: (i, 0)),
                  pl.BlockSpec((tm, kr), lambda i, j: (i, 0)),
                  pl.BlockSpec((kf, tn), lambda i, j: (0, j)),
                  pl.BlockSpec((kr, tn), lambda i, j: (0, j)),
                  pl.BlockSpec((tm, tn), lambda i, j: (i, j)),
                  pl.BlockSpec((tm, tn), lambda i, j: (i, nj + j))],
        out_specs=pl.BlockSpec((tm, tn), lambda i, j: (i, j)),
        out_shape=jax.ShapeDtypeStruct((n, d), BF16),
        compiler_params=_params("parallel", "parallel"), name="merge",
    )(y_fox, y_ret, w_fox16, w_ret16, gates16, gates16)


def _lane_pick(mask, values):
    return jnp.sum(jnp.where(mask, values, 0.0), axis=1, keepdims=True)


def _outproj_router_kernel(m_ref, w_ref, h_ref, g_ref, b_ref, wr_ref, br_ref,
                           h1_ref, route_ref, wcol_ref, cnt_ref, carry_sc, *, alpha):
    i = pl.program_id(0)

    @pl.when(i == 0)
    def _():
        carry_sc[...] = jnp.zeros(carry_sc.shape, F32)

    mix = jnp.dot(m_ref[...], w_ref[...], preferred_element_type=F32)
    hn = _layer_norm_rows(alpha * h_ref[...] + mix, g_ref[...], b_ref[...])
    h1_ref[...] = hn

    logits = jnp.dot(hn, wr_ref[...], preferred_element_type=F32, precision=lax.Precision.HIGHEST) + br_ref[...]
    tm = logits.shape[0]
    lane = lax.broadcasted_iota(jnp.int32, logits.shape, 1)
    neg = -jnp.inf
    gl = jnp.where(lane < N_GROUPS, logits, neg)
    gmax = jnp.max(gl, axis=1, keepdims=True)
    grp_w = 1.0 / jnp.sum(jnp.exp(gl - gmax), axis=1, keepdims=True)
    gidx = jnp.min(jnp.where(gl == gmax, lane, LANES), axis=1, keepdims=True)
    e_lane = lane - EXPERT_LANE0
    in_grp = (e_lane >= 0) & (e_lane < N_EXPERTS) & ((e_lane >> LOG2_EXPERTS_PER_GROUP) == gidx)
    el = jnp.where(in_grp, logits, neg)
    max1 = jnp.max(el, axis=1, keepdims=True)
    i1 = jnp.min(jnp.where(el == max1, lane, LANES), axis=1, keepdims=True)
    el2 = jnp.where(lane == i1, neg, el)
    max2 = jnp.max(el2, axis=1, keepdims=True)
    i2 = jnp.min(jnp.where(el2 == max2, lane, LANES), axis=1, keepdims=True)
    ratio = jnp.exp(max2 - max1)
    w1 = grp_w / (1.0 + ratio)
    w2 = grp_w * ratio / (1.0 + ratio)
    sel1 = lane == i1
    sel2 = lane == i2
    sel = sel1.astype(F32) + sel2.astype(F32)
    r_i = lax.broadcasted_iota(jnp.int32, (tm, tm), 0)
    c_i = lax.broadcasted_iota(jnp.int32, (tm, tm), 1)
    before = (c_i < r_i).astype(BF16)
    rank = carry_sc[...] + jnp.dot(before, sel.astype(BF16), preferred_element_type=F32)
    new_carry = carry_sc[...] + jnp.sum(sel, axis=0, keepdims=True)
    carry_sc[...] = new_carry
    cnt_ref[...] = jnp.broadcast_to(new_carry, cnt_ref.shape)
    r1 = _lane_pick(sel1, rank)
    r2 = _lane_pick(sel2, rank)
    e1 = (i1 - EXPERT_LANE0).astype(F32)
    e2 = (i2 - EXPERT_LANE0).astype(F32)
    rec = jnp.where(lane == 0, e1, jnp.where(lane == 1, e2, jnp.where(lane == 2, r1, jnp.where(lane == 3, r2, 0.0))))
    route_ref[...] = rec.T[0:8, :]
    wcol_ref[...] = jnp.where(lane == 0, w1, jnp.where(lane == 1, w2, 0.0))


def _outproj_router(merged16, w_out16, h32, g, b, w_router, b_router, alpha):
    n, d = merged16.shape
    tm = min(ROW_TILE, n)
    row16 = pl.BlockSpec((tm, d), lambda i: (i, 0))
    vec = pl.BlockSpec((1, d), lambda i: (0, 0))
    return pl.pallas_call(
        functools.partial(_outproj_router_kernel, alpha=alpha), grid=(n // tm,),
        in_specs=[row16, pl.BlockSpec((d, d), lambda i: (0, 0)), row16, vec, vec,
                  pl.BlockSpec((d, LANES), lambda i: (0, 0)), pl.BlockSpec((1, LANES), lambda i: (0, 0))],
        out_specs=[row16, pl.BlockSpec((8, tm), lambda i: (0, i)),
                   pl.BlockSpec((tm, LANES), lambda i: (i, 0)), pl.BlockSpec((8, LANES), lambda i: (0, 0))],
        out_shape=[jax.ShapeDtypeStruct((n, d), F32), jax.ShapeDtypeStruct((8, n), F32),
                   jax.ShapeDtypeStruct((n, LANES), F32), jax.ShapeDtypeStruct((8, LANES), F32)],
        scratch_shapes=[pltpu.VMEM((1, LANES), F32)],
        compiler_params=_params("arbitrary"), name="outproj_router",
    )(merged16, w_out16, h32, g.reshape(1, d), b.reshape(1, d), w_router, b_router)


def _plan_kernel(route_ref, cnt_ref, dest_ref, tiles_ref, *, tile, log2_tile):
    cnt = cnt_ref[...].astype(jnp.int32)
    padded = ((cnt + (tile - 1)) >> log2_tile) << log2_tile
    lane = lax.broadcasted_iota(jnp.int32, padded.shape, 1)
    end = padded
    k = 1
    while k < LANES:
        end = end + jnp.where(lane >= k, pltpu.roll(end, k, 1), 0)
        k *= 2
    start = end - padded
    rows = EXPERT_LANE0 + N_EXPERTS
    rows = -(-rows // 8) * 8
    def lane_to_sublane(v):
        return jnp.broadcast_to(v[0:1, :].astype(F32), (LANES, LANES)).T[0:rows, 0:1]

    start_col = lane_to_sublane(start)
    end_col = lane_to_sublane(end)
    route = route_ref[...]
    n = route.shape[1]
    sub = lax.broadcasted_iota(jnp.int32, (rows, n), 0)
    e1 = route[0:1, :].astype(jnp.int32) + EXPERT_LANE0
    e2 = route[1:2, :].astype(jnp.int32) + EXPERT_LANE0
    d1 = jnp.sum(jnp.where(sub == e1, start_col, 0.0), axis=0, keepdims=True) + route[2:3, :]
    d2 = jnp.sum(jnp.where(sub == e2, start_col, 0.0), axis=0, keepdims=True) + route[3:4, :]
    row8 = lax.broadcasted_iota(jnp.int32, dest_ref.shape, 0)
    dest_ref[...] = jnp.where(row8 == 0, d1, jnp.where(row8 == 1, d2, 0.0)).astype(jnp.int32)
    nt = tiles_ref.shape[1]
    t_start = (lax.broadcasted_iota(jnp.int32, (rows, nt), 1) * tile).astype(F32)
    sub_t = lax.broadcasted_iota(jnp.int32, (rows, nt), 0)
    is_expert = (sub_t >= EXPERT_LANE0) & (sub_t < EXPERT_LANE0 + N_EXPERTS)
    done = jnp.sum(jnp.where(is_expert & (end_col <= t_start), 1.0, 0.0), axis=0, keepdims=True)
    expert = jnp.minimum(done, float(N_EXPERTS - 1))
    used = jnp.max(end_col, axis=0, keepdims=True) * (1.0 / tile)
    row8t = lax.broadcasted_iota(jnp.int32, tiles_ref.shape, 0)
    tiles_ref[...] = jnp.where(row8t == 0, expert, jnp.broadcast_to(used, tiles_ref.shape)).astype(jnp.int32)


def _plan(route, counts, tile, n_tiles):
    n = route.shape[1]
    nt_pad = -(-n_tiles // LANES) * LANES
    return pl.pallas_call(
        functools.partial(_plan_kernel, tile=tile, log2_tile=tile.bit_length() - 1),
        out_shape=[jax.ShapeDtypeStruct((8, n), jnp.int32), jax.ShapeDtypeStruct((8, nt_pad), jnp.int32)],
        compiler_params=pltpu.CompilerParams(vmem_limit_bytes=VMEM_LIMIT), name="plan",
    )(route, counts)


def _dispatch_kernel(d1_ref, d2_ref, h_hbm, xs_in, xs_hbm, sem, *, tile):
    del xs_in
    base = pl.program_id(0) * tile

    def copies(r):
        n = base + r
        src = h_hbm.at[pl.ds(n, 1), :]
        return (pltpu.make_async_copy(src, xs_hbm.at[pl.ds(d1_ref[n], 1), :], sem),
                pltpu.make_async_copy(src, xs_hbm.at[pl.ds(d2_ref[n], 1), :], sem))

    def start(r, c):
        a, b = copies(r)
        a.start()
        b.start()
        return c

    def wait(r, c):
        a, b = copies(r)
        a.wait()
        b.wait()
        return c

    lax.fori_loop(0, tile, start, 0)
    lax.fori_loop(0, tile, wait, 0)


def _dispatch(d1, d2, h32, n_slots):
    n, d = h32.shape
    tile = min(GATHER_TILE, n)
    xs0 = jnp.zeros((n_slots, d), F32)
    return pl.pallas_call(
        functools.partial(_dispatch_kernel, tile=tile),
        grid_spec=pltpu.PrefetchScalarGridSpec(
            num_scalar_prefetch=2, grid=(n // tile,),
            in_specs=[pl.BlockSpec(memory_space=pl.ANY), pl.BlockSpec(memory_space=pl.ANY)],
            out_specs=pl.BlockSpec(memory_space=pl.ANY),
            scratch_shapes=[pltpu.SemaphoreType.DMA(())]),
        out_shape=jax.ShapeDtypeStruct((n_slots, d), F32),
        input_output_aliases={3: 0},
        compiler_params=pltpu.CompilerParams(dimension_semantics=("arbitrary",), has_side_effects=True),
        name="dispatch",
    )(d1, d2, h32, xs0)


def _expert_kernel(te_ref, nu_ref, x_ref, wg_ref, wu_ref, wd_ref, y_ref):
    t = pl.program_id(0)

    @pl.when(t < nu_ref[0])
    def _():
        x = x_ref[...].astype(BF16)
        gate = jnp.dot(x, wg_ref[0], preferred_element_type=F32)
        up = jnp.dot(x, wu_ref[0], preferred_element_type=F32)
        hid = (gate * jax.nn.sigmoid(gate) * up).astype(BF16)
        y_ref[...] = jnp.dot(hid, wd_ref[0], preferred_element_type=F32)

    @pl.when(t >= nu_ref[0])
    def _():
        y_ref[...] = jnp.zeros(y_ref.shape, F32)


def _experts(tile_expert, n_used, xs, wg16, wu16, wd16, tile):
    p, d = xs.shape
    f = wg16.shape[2]
    return pl.pallas_call(
        _expert_kernel,
        grid_spec=pltpu.PrefetchScalarGridSpec(
            num_scalar_prefetch=2, grid=(p // tile,),
            in_specs=[pl.BlockSpec((tile, d), lambda t, te, nu: (t, 0)),
                      pl.BlockSpec((1, d, f), lambda t, te, nu: (te[t], 0, 0)),
                      pl.BlockSpec((1, d, f), lambda t, te, nu: (te[t], 0, 0)),
                      pl.BlockSpec((1, f, d), lambda t, te, nu: (te[t], 0, 0))],
            out_specs=pl.BlockSpec((tile, d), lambda t, te, nu: (t, 0))),
        out_shape=jax.ShapeDtypeStruct((p, d), F32),
        compiler_params=_params("arbitrary"), name="experts",
    )(tile_expert, n_used, xs, wg16, wu16, wd16)


def _combine_kernel(d1_ref, d2_ref, h_ref, wc_ref, g_ref, b_ref, y_hbm, o32_ref, o16_ref,
                    buf1, buf2, sem, *, tile, alpha):
    base = pl.program_id(0) * tile

    def copies(r):
        n = base + r
        return (pltpu.make_async_copy(y_hbm.at[pl.ds(d1_ref[n], 1), :], buf1.at[pl.ds(r, 1), :], sem),
                pltpu.make_async_copy(y_hbm.at[pl.ds(d2_ref[n], 1), :], buf2.at[pl.ds(r, 1), :], sem))

    def start(r, c):
        a, b = copies(r)
        a.start()
        b.start()
        return c

    def wait(r, c):
        a, b = copies(r)
        a.wait()
        b.wait()
        return c

    lax.fori_loop(0, tile, start, 0)
    lax.fori_loop(0, tile, wait, 0)
    wc = wc_ref[...]
    ffn = wc[:, 0:1] * buf1[...] + wc[:, 1:2] * buf2[...]
    y = _layer_norm_rows(alpha * h_ref[...] + ffn, g_ref[...], b_ref[...])
    o32_ref[...] = y
    o16_ref[...] = y.astype(BF16)


def _combine(d1, d2, h32, wcol, g, b, y, alpha):
    n, d = h32.shape
    tile = min(GATHER_TILE, n)
    row = pl.BlockSpec((tile, d), lambda i, a, c: (i, 0))
    vec = pl.BlockSpec((1, d), lambda i, a, c: (0, 0))
    return pl.pallas_call(
        functools.partial(_combine_kernel, tile=tile, alpha=alpha),
        grid_spec=pltpu.PrefetchScalarGridSpec(
            num_scalar_prefetch=2, grid=(n // tile,),
            in_specs=[row, pl.BlockSpec((tile, LANES), lambda i, a, c: (i, 0)), vec, vec,
                      pl.BlockSpec(memory_space=pl.ANY)],
            out_specs=[row, row],
            scratch_shapes=[pltpu.VMEM((tile, d), F32), pltpu.VMEM((tile, d), F32), pltpu.SemaphoreType.DMA(())]),
        out_shape=[jax.ShapeDtypeStruct((n, d), F32), jax.ShapeDtypeStruct((n, d), BF16)],
        compiler_params=_params("arbitrary"), name="combine_ln",
    )(d1, d2, h32, wcol, g.reshape(1, d), b.reshape(1, d), y)


def _rotary_tables(seq):
    half = RET_QK_DIM // 2
    inv_freq = ROPE_BASE ** (-jnp.arange(half, dtype=F32) / half)
    ang = jnp.arange(seq, dtype=jnp.int32).astype(F32)[:, None] * inv_freq[None, :]
    cos, sin = jnp.cos(ang), jnp.sin(ang)
    return jnp.concatenate([cos, cos], axis=1), jnp.concatenate([-sin, sin], axis=1)


def _col_scale(sizes_scales):
    return jnp.concatenate([jnp.full((size,), scale, F32) for size, scale in sizes_scales])


def kernel(x, ln_in_g, ln_in_b, w_in, b_forget, w_branch_fox, w_branch_ret, w_out, ln1_g, ln1_b,
           w_router_group, b_router_group, w_router_expert, b_router_expert, w_gate, w_up, w_down, ln2_g, ln2_b):
    batch, seq, d = x.shape
    n = batch * seq
    depth = w_in.shape[0]
    alpha = (2.0 * depth) ** 0.25
    fox_w = FOX_HEADS * FOX_HEAD_DIM
    ret_qk = RET_HEADS * RET_QK_DIM
    ret_v = RET_HEADS * RET_V_DIM
    d_expert = w_gate.shape[-1]
    o_ff = 3 * fox_w
    o_rq = o_ff + FOX_HEADS
    o_rv = o_rq + 2 * ret_qk
    o_rg = o_rv + ret_v
    o_gate = o_rg + ret_v
    tables = _rotary_tables(seq)
    fox_scale = _col_scale([(fox_w, FOX_HEAD_DIM ** -0.5), (2 * fox_w, 1.0)])
    rqk_scale = _col_scale([(ret_qk, 1.0), (ret_qk, RET_QK_DIM ** -0.5)])
    ones_v = jnp.ones((ret_v,), F32)
    ones_g = jnp.ones((2 * d,), F32)
    n_tiles = (N_EXPERTS * (EXPERT_TILE - 1) + 2 * n) // EXPERT_TILE + 1
    n_slots = n_tiles * EXPERT_TILE

    h32, h16 = _layer_norm(x.reshape(n, d), ln_in_g, ln_in_b)
    for l in range(depth):
        w = w_in[l]
        qkv = _proj(h16, w[:, :o_ff].astype(BF16), fox_scale, "none", seq)
        cum = _forget_cumsum(h16, w[:, o_ff:o_rq].T.astype(BF16), b_forget[l], batch, seq)
        rqk = _proj(h16, w[:, o_rq:o_rv].astype(BF16), rqk_scale, "rotary", seq, tables)
        rv = _proj(h16, w[:, o_rv:o_rg].astype(BF16), ones_v, "none", seq)
        rg = _proj(h16, w[:, o_rg:o_gate].astype(BF16), ones_v, "silu", seq)
        gates = _proj(h16, w[:, o_gate:].astype(BF16), ones_g, "sigmoid", seq)
        y_fox = _fox_attention(qkv, cum, batch, seq)
        y_ret = _retention(rqk, rv, rg, batch, seq)
        merged = _merge(y_fox, y_ret, w_branch_fox[l].astype(BF16), w_branch_ret[l].astype(BF16), gates)

        w_router = jnp.zeros((d, LANES), F32)
        w_router = w_router.at[:, :N_GROUPS].set(w_router_group[l])
        w_router = w_router.at[:, EXPERT_LANE0:EXPERT_LANE0 + N_EXPERTS].set(w_router_expert[l])
        b_router = jnp.zeros((1, LANES), F32)
        b_router = b_router.at[0, :N_GROUPS].set(b_router_group[l])
        b_router = b_router.at[0, EXPERT_LANE0:EXPERT_LANE0 + N_EXPERTS].set(b_router_expert[l])
        h1, route, wcol, counts = _outproj_router(merged, w_out[l].astype(BF16), h32, ln1_g[l], ln1_b[l],
                                                  w_router, b_router, alpha)
        dest, tiles = _plan(route, counts, EXPERT_TILE, n_tiles)
        d1, d2 = dest[0], dest[1]
        xs = _dispatch(d1, d2, h1, n_slots)
        y = _experts(tiles[0, :n_tiles], tiles[1, :1], xs,
                     w_gate[l].reshape(N_EXPERTS, d, d_expert).astype(BF16),
                     w_up[l].reshape(N_EXPERTS, d, d_expert).astype(BF16),
                     w_down[l].reshape(N_EXPERTS, d_expert, d).astype(BF16), EXPERT_TILE)
        h32, h16 = _combine(d1, d2, h1, wcol, ln2_g[l], ln2_b[l], y, alpha)
    return h32.reshape(batch, seq, d)
```

```python
import functools

import jax
import jax.numpy as jnp
from jax import lax
from jax.experimental import pallas as pl
from jax.experimental.pallas import tpu as pltpu

F32 = jnp.float32
BF16 = jnp.bfloat16

CHUNK = 64
FOX_HEADS = 8
FOX_HEAD_DIM = 128
RET_HEADS = 8
RET_QK_DIM = 128
RET_V_DIM = 256
ROPE_BASE = 10000.0
N_GROUPS = 4
EXPERTS_PER_GROUP = 8
N_EXPERTS = N_GROUPS * EXPERTS_PER_GROUP
LOG2_EXPERTS_PER_GROUP = EXPERTS_PER_GROUP.bit_length() - 1
LN_EPS = 1e-5
RMS_EPS = 1e-6

LANES = 128
EXPERT_LANE0 = N_GROUPS
VMEM_LIMIT = 56 * 1024 * 1024

ROW_TILE = 512
ATTN_TILE = 256
RET_TILE = 256
EXPERT_TILE = 256
GATHER_TILE = 256


def _params(*sem):
    return pltpu.CompilerParams(dimension_semantics=sem, vmem_limit_bytes=VMEM_LIMIT)


def _layer_norm_rows(x, g, b):
    mu = jnp.mean(x, axis=-1, keepdims=True)
    xc = x - mu
    var = jnp.mean(xc * xc, axis=-1, keepdims=True)
    return xc * lax.rsqrt(var + LN_EPS) * g + b


def _ln_kernel(x_ref, g_ref, b_ref, o32_ref, o16_ref):
    y = _layer_norm_rows(x_ref[...], g_ref[...], b_ref[...])
    o32_ref[...] = y
    o16_ref[...] = y.astype(BF16)


def _layer_norm(x, g, b):
    n, d = x.shape
    tm = min(ROW_TILE, n)
    row = pl.BlockSpec((tm, d), lambda i: (i, 0))
    vec = pl.BlockSpec((1, d), lambda i: (0, 0))
    return pl.pallas_call(
        _ln_kernel, grid=(n // tm,), in_specs=[row, vec, vec], out_specs=[row, row],
        out_shape=[jax.ShapeDtypeStruct((n, d), F32), jax.ShapeDtypeStruct((n, d), BF16)],
        compiler_params=_params("parallel"), name="ln_in",
    )(x, g.reshape(1, d), b.reshape(1, d))


def _proj_kernel(x_ref, w_ref, s_ref, *rest, act):
    acc = jnp.dot(x_ref[...], w_ref[...], preferred_element_type=F32) * s_ref[...]
    if act == "rotary":
        cos_ref, sin_ref, o_ref = rest
        cos, sin = cos_ref[...], sin_ref[...]
        for c in range(acc.shape[1] // LANES):
            blk = acc[:, c * LANES:(c + 1) * LANES]
            rot = blk * cos + pltpu.roll(blk, LANES // 2, 1) * sin
            o_ref[:, c * LANES:(c + 1) * LANES] = rot.astype(o_ref.dtype)
        return
    (o_ref,) = rest
    if act == "silu":
        acc = acc * jax.nn.sigmoid(acc)
    elif act == "sigmoid":
        acc = jax.nn.sigmoid(acc)
    o_ref[...] = acc.astype(o_ref.dtype)


def _proj(x16, w16, colscale, act, seq, tables=None, tn=512):
    n, k = x16.shape
    m = w16.shape[1]
    tm = min(ROW_TILE, n)
    in_specs = [pl.BlockSpec((tm, k), lambda i, j: (i, 0)),
                pl.BlockSpec((k, tn), lambda i, j: (0, j)),
                pl.BlockSpec((1, tn), lambda i, j: (0, j))]
    args = [x16, w16, colscale.reshape(1, m)]
    if act == "rotary":
        per_seq = seq // tm
        tab = pl.BlockSpec((tm, LANES), lambda i, j: (i % per_seq, 0))
        in_specs += [tab, tab]
        args += list(tables)
    return pl.pallas_call(
        functools.partial(_proj_kernel, act=act), grid=(n // tm, m // tn),
        in_specs=in_specs, out_specs=pl.BlockSpec((tm, tn), lambda i, j: (i, j)),
        out_shape=jax.ShapeDtypeStruct((n, m), BF16),
        compiler_params=_params("parallel", "parallel"), name="proj_" + act,
    )(*args)


def _forget_kernel(h_ref, w_ref, b_ref, o_ref, *, heads, block):
    z = jnp.dot(h_ref[...], w_ref[...], preferred_element_type=F32) + b_ref[...]
    lf = jnp.minimum(z, 0.0) - jnp.log1p(jnp.exp(-jnp.abs(z)))
    seq = lf.shape[0]
    r_i = lax.broadcasted_iota(jnp.int32, (block, block), 0)
    c_i = lax.broadcasted_iota(jnp.int32, (block, block), 1)
    upto = (c_i <= r_i).astype(F32)
    lane = lax.broadcasted_iota(jnp.int32, (block, LANES), 1)
    carry = jnp.zeros((1, LANES), F32)
    for blk in range(seq // block):
        rows = slice(blk * block, (blk + 1) * block)
        cum = jnp.dot(upto, lf[rows], preferred_element_type=F32, precision=lax.Precision.HIGHEST) + carry
        carry = cum[block - 1:block, :]
        for h in range(heads):
            col = jnp.broadcast_to(cum[:, h:h + 1], (block, LANES))
            hi = col.astype(BF16).astype(F32)
            mid = (col - hi).astype(BF16).astype(F32)
            lo = col - hi - mid
            split = jnp.where(lane == 0, hi, jnp.where(lane == 1, mid, jnp.where(lane == 2, lo, 0.0)))
            o_ref[rows, h * LANES:(h + 1) * LANES] = split.astype(BF16)


def _forget_cumsum(h16, w_forget, b_forget, batch, seq):
    d, heads = w_forget.shape
    w_pad = jnp.zeros((d, LANES), BF16).at[:, :heads].set(w_forget.astype(BF16))
    b_pad = jnp.zeros((1, LANES), F32).at[0, :heads].set(b_forget.astype(F32))
    return pl.pallas_call(
        functools.partial(_forget_kernel, heads=heads, block=min(256, seq)), grid=(batch,),
        in_specs=[pl.BlockSpec((seq, d), lambda b: (b, 0)),
                  pl.BlockSpec((d, LANES), lambda b: (0, 0)),
                  pl.BlockSpec((1, LANES), lambda b: (0, 0))],
        out_specs=pl.BlockSpec((seq, heads * LANES), lambda b: (b, 0)),
        out_shape=jax.ShapeDtypeStruct((batch * seq, heads * LANES), BF16),
        compiler_params=_params("parallel"), name="forget_cumsum",
    )(h16, w_pad, b_pad)


N_SPLIT = 3


def _fox_kernel(q_ref, k_ref, v_ref, a_ref, o_ref, acc_sc, *, tile, hp):
    qi = pl.program_id(2)
    dh = FOX_HEAD_DIM
    lane = lax.broadcasted_iota(jnp.int32, (tile, dh), 1)
    q_extra = jnp.where(lane < N_SPLIT, -1.0, 0.0).astype(BF16)
    q_aug = [jnp.concatenate([q_ref[:, hh * dh:(hh + 1) * dh], q_extra], axis=1) for hh in range(hp)]
    acc_sc[...] = jnp.zeros(acc_sc.shape, F32)

    def block(j, stats, diagonal):
        rows = pl.ds(pl.multiple_of(j * tile, tile), tile)
        cols = [slice(hh * dh, (hh + 1) * dh) for hh in range(hp)]
        s_t = []
        for hh in range(hp):
            k_aug = jnp.concatenate([k_ref[rows, cols[hh]], a_ref[rows, cols[hh]]], axis=1)
            s = lax.dot_general(k_aug, q_aug[hh], (((1,), (1,)), ((), ())), preferred_element_type=F32)
            if diagonal:
                key = lax.broadcasted_iota(jnp.int32, s.shape, 0)
                qry = lax.broadcasted_iota(jnp.int32, s.shape, 1)
                s = jnp.where(key <= qry, s, -jnp.inf)
            s_t.append(s)
        out, scale, pv = [], [], []
        for hh in range(hp):
            m_prev, l_prev = stats[hh]
            m_new = jnp.maximum(m_prev, jnp.max(s_t[hh], axis=0, keepdims=True))
            a = jnp.exp(m_prev - m_new)
            p = jnp.exp(s_t[hh] - m_new)
            out.append((m_new, a * l_prev + jnp.sum(p, axis=0, keepdims=True)))
            scale.append(a)
            pv.append(lax.dot_general(v_ref[rows, cols[hh]], p.astype(BF16), (((0,), (0,)), ((), ())),
                                      preferred_element_type=F32))
        for hh in range(hp):
            acc_sc[hh] = scale[hh] * acc_sc[hh] + pv[hh]
        return tuple(out)

    init = tuple((jnp.full((1, tile), -jnp.inf, F32), jnp.zeros((1, tile), F32)) for _ in range(hp))
    stats = lax.fori_loop(0, qi, lambda j, st: block(j, st, False), init)
    stats = block(qi, stats, True)
    for hh in range(hp):
        o_ref[:, hh * dh:(hh + 1) * dh] = (acc_sc[hh] / stats[hh][1]).T.astype(o_ref.dtype)


def _fox_attention(qkv16, aug16, batch, seq, hp=8):
    heads, dh = FOX_HEADS, FOX_HEAD_DIM
    tile = min(ATTN_TILE, seq)
    nq = seq // tile
    groups = heads // hp
    w = hp * dh
    return pl.pallas_call(
        functools.partial(_fox_kernel, tile=tile, hp=hp), grid=(batch, groups, nq),
        in_specs=[pl.BlockSpec((tile, w), lambda b, g, i: (b * nq + i, g)),
                  pl.BlockSpec((seq, w), lambda b, g, i: (b, groups + g)),
                  pl.BlockSpec((seq, w), lambda b, g, i: (b, 2 * groups + g)),
                  pl.BlockSpec((seq, w), lambda b, g, i: (b, g))],
        out_specs=pl.BlockSpec((tile, w), lambda b, g, i: (b * nq + i, g)),
        out_shape=jax.ShapeDtypeStruct((batch * seq, heads * dh), BF16),
        scratch_shapes=[pltpu.VMEM((hp, dh, tile), F32)],
        compiler_params=_params("parallel", "parallel", "parallel"), name="fox_attention",
    )(qkv16, qkv16, qkv16, aug16)


def _ret_kernel(q_ref, k_ref, v_ref, g_ref, dm_ref, qd_ref, kd_ref, cd_ref, o_ref, st_sc, *, hp):
    @pl.when(pl.program_id(2) == 0)
    def _():
        st_sc[...] = jnp.zeros(st_sc.shape, F32)

    dk, dv = RET_QK_DIM, RET_V_DIM
    for hh in range(hp):
        q = q_ref[:, hh * dk:(hh + 1) * dk]
        k = k_ref[:, hh * dk:(hh + 1) * dk]
        v = v_ref[:, hh * dv:(hh + 1) * dv]
        scores = lax.dot_general(q, k, (((1,), (1,)), ((), ())), preferred_element_type=F32) * dm_ref[hh]
        intra = jnp.dot(scores.astype(BF16), v, preferred_element_type=F32)
        state = st_sc[hh]
        cross = jnp.dot(q, state.astype(BF16), preferred_element_type=F32) * qd_ref[hh]
        k_dec = (k.astype(F32) * kd_ref[hh]).astype(BF16)
        st_sc[hh] = state * cd_ref[hh] + lax.dot_general(
            k_dec, v, (((0,), (0,)), ((), ())), preferred_element_type=F32)
        o = intra + cross
        o = o * lax.rsqrt(jnp.mean(o * o, axis=-1, keepdims=True) + RMS_EPS)
        o_ref[:, hh * dv:(hh + 1) * dv] = (g_ref[:, hh * dv:(hh + 1) * dv].astype(F32) * o).astype(o_ref.dtype)


def _retention_tables(tile):
    heads = RET_HEADS
    log_gamma = jnp.log1p(-(2.0 ** (-5.0 - jnp.arange(heads, dtype=F32))))
    pos = jnp.arange(tile, dtype=F32)
    chunk = jnp.arange(tile, dtype=jnp.int32) // CHUNK
    dist = pos[:, None] - pos[None, :]
    same = chunk[:, None] == chunk[None, :]
    earlier = chunk[None, :] < chunk[:, None]
    expo = jnp.where(same, jnp.abs(dist), dist)
    dmask = jnp.where((same | earlier)[None], jnp.exp(log_gamma[:, None, None] * expo[None]), 0.0)
    q_dec = jnp.exp(log_gamma[:, None] * (pos + 1.0))[:, :, None]
    k_dec = jnp.exp(log_gamma[:, None] * (tile - 1 - pos))[:, :, None]
    c_dec = jnp.broadcast_to(jnp.exp(log_gamma * tile)[:, None, None], (heads, 1, RET_V_DIM))
    return dmask, q_dec, k_dec, c_dec


def _retention(rqk16, rv16, rg16, batch, seq, hp=4):
    heads, dk, dv = RET_HEADS, RET_QK_DIM, RET_V_DIM
    tile = min(RET_TILE, seq)
    nt = seq // tile
    groups = heads // hp
    dmask, q_dec, k_dec, c_dec = _retention_tables(tile)
    val = pl.BlockSpec((tile, hp * dv), lambda b, g, t: (b * nt + t, g))
    return pl.pallas_call(
        functools.partial(_ret_kernel, hp=hp), grid=(batch, groups, nt),
        in_specs=[pl.BlockSpec((tile, hp * dk), lambda b, g, t: (b * nt + t, g)),
                  pl.BlockSpec((tile, hp * dk), lambda b, g, t: (b * nt + t, groups + g)),
                  val, val,
                  pl.BlockSpec((hp, tile, tile), lambda b, g, t: (g, 0, 0)),
                  pl.BlockSpec((hp, tile, 1), lambda b, g, t: (g, 0, 0)),
                  pl.BlockSpec((hp, tile, 1), lambda b, g, t: (g, 0, 0)),
                  pl.BlockSpec((hp, 1, dv), lambda b, g, t: (g, 0, 0))],
        out_specs=val,
        out_shape=jax.ShapeDtypeStruct((batch * seq, heads * dv), BF16),
        scratch_shapes=[pltpu.VMEM((hp, dk, dv), F32)],
        compiler_params=_params("parallel", "parallel", "arbitrary"), name="retention",
    )(rqk16, rqk16, rv16, rg16, dmask, q_dec, k_dec, c_dec)


def _merge_kernel(yf_ref, yr_ref, wf_ref, wr_ref, gf_ref, gr_ref, o_ref):
    fox = jnp.dot(yf_ref[...], wf_ref[...], preferred_element_type=F32)
    ret = jnp.dot(yr_ref[...], wr_ref[...], preferred_element_type=F32)
    o_ref[...] = (gf_ref[...].astype(F32) * fox + gr_ref[...].astype(F32) * ret).astype(o_ref.dtype)


def _merge(y_fox, y_ret, w_fox16, w_ret16, gates16, tn=512):
    n, kf = y_fox.shape
    kr = y_ret.shape[1]
    d = w_fox16.shape[1]
    tm = min(ROW_TILE, n)
    nj = d // tn
    return pl.pallas_call(
        _merge_kernel, grid=(n // tm, nj),
        in_specs=[pl.BlockSpec((tm, kf), lambda i, j: (i, 0)),
                  pl.BlockSpec((tm, kr), lambda i, j: (i, 0)),
                  pl.BlockSpec((kf, tn), lambda i, j: (0, j)),
                  pl.BlockSpec((kr, tn), lambda i, j: (0, j)),
                  pl.BlockSpec((tm, tn), lambda i, j: (i, j)),
                  pl.BlockSpec((tm, tn), lambda i, j: (i, nj + j))],
        out_specs=pl.BlockSpec((tm, tn), lambda i, j: (i, j)),
        out_shape=jax.ShapeDtypeStruct((n, d), BF16),
        compiler_params=_params("parallel", "parallel"), name="merge",
    )(y_fox, y_ret, w_fox16, w_ret16, gates16, gates16)


def _lane_pick(mask, values):
    return jnp.sum(jnp.where(mask, values, 0.0), axis=1, keepdims=True)


def _outproj_router_kernel(m_ref, w_ref, h_ref, g_ref, b_ref, wr_ref, br_ref,
                           h1_ref, route_ref, wcol_ref, cnt_ref, carry_sc, *, alpha):
    i = pl.program_id(0)

    @pl.when(i == 0)
    def _():
        carry_sc[...] = jnp.zeros(carry_sc.shape, F32)

    mix = jnp.dot(m_ref[...], w_ref[...], preferred_element_type=F32)
    hn = _layer_norm_rows(alpha * h_ref[...] + mix, g_ref[...], b_ref[...])
    h1_ref[...] = hn

    logits = jnp.dot(hn, wr_ref[...], preferred_element_type=F32, precision=lax.Precision.HIGHEST) + br_ref[...]
    tm = logits.shape[0]
    lane = lax.broadcasted_iota(jnp.int32, logits.shape, 1)
    neg = -jnp.inf
    gl = jnp.where(lane < N_GROUPS, logits, neg)
    gmax = jnp.max(gl, axis=1, keepdims=True)
    grp_w = 1.0 / jnp.sum(jnp.exp(gl - gmax), axis=1, keepdims=True)
    gidx = jnp.min(jnp.where(gl == gmax, lane, LANES), axis=1, keepdims=True)
    e_lane = lane - EXPERT_LANE0
    in_grp = (e_lane >= 0) & (e_lane < N_EXPERTS) & ((e_lane >> LOG2_EXPERTS_PER_GROUP) == gidx)
    el = jnp.where(in_grp, logits, neg)
    max1 = jnp.max(el, axis=1, keepdims=True)
    i1 = jnp.min(jnp.where(el == max1, lane, LANES), axis=1, keepdims=True)
    el2 = jnp.where(lane == i1, neg, el)
    max2 = jnp.max(el2, axis=1, keepdims=True)
    i2 = jnp.min(jnp.where(el2 == max2, lane, LANES), axis=1, keepdims=True)
    ratio = jnp.exp(max2 - max1)
    w1 = grp_w / (1.0 + ratio)
    w2 = grp_w * ratio / (1.0 + ratio)
    sel1 = lane == i1
    sel2 = lane == i2
    sel = sel1.astype(F32) + sel2.astype(F32)
    r_i = lax.broadcasted_iota(jnp.int32, (tm, tm), 0)
    c_i = lax.broadcasted_iota(jnp.int32, (tm, tm), 1)
    before = (c_i < r_i).astype(BF16)
    rank = carry_sc[...] + jnp.dot(before, sel.astype(BF16), preferred_element_type=F32)
    new_carry = carry_sc[...] + jnp.sum(sel, axis=0, keepdims=True)
    carry_sc[...] = new_carry
    cnt_ref[...] = jnp.broadcast_to(new_carry, cnt_ref.shape)
    r1 = _lane_pick(sel1, rank)
    r2 = _lane_pick(sel2, rank)
    e1 = (i1 - EXPERT_LANE0).astype(F32)
    e2 = (i2 - EXPERT_LANE0).astype(F32)
    rec = jnp.where(lane == 0, e1, jnp.where(lane == 1, e2, jnp.where(lane == 2, r1, jnp.where(lane == 3, r2, 0.0))))
    route_ref[...] = rec.T[0:8, :]
    wcol_ref[...] = jnp.where(lane == 0, w1, jnp.where(lane == 1, w2, 0.0))


def _outproj_router(merged16, w_out16, h32, g, b, w_router, b_router, alpha):
    n, d = merged16.shape
    tm = min(ROW_TILE, n)
    row16 = pl.BlockSpec((tm, d), lambda i: (i, 0))
    vec = pl.BlockSpec((1, d), lambda i: (0, 0))
    return pl.pallas_call(
        functools.partial(_outproj_router_kernel, alpha=alpha), grid=(n // tm,),
        in_specs=[row16, pl.BlockSpec((d, d), lambda i: (0, 0)), row16, vec, vec,
                  pl.BlockSpec((d, LANES), lambda i: (0, 0)), pl.BlockSpec((1, LANES), lambda i: (0, 0))],
        out_specs=[row16, pl.BlockSpec((8, tm), lambda i: (0, i)),
                   pl.BlockSpec((tm, LANES), lambda i: (i, 0)), pl.BlockSpec((8, LANES), lambda i: (0, 0))],
        out_shape=[jax.ShapeDtypeStruct((n, d), F32), jax.ShapeDtypeStruct((8, n), F32),
                   jax.ShapeDtypeStruct((n, LANES), F32), jax.ShapeDtypeStruct((8, LANES), F32)],
        scratch_shapes=[pltpu.VMEM((1, LANES), F32)],
        compiler_params=_params("arbitrary"), name="outproj_router",
    )(merged16, w_out16, h32, g.reshape(1, d), b.reshape(1, d), w_router, b_router)


def _plan_kernel(route_ref, cnt_ref, dest_ref, tiles_ref, *, tile, log2_tile):
    cnt = cnt_ref[...].astype(jnp.int32)
    padded = ((cnt + (tile - 1)) >> log2_tile) << log2_tile
    lane = lax.broadcasted_iota(jnp.int32, padded.shape, 1)
    end = padded
    k = 1
    while k < LANES:
        end = end + jnp.where(lane >= k, pltpu.roll(end, k, 1), 0)
        k *= 2
    start = end - padded
    rows = EXPERT_LANE0 + N_EXPERTS
    rows = -(-rows // 8) * 8
    def lane_to_sublane(v):
        return jnp.broadcast_to(v[0:1, :].astype(F32), (LANES, LANES)).T[0:rows, 0:1]

    start_col = lane_to_sublane(start)
    end_col = lane_to_sublane(end)
    route = route_ref[...]
    n = route.shape[1]
    sub = lax.broadcasted_iota(jnp.int32, (rows, n), 0)
    e1 = route[0:1, :].astype(jnp.int32) + EXPERT_LANE0
    e2 = route[1:2, :].astype(jnp.int32) + EXPERT_LANE0
    d1 = jnp.sum(jnp.where(sub == e1, start_col, 0.0), axis=0, keepdims=True) + route[2:3, :]
    d2 = jnp.sum(jnp.where(sub == e2, start_col, 0.0), axis=0, keepdims=True) + route[3:4, :]
    row8 = lax.broadcasted_iota(jnp.int32, dest_ref.shape, 0)
    dest_ref[...] = jnp.where(row8 == 0, d1, jnp.where(row8 == 1, d2, 0.0)).astype(jnp.int32)
    nt = tiles_ref.shape[1]
    t_start = (lax.broadcasted_iota(jnp.int32, (rows, nt), 1) * tile).astype(F32)
    sub_t = lax.broadcasted_iota(jnp.int32, (rows, nt), 0)
    is_expert = (sub_t >= EXPERT_LANE0) & (sub_t < EXPERT_LANE0 + N_EXPERTS)
    done = jnp.sum(jnp.where(is_expert & (end_col <= t_start), 1.0, 0.0), axis=0, keepdims=True)
    expert = jnp.minimum(done, float(N_EXPERTS - 1))
    used = jnp.max(end_col, axis=0, keepdims=True) * (1.0 / tile)
    row8t = lax.broadcasted_iota(jnp.int32, tiles_ref.shape, 0)
    tiles_ref[...] = jnp.where(row8t == 0, expert, jnp.broadcast_to(used, tiles_ref.shape)).astype(jnp.int32)


def _plan(route, counts, tile, n_tiles):
    n = route.shape[1]
    nt_pad = -(-n_tiles // LANES) * LANES
    return pl.pallas_call(
        functools.partial(_plan_kernel, tile=tile, log2_tile=tile.bit_length() - 1),
        out_shape=[jax.ShapeDtypeStruct((8, n), jnp.int32), jax.ShapeDtypeStruct((8, nt_pad), jnp.int32)],
        compiler_params=pltpu.CompilerParams(vmem_limit_bytes=VMEM_LIMIT), name="plan",
    )(route, counts)


def _dispatch_kernel(d1_ref, d2_ref, h_ref, xs_in, xs_hbm, sem, *, tile):
    del xs_in
    base = pl.program_id(0) * tile

    def copies(r):
        n = base + r
        src = h_ref.at[pl.ds(r, 1), :]
        return (pltpu.make_async_copy(src, xs_hbm.at[pl.ds(d1_ref[n], 1), :], sem),
                pltpu.make_async_copy(src, xs_hbm.at[pl.ds(d2_ref[n], 1), :], sem))

    def start(r, c):
        a, b = copies(r)
        a.start()
        b.start()
        return c

    def wait(r, c):
        a, b = copies(r)
        a.wait()
        b.wait()
        return c

    lax.fori_loop(0, tile, start, 0)
    lax.fori_loop(0, tile, wait, 0)


def _dispatch(d1, d2, h32, n_slots):
    n, d = h32.shape
    tile = min(GATHER_TILE, n)
    xs0 = jnp.zeros((n_slots, d), F32)
    return pl.pallas_call(
        functools.partial(_dispatch_kernel, tile=tile),
        grid_spec=pltpu.PrefetchScalarGridSpec(
            num_scalar_prefetch=2, grid=(n // tile,),
            in_specs=[pl.BlockSpec((tile, d), lambda i, a, c: (i, 0)), pl.BlockSpec(memory_space=pl.ANY)],
            out_specs=pl.BlockSpec(memory_space=pl.ANY),
            scratch_shapes=[pltpu.SemaphoreType.DMA(())]),
        out_shape=jax.ShapeDtypeStruct((n_slots, d), F32),
        input_output_aliases={3: 0},
        compiler_params=pltpu.CompilerParams(dimension_semantics=("arbitrary",), has_side_effects=True,
                                             vmem_limit_bytes=VMEM_LIMIT),
        name="dispatch",
    )(d1, d2, h32, xs0)


def _expert_kernel(te_ref, nu_ref, x_ref, wg_ref, wu_ref, wd_ref, y_ref):
    t = pl.program_id(0)

    @pl.when(t < nu_ref[0])
    def _():
        x = x_ref[...].astype(BF16)
        gate = jnp.dot(x, wg_ref[0], preferred_element_type=F32)
        up = jnp.dot(x, wu_ref[0], preferred_element_type=F32)
        hid = (gate * jax.nn.sigmoid(gate) * up).astype(BF16)
        y_ref[...] = jnp.dot(hid, wd_ref[0], preferred_element_type=F32)

    @pl.when(t >= nu_ref[0])
    def _():
        y_ref[...] = jnp.zeros(y_ref.shape, F32)


def _experts(tile_expert, n_used, xs, wg16, wu16, wd16, tile):
    p, d = xs.shape
    f = wg16.shape[2]
    return pl.pallas_call(
        _expert_kernel,
        grid_spec=pltpu.PrefetchScalarGridSpec(
            num_scalar_prefetch=2, grid=(p // tile,),
            in_specs=[pl.BlockSpec((tile, d), lambda t, te, nu: (t, 0)),
                      pl.BlockSpec((1, d, f), lambda t, te, nu: (te[t], 0, 0)),
                      pl.BlockSpec((1, d, f), lambda t, te, nu: (te[t], 0, 0)),
                      pl.BlockSpec((1, f, d), lambda t, te, nu: (te[t], 0, 0))],
            out_specs=pl.BlockSpec((tile, d), lambda t, te, nu: (t, 0))),
        out_shape=jax.ShapeDtypeStruct((p, d), F32),
        compiler_params=_params("arbitrary"), name="experts",
    )(tile_expert, n_used, xs, wg16, wu16, wd16)


def _combine_kernel(d1_ref, d2_ref, h_ref, wc_ref, g_ref, b_ref, y_hbm, o32_ref, o16_ref,
                    buf1, buf2, sem, *, tile, alpha):
    base = pl.program_id(0) * tile

    def copies(r):
        n = base + r
        return (pltpu.make_async_copy(y_hbm.at[pl.ds(d1_ref[n], 1), :], buf1.at[pl.ds(r, 1), :], sem),
                pltpu.make_async_copy(y_hbm.at[pl.ds(d2_ref[n], 1), :], buf2.at[pl.ds(r, 1), :], sem))

    def start(r, c):
        a, b = copies(r)
        a.start()
        b.start()
        return c

    def wait(r, c):
        a, b = copies(r)
        a.wait()
        b.wait()
        return c

    lax.fori_loop(0, tile, start, 0)
    lax.fori_loop(0, tile, wait, 0)
    wc = wc_ref[...]
    ffn = wc[:, 0:1] * buf1[...] + wc[:, 1:2] * buf2[...]
    y = _layer_norm_rows(alpha * h_ref[...] + ffn, g_ref[...], b_ref[...])
    o32_ref[...] = y
    o16_ref[...] = y.astype(BF16)


def _combine(d1, d2, h32, wcol, g, b, y, alpha):
    n, d = h32.shape
    tile = min(GATHER_TILE, n)
    row = pl.BlockSpec((tile, d), lambda i, a, c: (i, 0))
    vec = pl.BlockSpec((1, d), lambda i, a, c: (0, 0))
    return pl.pallas_call(
        functools.partial(_combine_kernel, tile=tile, alpha=alpha),
        grid_spec=pltpu.PrefetchScalarGridSpec(
            num_scalar_prefetch=2, grid=(n // tile,),
            in_specs=[row, pl.BlockSpec((tile, LANES), lambda i, a, c: (i, 0)), vec, vec,
                      pl.BlockSpec(memory_space=pl.ANY)],
            out_specs=[row, row],
            scratch_shapes=[pltpu.VMEM((tile, d), F32), pltpu.VMEM((tile, d), F32), pltpu.SemaphoreType.DMA(())]),
        out_shape=[jax.ShapeDtypeStruct((n, d), F32), jax.ShapeDtypeStruct((n, d), BF16)],
        compiler_params=_params("arbitrary"), name="combine_ln",
    )(d1, d2, h32, wcol, g.reshape(1, d), b.reshape(1, d), y)


def _rotary_tables(seq):
    half = RET_QK_DIM // 2
    inv_freq = ROPE_BASE ** (-jnp.arange(half, dtype=F32) / half)
    ang = jnp.arange(seq, dtype=jnp.int32).astype(F32)[:, None] * inv_freq[None, :]
    cos, sin = jnp.cos(ang), jnp.sin(ang)
    return jnp.concatenate([cos, cos], axis=1), jnp.concatenate([-sin, sin], axis=1)


def _col_scale(sizes_scales):
    return jnp.concatenate([jnp.full((size,), scale, F32) for size, scale in sizes_scales])


def kernel(x, ln_in_g, ln_in_b, w_in, b_forget, w_branch_fox, w_branch_ret, w_out, ln1_g, ln1_b,
           w_router_group, b_router_group, w_router_expert, b_router_expert, w_gate, w_up, w_down, ln2_g, ln2_b):
    batch, seq, d = x.shape
    n = batch * seq
    depth = w_in.shape[0]
    alpha = (2.0 * depth) ** 0.25
    fox_w = FOX_HEADS * FOX_HEAD_DIM
    ret_qk = RET_HEADS * RET_QK_DIM
    ret_v = RET_HEADS * RET_V_DIM
    d_expert = w_gate.shape[-1]
    o_ff = 3 * fox_w
    o_rq = o_ff + FOX_HEADS
    o_rv = o_rq + 2 * ret_qk
    o_rg = o_rv + ret_v
    o_gate = o_rg + ret_v
    tables = _rotary_tables(seq)
    fox_scale = _col_scale([(fox_w, FOX_HEAD_DIM ** -0.5), (2 * fox_w, 1.0)])
    rqk_scale = _col_scale([(ret_qk, 1.0), (ret_qk, RET_QK_DIM ** -0.5)])
    ones_v = jnp.ones((ret_v,), F32)
    ones_g = jnp.ones((2 * d,), F32)
    n_tiles = (N_EXPERTS * (EXPERT_TILE - 1) + 2 * n) // EXPERT_TILE + 1
    n_slots = n_tiles * EXPERT_TILE

    h32, h16 = _layer_norm(x.reshape(n, d), ln_in_g, ln_in_b)
    for l in range(depth):
        w = w_in[l]
        qkv = _proj(h16, w[:, :o_ff].astype(BF16), fox_scale, "none", seq)
        aug = _forget_cumsum(h16, w[:, o_ff:o_rq], b_forget[l], batch, seq)
        rqk = _proj(h16, w[:, o_rq:o_rv].astype(BF16), rqk_scale, "rotary", seq, tables)
        rv = _proj(h16, w[:, o_rv:o_rg].astype(BF16), ones_v, "none", seq)
        rg = _proj(h16, w[:, o_rg:o_gate].astype(BF16), ones_v, "silu", seq)
        gates = _proj(h16, w[:, o_gate:].astype(BF16), ones_g, "sigmoid", seq)
        y_fox = _fox_attention(qkv, aug, batch, seq)
        y_ret = _retention(rqk, rv, rg, batch, seq)
        merged = _merge(y_fox, y_ret, w_branch_fox[l].astype(BF16), w_branch_ret[l].astype(BF16), gates)

        w_router = jnp.zeros((d, LANES), F32)
        w_router = w_router.at[:, :N_GROUPS].set(w_router_group[l])
        w_router = w_router.at[:, EXPERT_LANE0:EXPERT_LANE0 + N_EXPERTS].set(w_router_expert[l])
        b_router = jnp.zeros((1, LANES), F32)
        b_router = b_router.at[0, :N_GROUPS].set(b_router_group[l])
        b_router = b_router.at[0, EXPERT_LANE0:EXPERT_LANE0 + N_EXPERTS].set(b_router_expert[l])
        h1, route, wcol, counts = _outproj_router(merged, w_out[l].astype(BF16), h32, ln1_g[l], ln1_b[l],
                                                  w_router, b_router, alpha)
        dest, tiles = _plan(route, counts, EXPERT_TILE, n_tiles)
        d1, d2 = dest[0], dest[1]
        xs = _dispatch(d1, d2, h1, n_slots)
        y = _experts(tiles[0, :n_tiles], tiles[1, :1], xs,
                     w_gate[l].reshape(N_EXPERTS, d, d_expert).astype(BF16),
                     w_up[l].reshape(N_EXPERTS, d, d_expert).astype(BF16),
                     w_down[l].reshape(N_EXPERTS, d_expert, d).astype(BF16), EXPERT_TILE)
        h32, h16 = _combine(d1, d2, h1, wcol, ln2_g[l], ln2_b[l], y, alpha)
    return h32.reshape(batch, seq, d)
```

```python
import functools

import jax
import jax.numpy as jnp
from jax import lax
from jax.experimental import pallas as pl
from jax.experimental.pallas import tpu as pltpu

F32 = jnp.float32
BF16 = jnp.bfloat16

CHUNK = 64
FOX_HEADS = 8
FOX_HEAD_DIM = 128
RET_HEADS = 8
RET_QK_DIM = 128
RET_V_DIM = 256
ROPE_BASE = 10000.0
N_GROUPS = 4
EXPERTS_PER_GROUP = 8
N_EXPERTS = N_GROUPS * EXPERTS_PER_GROUP
LOG2_EXPERTS_PER_GROUP = EXPERTS_PER_GROUP.bit_length() - 1
LN_EPS = 1e-5
RMS_EPS = 1e-6

LANES = 128
EXPERT_LANE0 = N_GROUPS
VMEM_LIMIT = 56 * 1024 * 1024

ROW_TILE = 512
MATMUL_ROWS = 1024
DMA_UNROLL = 32
LOG2E = 1.4426950408889634
ATTN_TILE = 256
RET_TILE = 256
EXPERT_TILE = 256
GATHER_TILE = 256


def _params(*sem):
    return pltpu.CompilerParams(dimension_semantics=sem, vmem_limit_bytes=VMEM_LIMIT)


def _layer_norm_rows(x, g, b):
    mu = jnp.mean(x, axis=-1, keepdims=True)
    xc = x - mu
    var = jnp.mean(xc * xc, axis=-1, keepdims=True)
    return xc * lax.rsqrt(var + LN_EPS) * g + b


def _ln_kernel(x_ref, g_ref, b_ref, o32_ref, o16_ref):
    y = _layer_norm_rows(x_ref[...], g_ref[...], b_ref[...])
    o32_ref[...] = y
    o16_ref[...] = y.astype(BF16)


def _layer_norm(x, g, b):
    n, d = x.shape
    tm = min(ROW_TILE, n)
    row = pl.BlockSpec((tm, d), lambda i: (i, 0))
    vec = pl.BlockSpec((1, d), lambda i: (0, 0))
    return pl.pallas_call(
        _ln_kernel, grid=(n // tm,), in_specs=[row, vec, vec], out_specs=[row, row],
        out_shape=[jax.ShapeDtypeStruct((n, d), F32), jax.ShapeDtypeStruct((n, d), BF16)],
        compiler_params=_params("parallel"), name="ln_in",
    )(x, g.reshape(1, d), b.reshape(1, d))


def _proj_kernel(x_ref, w_ref, s_ref, *rest, act):
    acc = jnp.dot(x_ref[...], w_ref[...], preferred_element_type=F32) * s_ref[...]
    if act == "rotary":
        cos_ref, sin_ref, o_ref = rest
        cos, sin = cos_ref[...], sin_ref[...]
        for c in range(acc.shape[1] // LANES):
            blk = acc[:, c * LANES:(c + 1) * LANES]
            rot = blk * cos + pltpu.roll(blk, LANES // 2, 1) * sin
            o_ref[:, c * LANES:(c + 1) * LANES] = rot.astype(o_ref.dtype)
        return
    (o_ref,) = rest
    if act == "silu":
        acc = acc * jax.nn.sigmoid(acc)
    elif act == "sigmoid":
        acc = jax.nn.sigmoid(acc)
    o_ref[...] = acc.astype(o_ref.dtype)


def _proj(x16, w16, colscale, act, seq, tables=None, tn=1024):
    n, k = x16.shape
    m = w16.shape[1]
    tm = min(MATMUL_ROWS, seq)
    in_specs = [pl.BlockSpec((tm, k), lambda i, j: (i, 0)),
                pl.BlockSpec((k, tn), lambda i, j: (0, j)),
                pl.BlockSpec((1, tn), lambda i, j: (0, j))]
    args = [x16, w16, colscale.reshape(1, m)]
    if act == "rotary":
        per_seq = seq // tm
        tab = pl.BlockSpec((tm, LANES), lambda i, j: (i % per_seq, 0))
        in_specs += [tab, tab]
        args += list(tables)
    return pl.pallas_call(
        functools.partial(_proj_kernel, act=act), grid=(n // tm, m // tn),
        in_specs=in_specs, out_specs=pl.BlockSpec((tm, tn), lambda i, j: (i, j)),
        out_shape=jax.ShapeDtypeStruct((n, m), BF16),
        compiler_params=_params("parallel", "parallel"), name="proj_" + act,
    )(*args)


def _forget_kernel(h_ref, w_ref, b_ref, o_ref, *, heads, block):
    z = jnp.dot(h_ref[...], w_ref[...], preferred_element_type=F32) + b_ref[...]
    lf = (jnp.minimum(z, 0.0) - jnp.log1p(jnp.exp(-jnp.abs(z)))) * LOG2E
    seq = lf.shape[0]
    r_i = lax.broadcasted_iota(jnp.int32, (block, block), 0)
    c_i = lax.broadcasted_iota(jnp.int32, (block, block), 1)
    upto = (c_i <= r_i).astype(F32)
    lane = lax.broadcasted_iota(jnp.int32, (block, LANES), 1)
    carry = jnp.zeros((1, LANES), F32)
    for blk in range(seq // block):
        rows = slice(blk * block, (blk + 1) * block)
        cum = jnp.dot(upto, lf[rows], preferred_element_type=F32, precision=lax.Precision.HIGHEST) + carry
        carry = cum[block - 1:block, :]
        for h in range(heads):
            col = jnp.broadcast_to(cum[:, h:h + 1], (block, LANES))
            hi = col.astype(BF16).astype(F32)
            mid = (col - hi).astype(BF16).astype(F32)
            lo = col - hi - mid
            split = jnp.where(lane == 0, hi, jnp.where(lane == 1, mid, jnp.where(lane == 2, lo, 0.0)))
            o_ref[rows, h * LANES:(h + 1) * LANES] = split.astype(BF16)


def _forget_cumsum(h16, w_forget, b_forget, batch, seq):
    d, heads = w_forget.shape
    w_pad = jnp.zeros((d, LANES), BF16).at[:, :heads].set(w_forget.astype(BF16))
    b_pad = jnp.zeros((1, LANES), F32).at[0, :heads].set(b_forget.astype(F32))
    return pl.pallas_call(
        functools.partial(_forget_kernel, heads=heads, block=min(256, seq)), grid=(batch,),
        in_specs=[pl.BlockSpec((seq, d), lambda b: (b, 0)),
                  pl.BlockSpec((d, LANES), lambda b: (0, 0)),
                  pl.BlockSpec((1, LANES), lambda b: (0, 0))],
        out_specs=pl.BlockSpec((seq, heads * LANES), lambda b: (b, 0)),
        out_shape=jax.ShapeDtypeStruct((batch * seq, heads * LANES), BF16),
        compiler_params=_params("parallel"), name="forget_cumsum",
    )(h16, w_pad, b_pad)


N_SPLIT = 3


def _fox_kernel(q_ref, k_ref, v_ref, a_ref, o_ref, acc_sc, kaug_sc, *, tile, hp):
    qi = pl.program_id(2)
    dh = FOX_HEAD_DIM

    @pl.when(qi == 0)
    def _():
        for hh in range(hp):
            kaug_sc[:, 2 * hh * dh:(2 * hh + 1) * dh] = k_ref[:, hh * dh:(hh + 1) * dh]
            kaug_sc[:, (2 * hh + 1) * dh:(2 * hh + 2) * dh] = a_ref[:, hh * dh:(hh + 1) * dh]

    lane = lax.broadcasted_iota(jnp.int32, (tile, dh), 1)
    q_extra = jnp.where(lane < N_SPLIT, -1.0, 0.0).astype(BF16)
    q_aug = [jnp.concatenate([q_ref[:, hh * dh:(hh + 1) * dh], q_extra], axis=1) for hh in range(hp)]
    acc_sc[...] = jnp.zeros(acc_sc.shape, F32)

    def block(j, stats, diagonal):
        rows = pl.ds(pl.multiple_of(j * tile, tile), tile)
        s_t = []
        for hh in range(hp):
            s = lax.dot_general(kaug_sc[rows, 2 * hh * dh:(2 * hh + 2) * dh], q_aug[hh],
                                (((1,), (1,)), ((), ())), preferred_element_type=F32)
            if diagonal:
                key = lax.broadcasted_iota(jnp.int32, s.shape, 0)
                qry = lax.broadcasted_iota(jnp.int32, s.shape, 1)
                s = jnp.where(key <= qry, s, -jnp.inf)
            s_t.append(s)
        out, scale, pv = [], [], []
        for hh in range(hp):
            m_prev, l_prev = stats[hh]
            m_new = jnp.maximum(m_prev, jnp.max(s_t[hh], axis=0, keepdims=True))
            a = jnp.exp2(m_prev - m_new)
            p = jnp.exp2(s_t[hh] - m_new)
            out.append((m_new, a * l_prev + jnp.sum(p, axis=0, keepdims=True)))
            scale.append(a)
            pv.append(lax.dot_general(v_ref[rows, hh * dh:(hh + 1) * dh], p.astype(BF16),
                                      (((0,), (0,)), ((), ())), preferred_element_type=F32))
        for hh in range(hp):
            acc_sc[hh] = scale[hh] * acc_sc[hh] + pv[hh]
        return tuple(out)

    init = tuple((jnp.full((1, tile), -jnp.inf, F32), jnp.zeros((1, tile), F32)) for _ in range(hp))
    stats = lax.fori_loop(0, qi, lambda j, st: block(j, st, False), init)
    stats = block(qi, stats, True)
    for hh in range(hp):
        o_ref[:, hh * dh:(hh + 1) * dh] = (acc_sc[hh] / stats[hh][1]).T.astype(o_ref.dtype)


def _fox_attention(qkv16, aug16, batch, seq, hp=8):
    heads, dh = FOX_HEADS, FOX_HEAD_DIM
    tile = min(ATTN_TILE, seq)
    nq = seq // tile
    groups = heads // hp
    w = hp * dh
    return pl.pallas_call(
        functools.partial(_fox_kernel, tile=tile, hp=hp), grid=(batch, groups, nq),
        in_specs=[pl.BlockSpec((tile, w), lambda b, g, i: (b * nq + i, g)),
                  pl.BlockSpec((seq, w), lambda b, g, i: (b, groups + g)),
                  pl.BlockSpec((seq, w), lambda b, g, i: (b, 2 * groups + g)),
                  pl.BlockSpec((seq, w), lambda b, g, i: (b, g))],
        out_specs=pl.BlockSpec((tile, w), lambda b, g, i: (b * nq + i, g)),
        out_shape=jax.ShapeDtypeStruct((batch * seq, heads * dh), BF16),
        scratch_shapes=[pltpu.VMEM((hp, dh, tile), F32), pltpu.VMEM((seq, 2 * w), BF16)],
        compiler_params=_params("parallel", "parallel", "arbitrary"), name="fox_attention",
    )(qkv16, qkv16, qkv16, aug16)


def _ret_kernel(q_ref, k_ref, v_ref, g_ref, dm_ref, qd_ref, kd_ref, cd_ref, o_ref, st_sc, *, hp):
    @pl.when(pl.program_id(2) == 0)
    def _():
        st_sc[...] = jnp.zeros(st_sc.shape, F32)

    dk, dv = RET_QK_DIM, RET_V_DIM
    for hh in range(hp):
        q = q_ref[:, hh * dk:(hh + 1) * dk]
        k = k_ref[:, hh * dk:(hh + 1) * dk]
        v = v_ref[:, hh * dv:(hh + 1) * dv]
        scores = lax.dot_general(q, k, (((1,), (1,)), ((), ())), preferred_element_type=F32) * dm_ref[hh]
        intra = jnp.dot(scores.astype(BF16), v, preferred_element_type=F32)
        state = st_sc[hh]
        cross = jnp.dot(q, state.astype(BF16), preferred_element_type=F32) * qd_ref[hh]
        k_dec = (k.astype(F32) * kd_ref[hh]).astype(BF16)
        st_sc[hh] = state * cd_ref[hh] + lax.dot_general(
            k_dec, v, (((0,), (0,)), ((), ())), preferred_element_type=F32)
        o = intra + cross
        o = o * lax.rsqrt(jnp.mean(o * o, axis=-1, keepdims=True) + RMS_EPS)
        o_ref[:, hh * dv:(hh + 1) * dv] = (g_ref[:, hh * dv:(hh + 1) * dv].astype(F32) * o).astype(o_ref.dtype)


def _retention_tables(tile):
    heads = RET_HEADS
    log_gamma = jnp.log1p(-(2.0 ** (-5.0 - jnp.arange(heads, dtype=F32))))
    pos = jnp.arange(tile, dtype=F32)
    chunk = jnp.arange(tile, dtype=jnp.int32) // CHUNK
    dist = pos[:, None] - pos[None, :]
    same = chunk[:, None] == chunk[None, :]
    earlier = chunk[None, :] < chunk[:, None]
    expo = jnp.where(same, jnp.abs(dist), dist)
    dmask = jnp.where((same | earlier)[None], jnp.exp(log_gamma[:, None, None] * expo[None]), 0.0)
    q_dec = jnp.exp(log_gamma[:, None] * (pos + 1.0))[:, :, None]
    k_dec = jnp.exp(log_gamma[:, None] * (tile - 1 - pos))[:, :, None]
    c_dec = jnp.broadcast_to(jnp.exp(log_gamma * tile)[:, None, None], (heads, 1, RET_V_DIM))
    return dmask, q_dec, k_dec, c_dec


def _retention(rqk16, rv16, rg16, batch, seq, hp=4):
    heads, dk, dv = RET_HEADS, RET_QK_DIM, RET_V_DIM
    tile = min(RET_TILE, seq)
    nt = seq // tile
    groups = heads // hp
    dmask, q_dec, k_dec, c_dec = _retention_tables(tile)
    val = pl.BlockSpec((tile, hp * dv), lambda b, g, t: (b * nt + t, g))
    return pl.pallas_call(
        functools.partial(_ret_kernel, hp=hp), grid=(batch, groups, nt),
        in_specs=[pl.BlockSpec((tile, hp * dk), lambda b, g, t: (b * nt + t, g)),
                  pl.BlockSpec((tile, hp * dk), lambda b, g, t: (b * nt + t, groups + g)),
                  val, val,
                  pl.BlockSpec((hp, tile, tile), lambda b, g, t: (g, 0, 0)),
                  pl.BlockSpec((hp, tile, 1), lambda b, g, t: (g, 0, 0)),
                  pl.BlockSpec((hp, tile, 1), lambda b, g, t: (g, 0, 0)),
                  pl.BlockSpec((hp, 1, dv), lambda b, g, t: (g, 0, 0))],
        out_specs=val,
        out_shape=jax.ShapeDtypeStruct((batch * seq, heads * dv), BF16),
        scratch_shapes=[pltpu.VMEM((hp, dk, dv), F32)],
        compiler_params=_params("parallel", "parallel", "arbitrary"), name="retention",
    )(rqk16, rqk16, rv16, rg16, dmask, q_dec, k_dec, c_dec)


def _merge_kernel(yf_ref, yr_ref, wf_ref, wr_ref, gf_ref, gr_ref, o_ref):
    fox = jnp.dot(yf_ref[...], wf_ref[...], preferred_element_type=F32)
    ret = jnp.dot(yr_ref[...], wr_ref[...], preferred_element_type=F32)
    o_ref[...] = (gf_ref[...].astype(F32) * fox + gr_ref[...].astype(F32) * ret).astype(o_ref.dtype)


def _merge(y_fox, y_ret, w_fox16, w_ret16, gates16, tn=512):
    n, kf = y_fox.shape
    kr = y_ret.shape[1]
    d = w_fox16.shape[1]
    tm = min(MATMUL_ROWS, n)
    nj = d // tn
    return pl.pallas_call(
        _merge_kernel, grid=(n // tm, nj),
        in_specs=[pl.BlockSpec((tm, kf), lambda i, j: (i, 0)),
                  pl.BlockSpec((tm, kr), lambda i, j: (i, 0)),
                  pl.BlockSpec((kf, tn), lambda i, j: (0, j)),
                  pl.BlockSpec((kr, tn), lambda i, j: (0, j)),
                  pl.BlockSpec((tm, tn), lambda i, j: (i, j)),
                  pl.BlockSpec((tm, tn), lambda i, j: (i, nj + j))],
        out_specs=pl.BlockSpec((tm, tn), lambda i, j: (i, j)),
        out_shape=jax.ShapeDtypeStruct((n, d), BF16),
        compiler_params=_params("parallel", "parallel"), name="merge",
    )(y_fox, y_ret, w_fox16, w_ret16, gates16, gates16)


def _lane_pick(mask, values):
    return jnp.sum(jnp.where(mask, values, 0.0), axis=1, keepdims=True)


def _outproj_router_kernel(m_ref, w_ref, h_ref, g_ref, b_ref, wr_ref, br_ref,
                           h1_ref, route_ref, wcol_ref, cnt_ref, carry_sc, *, alpha):
    i = pl.program_id(0)

    @pl.when(i == 0)
    def _():
        carry_sc[...] = jnp.zeros(carry_sc.shape, F32)

    mix = jnp.dot(m_ref[...], w_ref[...], preferred_element_type=F32)
    hn = _layer_norm_rows(alpha * h_ref[...] + mix, g_ref[...], b_ref[...])
    h1_ref[...] = hn

    hn_hi = hn.astype(BF16)
    hn_lo = (hn - hn_hi.astype(F32)).astype(BF16)
    part = jnp.dot(hn_hi, wr_ref[...], preferred_element_type=F32)
    logits = (part[:, :LANES] + part[:, LANES:] + br_ref[...]
              + jnp.dot(hn_lo, wr_ref[:, :LANES], preferred_element_type=F32))
    tm = logits.shape[0]
    lane = lax.broadcasted_iota(jnp.int32, logits.shape, 1)
    neg = -jnp.inf
    gl = jnp.where(lane < N_GROUPS, logits, neg)
    gmax = jnp.max(gl, axis=1, keepdims=True)
    grp_w = 1.0 / jnp.sum(jnp.exp(gl - gmax), axis=1, keepdims=True)
    gidx = jnp.min(jnp.where(gl == gmax, lane, LANES), axis=1, keepdims=True)
    e_lane = lane - EXPERT_LANE0
    in_grp = (e_lane >= 0) & (e_lane < N_EXPERTS) & ((e_lane >> LOG2_EXPERTS_PER_GROUP) == gidx)
    el = jnp.where(in_grp, logits, neg)
    max1 = jnp.max(el, axis=1, keepdims=True)
    i1 = jnp.min(jnp.where(el == max1, lane, LANES), axis=1, keepdims=True)
    el2 = jnp.where(lane == i1, neg, el)
    max2 = jnp.max(el2, axis=1, keepdims=True)
    i2 = jnp.min(jnp.where(el2 == max2, lane, LANES), axis=1, keepdims=True)
    ratio = jnp.exp(max2 - max1)
    w1 = grp_w / (1.0 + ratio)
    w2 = grp_w * ratio / (1.0 + ratio)
    sel1 = lane == i1
    sel2 = lane == i2
    sel = sel1.astype(F32) + sel2.astype(F32)
    r_i = lax.broadcasted_iota(jnp.int32, (tm, tm), 0)
    c_i = lax.broadcasted_iota(jnp.int32, (tm, tm), 1)
    before = (c_i < r_i).astype(BF16)
    rank = carry_sc[...] + jnp.dot(before, sel.astype(BF16), preferred_element_type=F32)
    new_carry = carry_sc[...] + jnp.sum(sel, axis=0, keepdims=True)
    carry_sc[...] = new_carry
    cnt_ref[...] = jnp.broadcast_to(new_carry, cnt_ref.shape)
    r1 = _lane_pick(sel1, rank)
    r2 = _lane_pick(sel2, rank)
    e1 = (i1 - EXPERT_LANE0).astype(F32)
    e2 = (i2 - EXPERT_LANE0).astype(F32)
    rec = jnp.where(lane == 0, e1, jnp.where(lane == 1, e2, jnp.where(lane == 2, r1, jnp.where(lane == 3, r2, 0.0))))
    route_ref[...] = rec.T[0:8, :]
    wcol_ref[...] = jnp.where(lane == 0, w1, jnp.where(lane == 1, w2, 0.0))


def _outproj_router(merged16, w_out16, h32, g, b, w_router, b_router, alpha):
    n, d = merged16.shape
    tm = min(ROW_TILE, n)
    row16 = pl.BlockSpec((tm, d), lambda i: (i, 0))
    vec = pl.BlockSpec((1, d), lambda i: (0, 0))
    return pl.pallas_call(
        functools.partial(_outproj_router_kernel, alpha=alpha), grid=(n // tm,),
        in_specs=[row16, pl.BlockSpec((d, d), lambda i: (0, 0)), row16, vec, vec,
                  pl.BlockSpec((d, 2 * LANES), lambda i: (0, 0)), pl.BlockSpec((1, LANES), lambda i: (0, 0))],
        out_specs=[row16, pl.BlockSpec((8, tm), lambda i: (0, i)),
                   pl.BlockSpec((tm, LANES), lambda i: (i, 0)), pl.BlockSpec((8, LANES), lambda i: (0, 0))],
        out_shape=[jax.ShapeDtypeStruct((n, d), F32), jax.ShapeDtypeStruct((8, n), F32),
                   jax.ShapeDtypeStruct((n, LANES), F32), jax.ShapeDtypeStruct((8, LANES), F32)],
        scratch_shapes=[pltpu.VMEM((1, LANES), F32)],
        compiler_params=_params("arbitrary"), name="outproj_router",
    )(merged16, w_out16, h32, g.reshape(1, d), b.reshape(1, d), w_router, b_router)


def _plan_kernel(route_ref, cnt_ref, dest_ref, tiles_ref, *, tile, log2_tile):
    cnt = cnt_ref[...].astype(jnp.int32)
    padded = ((cnt + (tile - 1)) >> log2_tile) << log2_tile
    lane = lax.broadcasted_iota(jnp.int32, padded.shape, 1)
    end = padded
    k = 1
    while k < LANES:
        end = end + jnp.where(lane >= k, pltpu.roll(end, k, 1), 0)
        k *= 2
    start = end - padded
    rows = EXPERT_LANE0 + N_EXPERTS
    rows = -(-rows // 8) * 8
    def lane_to_sublane(v):
        return jnp.broadcast_to(v[0:1, :].astype(F32), (LANES, LANES)).T[0:rows, 0:1]

    start_col = lane_to_sublane(start)
    end_col = lane_to_sublane(end)
    route = route_ref[...]
    n = route.shape[1]
    sub = lax.broadcasted_iota(jnp.int32, (rows, n), 0)
    e1 = route[0:1, :].astype(jnp.int32) + EXPERT_LANE0
    e2 = route[1:2, :].astype(jnp.int32) + EXPERT_LANE0
    d1 = jnp.sum(jnp.where(sub == e1, start_col, 0.0), axis=0, keepdims=True) + route[2:3, :]
    d2 = jnp.sum(jnp.where(sub == e2, start_col, 0.0), axis=0, keepdims=True) + route[3:4, :]
    row8 = lax.broadcasted_iota(jnp.int32, dest_ref.shape, 0)
    dest_ref[...] = jnp.where(row8 == 0, d1, jnp.where(row8 == 1, d2, 0.0)).astype(jnp.int32)
    nt = tiles_ref.shape[1]
    t_start = (lax.broadcasted_iota(jnp.int32, (rows, nt), 1) * tile).astype(F32)
    sub_t = lax.broadcasted_iota(jnp.int32, (rows, nt), 0)
    is_expert = (sub_t >= EXPERT_LANE0) & (sub_t < EXPERT_LANE0 + N_EXPERTS)
    done = jnp.sum(jnp.where(is_expert & (end_col <= t_start), 1.0, 0.0), axis=0, keepdims=True)
    expert = jnp.minimum(done, float(N_EXPERTS - 1))
    used = jnp.max(end_col, axis=0, keepdims=True) * (1.0 / tile)
    row8t = lax.broadcasted_iota(jnp.int32, tiles_ref.shape, 0)
    tiles_ref[...] = jnp.where(row8t == 0, expert, jnp.broadcast_to(used, tiles_ref.shape)).astype(jnp.int32)


def _plan(route, counts, tile, n_tiles):
    n = route.shape[1]
    nt_pad = -(-n_tiles // LANES) * LANES
    return pl.pallas_call(
        functools.partial(_plan_kernel, tile=tile, log2_tile=tile.bit_length() - 1),
        out_shape=[jax.ShapeDtypeStruct((8, n), jnp.int32), jax.ShapeDtypeStruct((8, nt_pad), jnp.int32)],
        compiler_params=pltpu.CompilerParams(vmem_limit_bytes=VMEM_LIMIT), name="plan",
    )(route, counts)


def _start_then_wait_rows(copies, n_rows):
    def start(o, c):
        for u in range(DMA_UNROLL):
            for cp in copies(o * DMA_UNROLL + u):
                cp.start()
        return c

    def wait(o, c):
        for u in range(DMA_UNROLL):
            for cp in copies(o * DMA_UNROLL + u):
                cp.wait()
        return c

    lax.fori_loop(0, n_rows // DMA_UNROLL, start, 0)
    lax.fori_loop(0, n_rows // DMA_UNROLL, wait, 0)


def _dispatch_kernel(d1_ref, d2_ref, h_ref, xs_in, xs_hbm, sem, *, tile):
    del xs_in
    base = pl.program_id(0) * tile

    def copies(r):
        n = base + r
        src = h_ref.at[pl.ds(r, 1), :]
        return (pltpu.make_async_copy(src, xs_hbm.at[pl.ds(d1_ref[n], 1), :], sem),
                pltpu.make_async_copy(src, xs_hbm.at[pl.ds(d2_ref[n], 1), :], sem))

    _start_then_wait_rows(copies, tile)


def _dispatch(d1, d2, h32, n_slots):
    n, d = h32.shape
    tile = min(GATHER_TILE, n)
    xs0 = jnp.zeros((n_slots, d), F32)
    return pl.pallas_call(
        functools.partial(_dispatch_kernel, tile=tile),
        grid_spec=pltpu.PrefetchScalarGridSpec(
            num_scalar_prefetch=2, grid=(n // tile,),
            in_specs=[pl.BlockSpec((tile, d), lambda i, a, c: (i, 0)), pl.BlockSpec(memory_space=pl.ANY)],
            out_specs=pl.BlockSpec(memory_space=pl.ANY),
            scratch_shapes=[pltpu.SemaphoreType.DMA(())]),
        out_shape=jax.ShapeDtypeStruct((n_slots, d), F32),
        input_output_aliases={3: 0},
        compiler_params=pltpu.CompilerParams(dimension_semantics=("arbitrary",), has_side_effects=True,
                                             vmem_limit_bytes=VMEM_LIMIT),
        name="dispatch",
    )(d1, d2, h32, xs0)


def _expert_kernel(te_ref, nu_ref, x_ref, wg_ref, wu_ref, wd_ref, y_ref):
    t = pl.program_id(0)

    @pl.when(t < nu_ref[0])
    def _():
        x = x_ref[...].astype(BF16)
        gate = jnp.dot(x, wg_ref[0], preferred_element_type=F32)
        up = jnp.dot(x, wu_ref[0], preferred_element_type=F32)
        hid = (gate * jax.nn.sigmoid(gate) * up).astype(BF16)
        y_ref[...] = jnp.dot(hid, wd_ref[0], preferred_element_type=F32)

    @pl.when(t >= nu_ref[0])
    def _():
        y_ref[...] = jnp.zeros(y_ref.shape, F32)


def _experts(tile_expert, n_used, xs, wg16, wu16, wd16, tile):
    p, d = xs.shape
    f = wg16.shape[2]
    return pl.pallas_call(
        _expert_kernel,
        grid_spec=pltpu.PrefetchScalarGridSpec(
            num_scalar_prefetch=2, grid=(p // tile,),
            in_specs=[pl.BlockSpec((tile, d), lambda t, te, nu: (t, 0)),
                      pl.BlockSpec((1, d, f), lambda t, te, nu: (te[t], 0, 0)),
                      pl.BlockSpec((1, d, f), lambda t, te, nu: (te[t], 0, 0)),
                      pl.BlockSpec((1, f, d), lambda t, te, nu: (te[t], 0, 0))],
            out_specs=pl.BlockSpec((tile, d), lambda t, te, nu: (t, 0))),
        out_shape=jax.ShapeDtypeStruct((p, d), F32),
        compiler_params=_params("arbitrary"), name="experts",
    )(tile_expert, n_used, xs, wg16, wu16, wd16)


def _combine_kernel(d1_ref, d2_ref, h_ref, wc_ref, g_ref, b_ref, y_hbm, o32_ref, o16_ref,
                    buf1, buf2, sem, *, tile, alpha):
    base = pl.program_id(0) * tile

    def copies(r):
        n = base + r
        return (pltpu.make_async_copy(y_hbm.at[pl.ds(d1_ref[n], 1), :], buf1.at[pl.ds(r, 1), :], sem),
                pltpu.make_async_copy(y_hbm.at[pl.ds(d2_ref[n], 1), :], buf2.at[pl.ds(r, 1), :], sem))

    _start_then_wait_rows(copies, tile)
    wc = wc_ref[...]
    ffn = wc[:, 0:1] * buf1[...] + wc[:, 1:2] * buf2[...]
    y = _layer_norm_rows(alpha * h_ref[...] + ffn, g_ref[...], b_ref[...])
    o32_ref[...] = y
    o16_ref[...] = y.astype(BF16)


def _combine(d1, d2, h32, wcol, g, b, y, alpha):
    n, d = h32.shape
    tile = min(GATHER_TILE, n)
    row = pl.BlockSpec((tile, d), lambda i, a, c: (i, 0))
    vec = pl.BlockSpec((1, d), lambda i, a, c: (0, 0))
    return pl.pallas_call(
        functools.partial(_combine_kernel, tile=tile, alpha=alpha),
        grid_spec=pltpu.PrefetchScalarGridSpec(
            num_scalar_prefetch=2, grid=(n // tile,),
            in_specs=[row, pl.BlockSpec((tile, LANES), lambda i, a, c: (i, 0)), vec, vec,
                      pl.BlockSpec(memory_space=pl.ANY)],
            out_specs=[row, row],
            scratch_shapes=[pltpu.VMEM((tile, d), F32), pltpu.VMEM((tile, d), F32), pltpu.SemaphoreType.DMA(())]),
        out_shape=[jax.ShapeDtypeStruct((n, d), F32), jax.ShapeDtypeStruct((n, d), BF16)],
        compiler_params=_params("arbitrary"), name="combine_ln",
    )(d1, d2, h32, wcol, g.reshape(1, d), b.reshape(1, d), y)


def _rotary_tables(seq):
    half = RET_QK_DIM // 2
    inv_freq = ROPE_BASE ** (-jnp.arange(half, dtype=F32) / half)
    ang = jnp.arange(seq, dtype=jnp.int32).astype(F32)[:, None] * inv_freq[None, :]
    cos, sin = jnp.cos(ang), jnp.sin(ang)
    return jnp.concatenate([cos, cos], axis=1), jnp.concatenate([-sin, sin], axis=1)


def _col_scale(sizes_scales):
    return jnp.concatenate([jnp.full((size,), scale, F32) for size, scale in sizes_scales])


def kernel(x, ln_in_g, ln_in_b, w_in, b_forget, w_branch_fox, w_branch_ret, w_out, ln1_g, ln1_b,
           w_router_group, b_router_group, w_router_expert, b_router_expert, w_gate, w_up, w_down, ln2_g, ln2_b):
    batch, seq, d = x.shape
    n = batch * seq
    depth = w_in.shape[0]
    alpha = (2.0 * depth) ** 0.25
    fox_w = FOX_HEADS * FOX_HEAD_DIM
    ret_qk = RET_HEADS * RET_QK_DIM
    ret_v = RET_HEADS * RET_V_DIM
    d_expert = w_gate.shape[-1]
    o_ff = 3 * fox_w
    o_rq = o_ff + FOX_HEADS
    o_rv = o_rq + 2 * ret_qk
    o_rg = o_rv + ret_v
    o_gate = o_rg + ret_v
    tables = _rotary_tables(seq)
    fox_scale = _col_scale([(fox_w, FOX_HEAD_DIM ** -0.5 * LOG2E), (2 * fox_w, 1.0)])
    rqk_scale = _col_scale([(ret_qk, 1.0), (ret_qk, RET_QK_DIM ** -0.5)])
    ones_v = jnp.ones((ret_v,), F32)
    ones_g = jnp.ones((2 * d,), F32)
    n_tiles = (N_EXPERTS * (EXPERT_TILE - 1) + 2 * n) // EXPERT_TILE + 1
    n_slots = n_tiles * EXPERT_TILE

    h32, h16 = _layer_norm(x.reshape(n, d), ln_in_g, ln_in_b)
    for l in range(depth):
        w = w_in[l]
        qkv = _proj(h16, w[:, :o_ff].astype(BF16), fox_scale, "none", seq)
        aug = _forget_cumsum(h16, w[:, o_ff:o_rq], b_forget[l], batch, seq)
        rqk = _proj(h16, w[:, o_rq:o_rv].astype(BF16), rqk_scale, "rotary", seq, tables)
        rv = _proj(h16, w[:, o_rv:o_rg].astype(BF16), ones_v, "none", seq)
        rg = _proj(h16, w[:, o_rg:o_gate].astype(BF16), ones_v, "silu", seq)
        gates = _proj(h16, w[:, o_gate:].astype(BF16), ones_g, "sigmoid", seq)
        y_fox = _fox_attention(qkv, aug, batch, seq)
        y_ret = _retention(rqk, rv, rg, batch, seq)
        merged = _merge(y_fox, y_ret, w_branch_fox[l].astype(BF16), w_branch_ret[l].astype(BF16), gates)

        w_router = jnp.zeros((d, LANES), F32)
        w_router = w_router.at[:, :N_GROUPS].set(w_router_group[l])
        w_router = w_router.at[:, EXPERT_LANE0:EXPERT_LANE0 + N_EXPERTS].set(w_router_expert[l])
        b_router = jnp.zeros((1, LANES), F32)
        b_router = b_router.at[0, :N_GROUPS].set(b_router_group[l])
        b_router = b_router.at[0, EXPERT_LANE0:EXPERT_LANE0 + N_EXPERTS].set(b_router_expert[l])
        w_router_hi = w_router.astype(BF16)
        w_router_lo = (w_router - w_router_hi.astype(F32)).astype(BF16)
        h1, route, wcol, counts = _outproj_router(merged, w_out[l].astype(BF16), h32, ln1_g[l], ln1_b[l],
                                                  jnp.concatenate([w_router_hi, w_router_lo], axis=1),
                                                  b_router, alpha)
        dest, tiles = _plan(route, counts, EXPERT_TILE, n_tiles)
        d1, d2 = dest[0], dest[1]
        xs = _dispatch(d1, d2, h1, n_slots)
        y = _experts(tiles[0, :n_tiles], tiles[1, :1], xs,
                     w_gate[l].reshape(N_EXPERTS, d, d_expert).astype(BF16),
                     w_up[l].reshape(N_EXPERTS, d, d_expert).astype(BF16),
                     w_down[l].reshape(N_EXPERTS, d_expert, d).astype(BF16), EXPERT_TILE)
        h32, h16 = _combine(d1, d2, h1, wcol, ln2_g[l], ln2_b[l], y, alpha)
    return h32.reshape(batch, seq, d)
```

```python
import functools

import jax
import jax.numpy as jnp
from jax import lax
from jax.experimental import pallas as pl
from jax.experimental.pallas import tpu as pltpu

F32 = jnp.float32
BF16 = jnp.bfloat16

CHUNK = 64
FOX_HEADS = 8
FOX_HEAD_DIM = 128
RET_HEADS = 8
RET_QK_DIM = 128
RET_V_DIM = 256
ROPE_BASE = 10000.0
N_GROUPS = 4
EXPERTS_PER_GROUP = 8
N_EXPERTS = N_GROUPS * EXPERTS_PER_GROUP
LOG2_EXPERTS_PER_GROUP = EXPERTS_PER_GROUP.bit_length() - 1
LN_EPS = 1e-5
RMS_EPS = 1e-6

LANES = 128
EXPERT_LANE0 = N_GROUPS
VMEM_LIMIT = 56 * 1024 * 1024

ROW_TILE = 512
MATMUL_ROWS = 1024
DMA_UNROLL = 32
LOG2E = 1.4426950408889634
ATTN_TILE = 256
RET_TILE = 256
EXPERT_TILE = 256
GATHER_TILE = 256


def _params(*sem):
    return pltpu.CompilerParams(dimension_semantics=sem, vmem_limit_bytes=VMEM_LIMIT)


def _layer_norm_rows(x, g, b):
    mu = jnp.mean(x, axis=-1, keepdims=True)
    xc = x - mu
    var = jnp.mean(xc * xc, axis=-1, keepdims=True)
    return xc * lax.rsqrt(var + LN_EPS) * g + b


def _ln_kernel(x_ref, g_ref, b_ref, o32_ref, o16_ref):
    y = _layer_norm_rows(x_ref[...], g_ref[...], b_ref[...])
    o32_ref[...] = y
    o16_ref[...] = y.astype(BF16)


def _layer_norm(x, g, b):
    n, d = x.shape
    tm = min(ROW_TILE, n)
    row = pl.BlockSpec((tm, d), lambda i: (i, 0))
    vec = pl.BlockSpec((1, d), lambda i: (0, 0))
    return pl.pallas_call(
        _ln_kernel, grid=(n // tm,), in_specs=[row, vec, vec], out_specs=[row, row],
        out_shape=[jax.ShapeDtypeStruct((n, d), F32), jax.ShapeDtypeStruct((n, d), BF16)],
        compiler_params=_params("parallel"), name="ln_in",
    )(x, g.reshape(1, d), b.reshape(1, d))


def _proj_kernel(x_ref, w_ref, s_ref, *rest, act):
    acc = jnp.dot(x_ref[...], w_ref[...], preferred_element_type=F32) * s_ref[...]
    if act == "rotary":
        cos_ref, sin_ref, o_ref = rest
        cos, sin = cos_ref[...], sin_ref[...]
        for c in range(acc.shape[1] // LANES):
            blk = acc[:, c * LANES:(c + 1) * LANES]
            rot = blk * cos + pltpu.roll(blk, LANES // 2, 1) * sin
            o_ref[:, c * LANES:(c + 1) * LANES] = rot.astype(o_ref.dtype)
        return
    (o_ref,) = rest
    if act == "silu":
        acc = acc * jax.nn.sigmoid(acc)
    elif act == "sigmoid":
        acc = jax.nn.sigmoid(acc)
    o_ref[...] = acc.astype(o_ref.dtype)


def _proj(x16, w16, colscale, act, seq, tables=None, tn=1024):
    n, k = x16.shape
    m = w16.shape[1]
    tm = min(MATMUL_ROWS, seq)
    in_specs = [pl.BlockSpec((tm, k), lambda i, j: (i, 0)),
                pl.BlockSpec((k, tn), lambda i, j: (0, j)),
                pl.BlockSpec((1, tn), lambda i, j: (0, j))]
    args = [x16, w16, colscale.reshape(1, m)]
    if act == "rotary":
        per_seq = seq // tm
        tab = pl.BlockSpec((tm, LANES), lambda i, j: (i % per_seq, 0))
        in_specs += [tab, tab]
        args += list(tables)
    return pl.pallas_call(
        functools.partial(_proj_kernel, act=act), grid=(n // tm, m // tn),
        in_specs=in_specs, out_specs=pl.BlockSpec((tm, tn), lambda i, j: (i, j)),
        out_shape=jax.ShapeDtypeStruct((n, m), BF16),
        compiler_params=_params("parallel", "parallel"), name="proj_" + act,
    )(*args)


def _forget_kernel(h_ref, w_ref, b_ref, o_ref, *, heads, block):
    z = jnp.dot(h_ref[...], w_ref[...], preferred_element_type=F32) + b_ref[...]
    lf = (jnp.minimum(z, 0.0) - jnp.log1p(jnp.exp(-jnp.abs(z)))) * LOG2E
    seq = lf.shape[0]
    r_i = lax.broadcasted_iota(jnp.int32, (block, block), 0)
    c_i = lax.broadcasted_iota(jnp.int32, (block, block), 1)
    upto = (c_i <= r_i).astype(F32)
    lane = lax.broadcasted_iota(jnp.int32, (block, LANES), 1)
    carry = jnp.zeros((1, LANES), F32)
    for blk in range(seq // block):
        rows = slice(blk * block, (blk + 1) * block)
        cum = jnp.dot(upto, lf[rows], preferred_element_type=F32, precision=lax.Precision.HIGHEST) + carry
        carry = cum[block - 1:block, :]
        for h in range(heads):
            col = jnp.broadcast_to(cum[:, h:h + 1], (block, LANES))
            hi = col.astype(BF16).astype(F32)
            mid = (col - hi).astype(BF16).astype(F32)
            lo = col - hi - mid
            split = jnp.where(lane == 0, hi, jnp.where(lane == 1, mid, jnp.where(lane == 2, lo, 0.0)))
            o_ref[rows, h * LANES:(h + 1) * LANES] = split.astype(BF16)


def _forget_cumsum(h16, w_forget, b_forget, batch, seq):
    d, heads = w_forget.shape
    w_pad = jnp.zeros((d, LANES), BF16).at[:, :heads].set(w_forget.astype(BF16))
    b_pad = jnp.zeros((1, LANES), F32).at[0, :heads].set(b_forget.astype(F32))
    return pl.pallas_call(
        functools.partial(_forget_kernel, heads=heads, block=min(256, seq)), grid=(batch,),
        in_specs=[pl.BlockSpec((seq, d), lambda b: (b, 0)),
                  pl.BlockSpec((d, LANES), lambda b: (0, 0)),
                  pl.BlockSpec((1, LANES), lambda b: (0, 0))],
        out_specs=pl.BlockSpec((seq, heads * LANES), lambda b: (b, 0)),
        out_shape=jax.ShapeDtypeStruct((batch * seq, heads * LANES), BF16),
        compiler_params=_params("parallel"), name="forget_cumsum",
    )(h16, w_pad, b_pad)


N_SPLIT = 3


def _fox_kernel(q_ref, k_ref, v_ref, a_ref, o_ref, acc_sc, kaug_sc, *, tile, hp):
    qi = pl.program_id(2)
    dh = FOX_HEAD_DIM

    @pl.when(qi == 0)
    def _():
        for hh in range(hp):
            kaug_sc[:, 2 * hh * dh:(2 * hh + 1) * dh] = k_ref[:, hh * dh:(hh + 1) * dh]
            kaug_sc[:, (2 * hh + 1) * dh:(2 * hh + 2) * dh] = a_ref[:, hh * dh:(hh + 1) * dh]

    lane = lax.broadcasted_iota(jnp.int32, (tile, dh), 1)
    q_extra = jnp.where(lane < N_SPLIT, -1.0, 0.0).astype(BF16)
    q_aug = [jnp.concatenate([q_ref[:, hh * dh:(hh + 1) * dh], q_extra], axis=1) for hh in range(hp)]
    acc_sc[...] = jnp.zeros(acc_sc.shape, F32)

    def block(j, stats, diagonal):
        rows = pl.ds(pl.multiple_of(j * tile, tile), tile)
        s_t = []
        for hh in range(hp):
            s = lax.dot_general(kaug_sc[rows, 2 * hh * dh:(2 * hh + 2) * dh], q_aug[hh],
                                (((1,), (1,)), ((), ())), preferred_element_type=F32)
            if diagonal:
                key = lax.broadcasted_iota(jnp.int32, s.shape, 0)
                qry = lax.broadcasted_iota(jnp.int32, s.shape, 1)
                s = jnp.where(key <= qry, s, -jnp.inf)
            s_t.append(s)
        out, scale, pv = [], [], []
        for hh in range(hp):
            m_prev, l_prev = stats[hh]
            m_new = jnp.maximum(m_prev, jnp.max(s_t[hh], axis=0, keepdims=True))
            a = jnp.exp2(m_prev - m_new)
            p = jnp.exp2(s_t[hh] - m_new)
            out.append((m_new, a * l_prev + jnp.sum(p, axis=0, keepdims=True)))
            scale.append(a)
            pv.append(lax.dot_general(v_ref[rows, hh * dh:(hh + 1) * dh], p.astype(BF16),
                                      (((0,), (0,)), ((), ())), preferred_element_type=F32))
        for hh in range(hp):
            acc_sc[hh] = scale[hh] * acc_sc[hh] + pv[hh]
        return tuple(out)

    init = tuple((jnp.full((1, tile), -jnp.inf, F32), jnp.zeros((1, tile), F32)) for _ in range(hp))
    stats = lax.fori_loop(0, qi, lambda j, st: block(j, st, False), init)
    stats = block(qi, stats, True)
    for hh in range(hp):
        o_ref[:, hh * dh:(hh + 1) * dh] = (acc_sc[hh] / stats[hh][1]).T.astype(o_ref.dtype)


def _fox_attention(qkv16, aug16, batch, seq, hp=8):
    heads, dh = FOX_HEADS, FOX_HEAD_DIM
    tile = min(ATTN_TILE, seq)
    nq = seq // tile
    groups = heads // hp
    w = hp * dh
    return pl.pallas_call(
        functools.partial(_fox_kernel, tile=tile, hp=hp), grid=(batch, groups, nq),
        in_specs=[pl.BlockSpec((tile, w), lambda b, g, i: (b * nq + i, g)),
                  pl.BlockSpec((seq, w), lambda b, g, i: (b, groups + g)),
                  pl.BlockSpec((seq, w), lambda b, g, i: (b, 2 * groups + g)),
                  pl.BlockSpec((seq, w), lambda b, g, i: (b, g))],
        out_specs=pl.BlockSpec((tile, w), lambda b, g, i: (b * nq + i, g)),
        out_shape=jax.ShapeDtypeStruct((batch * seq, heads * dh), BF16),
        scratch_shapes=[pltpu.VMEM((hp, dh, tile), F32), pltpu.VMEM((seq, 2 * w), BF16)],
        compiler_params=_params("parallel", "parallel", "arbitrary"), name="fox_attention",
    )(qkv16, qkv16, qkv16, aug16)


def _ret_kernel(q_ref, k_ref, v_ref, g_ref, dm_ref, qd_ref, kd_ref, cd_ref, o_ref, st_sc, *, hp):
    @pl.when(pl.program_id(2) == 0)
    def _():
        st_sc[...] = jnp.zeros(st_sc.shape, F32)

    dk, dv = RET_QK_DIM, RET_V_DIM
    for hh in range(hp):
        q = q_ref[:, hh * dk:(hh + 1) * dk]
        k = k_ref[:, hh * dk:(hh + 1) * dk]
        v = v_ref[:, hh * dv:(hh + 1) * dv]
        scores = lax.dot_general(q, k, (((1,), (1,)), ((), ())), preferred_element_type=F32) * dm_ref[hh]
        intra = jnp.dot(scores.astype(BF16), v, preferred_element_type=F32)
        state = st_sc[hh]
        cross = jnp.dot(q, state.astype(BF16), preferred_element_type=F32) * qd_ref[hh]
        k_dec = (k.astype(F32) * kd_ref[hh]).astype(BF16)
        st_sc[hh] = state * cd_ref[hh] + lax.dot_general(
            k_dec, v, (((0,), (0,)), ((), ())), preferred_element_type=F32)
        o = intra + cross
        o = o * lax.rsqrt(jnp.mean(o * o, axis=-1, keepdims=True) + RMS_EPS)
        o_ref[:, hh * dv:(hh + 1) * dv] = (g_ref[:, hh * dv:(hh + 1) * dv].astype(F32) * o).astype(o_ref.dtype)


def _retention_tables(tile):
    heads = RET_HEADS
    log_gamma = jnp.log1p(-(2.0 ** (-5.0 - jnp.arange(heads, dtype=F32))))
    pos = jnp.arange(tile, dtype=F32)
    chunk = jnp.arange(tile, dtype=jnp.int32) // CHUNK
    dist = pos[:, None] - pos[None, :]
    same = chunk[:, None] == chunk[None, :]
    earlier = chunk[None, :] < chunk[:, None]
    expo = jnp.where(same, jnp.abs(dist), dist)
    dmask = jnp.where((same | earlier)[None], jnp.exp(log_gamma[:, None, None] * expo[None]), 0.0)
    q_dec = jnp.exp(log_gamma[:, None] * (pos + 1.0))[:, :, None]
    k_dec = jnp.exp(log_gamma[:, None] * (tile - 1 - pos))[:, :, None]
    c_dec = jnp.broadcast_to(jnp.exp(log_gamma * tile)[:, None, None], (heads, 1, RET_V_DIM))
    return dmask, q_dec, k_dec, c_dec


def _retention(rqk16, rv16, rg16, batch, seq, hp=4):
    heads, dk, dv = RET_HEADS, RET_QK_DIM, RET_V_DIM
    tile = min(RET_TILE, seq)
    nt = seq // tile
    groups = heads // hp
    dmask, q_dec, k_dec, c_dec = _retention_tables(tile)
    val = pl.BlockSpec((tile, hp * dv), lambda b, g, t: (b * nt + t, g))
    return pl.pallas_call(
        functools.partial(_ret_kernel, hp=hp), grid=(batch, groups, nt),
        in_specs=[pl.BlockSpec((tile, hp * dk), lambda b, g, t: (b * nt + t, g)),
                  pl.BlockSpec((tile, hp * dk), lambda b, g, t: (b * nt + t, groups + g)),
                  val, val,
                  pl.BlockSpec((hp, tile, tile), lambda b, g, t: (g, 0, 0)),
                  pl.BlockSpec((hp, tile, 1), lambda b, g, t: (g, 0, 0)),
                  pl.BlockSpec((hp, tile, 1), lambda b, g, t: (g, 0, 0)),
                  pl.BlockSpec((hp, 1, dv), lambda b, g, t: (g, 0, 0))],
        out_specs=val,
        out_shape=jax.ShapeDtypeStruct((batch * seq, heads * dv), BF16),
        scratch_shapes=[pltpu.VMEM((hp, dk, dv), F32)],
        compiler_params=_params("parallel", "parallel", "arbitrary"), name="retention",
    )(rqk16, rqk16, rv16, rg16, dmask, q_dec, k_dec, c_dec)


def _merge_kernel(yf_ref, yr_ref, wf_ref, wr_ref, gf_ref, gr_ref, o_ref):
    fox = jnp.dot(yf_ref[...], wf_ref[...], preferred_element_type=F32)
    ret = jnp.dot(yr_ref[...], wr_ref[...], preferred_element_type=F32)
    o_ref[...] = (gf_ref[...].astype(F32) * fox + gr_ref[...].astype(F32) * ret).astype(o_ref.dtype)


def _merge(y_fox, y_ret, w_fox16, w_ret16, gates16, tn=512):
    n, kf = y_fox.shape
    kr = y_ret.shape[1]
    d = w_fox16.shape[1]
    tm = min(MATMUL_ROWS, n)
    nj = d // tn
    return pl.pallas_call(
        _merge_kernel, grid=(n // tm, nj),
        in_specs=[pl.BlockSpec((tm, kf), lambda i, j: (i, 0)),
                  pl.BlockSpec((tm, kr), lambda i, j: (i, 0)),
                  pl.BlockSpec((kf, tn), lambda i, j: (0, j)),
                  pl.BlockSpec((kr, tn), lambda i, j: (0, j)),
                  pl.BlockSpec((tm, tn), lambda i, j: (i, j)),
                  pl.BlockSpec((tm, tn), lambda i, j: (i, nj + j))],
        out_specs=pl.BlockSpec((tm, tn), lambda i, j: (i, j)),
        out_shape=jax.ShapeDtypeStruct((n, d), BF16),
        compiler_params=_params("parallel", "parallel"), name="merge",
    )(y_fox, y_ret, w_fox16, w_ret16, gates16, gates16)


def _lane_pick(mask, values):
    return jnp.sum(jnp.where(mask, values, 0.0), axis=1, keepdims=True)


def _outproj_router_kernel(m_ref, w_ref, h_ref, g_ref, b_ref, wr_ref, br_ref,
                           h1_ref, route_ref, wcol_ref, cnt_ref, carry_sc, *, alpha):
    i = pl.program_id(0)

    @pl.when(i == 0)
    def _():
        carry_sc[...] = jnp.zeros(carry_sc.shape, F32)

    mix = jnp.dot(m_ref[...], w_ref[...], preferred_element_type=F32)
    hn = _layer_norm_rows(alpha * h_ref[...] + mix, g_ref[...], b_ref[...])
    h1_ref[...] = hn

    hn_hi = hn.astype(BF16)
    hn_lo = (hn - hn_hi.astype(F32)).astype(BF16)
    part = jnp.dot(hn_hi, wr_ref[...], preferred_element_type=F32)
    logits = (part[:, :LANES] + part[:, LANES:] + br_ref[...]
              + jnp.dot(hn_lo, wr_ref[:, :LANES], preferred_element_type=F32))
    tm = logits.shape[0]
    lane = lax.broadcasted_iota(jnp.int32, logits.shape, 1)
    neg = -jnp.inf
    gl = jnp.where(lane < N_GROUPS, logits, neg)
    gmax = jnp.max(gl, axis=1, keepdims=True)
    grp_w = 1.0 / jnp.sum(jnp.exp(gl - gmax), axis=1, keepdims=True)
    gidx = jnp.min(jnp.where(gl == gmax, lane, LANES), axis=1, keepdims=True)
    e_lane = lane - EXPERT_LANE0
    in_grp = (e_lane >= 0) & (e_lane < N_EXPERTS) & ((e_lane >> LOG2_EXPERTS_PER_GROUP) == gidx)
    el = jnp.where(in_grp, logits, neg)
    max1 = jnp.max(el, axis=1, keepdims=True)
    i1 = jnp.min(jnp.where(el == max1, lane, LANES), axis=1, keepdims=True)
    el2 = jnp.where(lane == i1, neg, el)
    max2 = jnp.max(el2, axis=1, keepdims=True)
    i2 = jnp.min(jnp.where(el2 == max2, lane, LANES), axis=1, keepdims=True)
    ratio = jnp.exp(max2 - max1)
    w1 = grp_w / (1.0 + ratio)
    w2 = grp_w * ratio / (1.0 + ratio)
    sel1 = lane == i1
    sel2 = lane == i2
    sel = sel1.astype(F32) + sel2.astype(F32)
    r_i = lax.broadcasted_iota(jnp.int32, (tm, tm), 0)
    c_i = lax.broadcasted_iota(jnp.int32, (tm, tm), 1)
    before = (c_i < r_i).astype(BF16)
    rank = carry_sc[...] + jnp.dot(before, sel.astype(BF16), preferred_element_type=F32)
    new_carry = carry_sc[...] + jnp.sum(sel, axis=0, keepdims=True)
    carry_sc[...] = new_carry
    cnt_ref[...] = jnp.broadcast_to(new_carry, cnt_ref.shape)
    r1 = _lane_pick(sel1, rank)
    r2 = _lane_pick(sel2, rank)
    e1 = (i1 - EXPERT_LANE0).astype(F32)
    e2 = (i2 - EXPERT_LANE0).astype(F32)
    rec = jnp.where(lane == 0, e1, jnp.where(lane == 1, e2, jnp.where(lane == 2, r1, jnp.where(lane == 3, r2, 0.0))))
    route_ref[...] = rec.T[0:8, :]
    wcol_ref[...] = jnp.where(lane == 0, w1, jnp.where(lane == 1, w2, 0.0))


def _outproj_router(merged16, w_out16, h32, g, b, w_router, b_router, alpha):
    n, d = merged16.shape
    tm = min(ROW_TILE, n)
    row16 = pl.BlockSpec((tm, d), lambda i: (i, 0))
    vec = pl.BlockSpec((1, d), lambda i: (0, 0))
    return pl.pallas_call(
        functools.partial(_outproj_router_kernel, alpha=alpha), grid=(n // tm,),
        in_specs=[row16, pl.BlockSpec((d, d), lambda i: (0, 0)), row16, vec, vec,
                  pl.BlockSpec((d, 2 * LANES), lambda i: (0, 0)), pl.BlockSpec((1, LANES), lambda i: (0, 0))],
        out_specs=[row16, pl.BlockSpec((8, tm), lambda i: (0, i)),
                   pl.BlockSpec((tm, LANES), lambda i: (i, 0)), pl.BlockSpec((8, LANES), lambda i: (0, 0))],
        out_shape=[jax.ShapeDtypeStruct((n, d), F32), jax.ShapeDtypeStruct((8, n), F32),
                   jax.ShapeDtypeStruct((n, LANES), F32), jax.ShapeDtypeStruct((8, LANES), F32)],
        scratch_shapes=[pltpu.VMEM((1, LANES), F32)],
        compiler_params=_params("arbitrary"), name="outproj_router",
    )(merged16, w_out16, h32, g.reshape(1, d), b.reshape(1, d), w_router, b_router)


def _plan_kernel(route_ref, cnt_ref, dest_ref, tiles_ref, *, tile, log2_tile):
    cnt = cnt_ref[...].astype(jnp.int32)
    padded = ((cnt + (tile - 1)) >> log2_tile) << log2_tile
    lane = lax.broadcasted_iota(jnp.int32, padded.shape, 1)
    end = padded
    k = 1
    while k < LANES:
        end = end + jnp.where(lane >= k, pltpu.roll(end, k, 1), 0)
        k *= 2
    start = end - padded
    rows = EXPERT_LANE0 + N_EXPERTS
    rows = -(-rows // 8) * 8
    def lane_to_sublane(v):
        return jnp.broadcast_to(v[0:1, :].astype(F32), (LANES, LANES)).T[0:rows, 0:1]

    start_col = lane_to_sublane(start)
    end_col = lane_to_sublane(end)
    route = route_ref[...]
    n = route.shape[1]
    sub = lax.broadcasted_iota(jnp.int32, (rows, n), 0)
    e1 = route[0:1, :].astype(jnp.int32) + EXPERT_LANE0
    e2 = route[1:2, :].astype(jnp.int32) + EXPERT_LANE0
    d1 = jnp.sum(jnp.where(sub == e1, start_col, 0.0), axis=0, keepdims=True) + route[2:3, :]
    d2 = jnp.sum(jnp.where(sub == e2, start_col, 0.0), axis=0, keepdims=True) + route[3:4, :]
    row8 = lax.broadcasted_iota(jnp.int32, dest_ref.shape, 0)
    dest_ref[...] = jnp.where(row8 == 0, d1, jnp.where(row8 == 1, d2, 0.0)).astype(jnp.int32)
    nt = tiles_ref.shape[1]
    t_start = (lax.broadcasted_iota(jnp.int32, (rows, nt), 1) * tile).astype(F32)
    sub_t = lax.broadcasted_iota(jnp.int32, (rows, nt), 0)
    is_expert = (sub_t >= EXPERT_LANE0) & (sub_t < EXPERT_LANE0 + N_EXPERTS)
    done = jnp.sum(jnp.where(is_expert & (end_col <= t_start), 1.0, 0.0), axis=0, keepdims=True)
    expert = jnp.minimum(done, float(N_EXPERTS - 1))
    used = jnp.max(end_col, axis=0, keepdims=True) * (1.0 / tile)
    row8t = lax.broadcasted_iota(jnp.int32, tiles_ref.shape, 0)
    tiles_ref[...] = jnp.where(row8t == 0, expert, jnp.broadcast_to(used, tiles_ref.shape)).astype(jnp.int32)


def _plan(route, counts, tile, n_tiles):
    n = route.shape[1]
    nt_pad = -(-n_tiles // LANES) * LANES
    return pl.pallas_call(
        functools.partial(_plan_kernel, tile=tile, log2_tile=tile.bit_length() - 1),
        out_shape=[jax.ShapeDtypeStruct((8, n), jnp.int32), jax.ShapeDtypeStruct((8, nt_pad), jnp.int32)],
        compiler_params=pltpu.CompilerParams(vmem_limit_bytes=VMEM_LIMIT), name="plan",
    )(route, counts)


def _start_rows(copies, n_rows):
    def start(o, c):
        for u in range(DMA_UNROLL):
            for cp in copies(o * DMA_UNROLL + u):
                cp.start()
        return c

    lax.fori_loop(0, n_rows // DMA_UNROLL, start, 0)


def _wait_rows(copies, n_rows):
    def wait(o, c):
        for u in range(DMA_UNROLL):
            for cp in copies(o * DMA_UNROLL + u):
                cp.wait()
        return c

    lax.fori_loop(0, n_rows // DMA_UNROLL, wait, 0)


def _invert_kernel(d1_ref, d2_ref, tok_ref, *, unroll):
    def clear(o, c):
        for u in range(unroll):
            tok_ref[o * unroll + u] = 0
        return c

    def put(o, c):
        for u in range(unroll):
            n = o * unroll + u
            tok_ref[d1_ref[n]] = n
            tok_ref[d2_ref[n]] = n
        return c

    lax.fori_loop(0, tok_ref.shape[0] // unroll, clear, 0)
    lax.fori_loop(0, d1_ref.shape[0] // unroll, put, 0)


def _invert(d1, d2, n_slots):
    return pl.pallas_call(
        functools.partial(_invert_kernel, unroll=DMA_UNROLL),
        grid_spec=pltpu.PrefetchScalarGridSpec(
            num_scalar_prefetch=2, grid=(),
            in_specs=[], out_specs=pl.BlockSpec(memory_space=pltpu.SMEM)),
        out_shape=jax.ShapeDtypeStruct((n_slots,), jnp.int32), name="invert",
    )(d1, d2)


def _expert_kernel(te_ref, nu_ref, tok_ref, h_hbm, wg_ref, wu_ref, wd_ref, y_ref,
                   xbuf0, xbuf1, wg_sc, wu_sc, wd_sc, sem, *, tile):
    t = pl.program_id(0)
    n_used = nu_ref[0]
    bufs = (xbuf0, xbuf1)

    def row_copy(tt, slot, r):
        src = h_hbm.at[pl.ds(tok_ref[tt * tile + r], 1), :]
        return pltpu.make_async_copy(src, bufs[slot].at[pl.ds(r, 1), :], sem.at[slot])

    @pl.when(t == 0)
    def _():
        _start_rows(lambda r: (row_copy(t, 0, r),), tile)

    nxt = jnp.minimum(t + 1, n_used - 1)

    def tile_step(slot):
        _wait_rows(lambda r: (row_copy(t, slot, r),), tile)

        @pl.when((t == 0) | (te_ref[t] != te_ref[jnp.maximum(t - 1, 0)]))
        def _():
            wg_sc[...] = wg_ref[0].astype(BF16)
            wu_sc[...] = wu_ref[0].astype(BF16)
            wd_sc[...] = wd_ref[0].astype(BF16)

        for r in range(tile):
            row_copy(nxt, 1 - slot, r).start()
        x = bufs[slot][...].astype(BF16)
        gate = jnp.dot(x, wg_sc[...], preferred_element_type=F32)
        up = jnp.dot(x, wu_sc[...], preferred_element_type=F32)
        hid = (gate * jax.nn.sigmoid(gate) * up).astype(BF16)
        y_ref[...] = jnp.dot(hid, wd_sc[...], preferred_element_type=F32)

        @pl.when(t == n_used - 1)
        def _():
            _wait_rows(lambda r: (row_copy(nxt, 1 - slot, r),), tile)

    for slot in range(2):
        pl.when((t < n_used) & (t % 2 == slot))(functools.partial(tile_step, slot))

    @pl.when(t >= n_used)
    def _():
        y_ref[...] = jnp.zeros(y_ref.shape, F32)


def _experts(tile_expert, n_used, tok, h32, w_gate, w_up, w_down, tile):
    d = h32.shape[1]
    f = w_gate.shape[2]
    p = tok.shape[0]
    return pl.pallas_call(
        functools.partial(_expert_kernel, tile=tile),
        grid_spec=pltpu.PrefetchScalarGridSpec(
            num_scalar_prefetch=3, grid=(p // tile,),
            in_specs=[pl.BlockSpec(memory_space=pl.ANY),
                      pl.BlockSpec((1, d, f), lambda t, te, nu, tk: (te[t], 0, 0)),
                      pl.BlockSpec((1, d, f), lambda t, te, nu, tk: (te[t], 0, 0)),
                      pl.BlockSpec((1, f, d), lambda t, te, nu, tk: (te[t], 0, 0))],
            out_specs=pl.BlockSpec((tile, d), lambda t, te, nu, tk: (t, 0)),
            scratch_shapes=[pltpu.VMEM((tile, d), F32), pltpu.VMEM((tile, d), F32),
                            pltpu.VMEM((d, f), BF16), pltpu.VMEM((d, f), BF16),
                            pltpu.VMEM((f, d), BF16), pltpu.SemaphoreType.DMA((2,))]),
        out_shape=jax.ShapeDtypeStruct((p, d), F32),
        compiler_params=_params("arbitrary"), name="experts",
    )(tile_expert, n_used, tok, h32, w_gate, w_up, w_down)


def _combine_kernel(d1_ref, d2_ref, h_ref, wc_ref, g_ref, b_ref, y_hbm, o32_ref, o16_ref,
                    a0, b0, a1, b1, sem, *, tile, alpha, n_steps):
    i = pl.program_id(0)
    bufs = ((a0, b0), (a1, b1))

    def copies(ii, slot, r):
        n = ii * tile + r
        first, second = bufs[slot]
        return (pltpu.make_async_copy(y_hbm.at[pl.ds(d1_ref[n], 1), :], first.at[pl.ds(r, 1), :], sem.at[slot]),
                pltpu.make_async_copy(y_hbm.at[pl.ds(d2_ref[n], 1), :], second.at[pl.ds(r, 1), :], sem.at[slot]))

    @pl.when(i == 0)
    def _():
        _start_rows(lambda r: copies(i, 0, r), tile)

    nxt = jnp.minimum(i + 1, n_steps - 1)

    def tile_step(slot):
        _wait_rows(lambda r: copies(i, slot, r), tile)
        for r in range(tile):
            for cp in copies(nxt, 1 - slot, r):
                cp.start()
        first, second = bufs[slot]
        wc = wc_ref[...]
        ffn = wc[:, 0:1] * first[...] + wc[:, 1:2] * second[...]
        y = _layer_norm_rows(alpha * h_ref[...] + ffn, g_ref[...], b_ref[...])
        o32_ref[...] = y
        o16_ref[...] = y.astype(BF16)

        @pl.when(i == n_steps - 1)
        def _():
            _wait_rows(lambda r: copies(nxt, 1 - slot, r), tile)

    for slot in range(2):
        pl.when(i % 2 == slot)(functools.partial(tile_step, slot))


def _combine(d1, d2, h32, wcol, g, b, y, alpha):
    n, d = h32.shape
    tile = min(GATHER_TILE, n)
    row = pl.BlockSpec((tile, d), lambda i, a, c: (i, 0))
    vec = pl.BlockSpec((1, d), lambda i, a, c: (0, 0))
    return pl.pallas_call(
        functools.partial(_combine_kernel, tile=tile, alpha=alpha, n_steps=n // tile),
        grid_spec=pltpu.PrefetchScalarGridSpec(
            num_scalar_prefetch=2, grid=(n // tile,),
            in_specs=[row, pl.BlockSpec((tile, LANES), lambda i, a, c: (i, 0)), vec, vec,
                      pl.BlockSpec(memory_space=pl.ANY)],
            out_specs=[row, row],
            scratch_shapes=[pltpu.VMEM((tile, d), F32)] * 4 + [pltpu.SemaphoreType.DMA((2,))]),
        out_shape=[jax.ShapeDtypeStruct((n, d), F32), jax.ShapeDtypeStruct((n, d), BF16)],
        compiler_params=_params("arbitrary"), name="combine_ln",
    )(d1, d2, h32, wcol, g.reshape(1, d), b.reshape(1, d), y)


def _rotary_tables(seq):
    half = RET_QK_DIM // 2
    inv_freq = ROPE_BASE ** (-jnp.arange(half, dtype=F32) / half)
    ang = jnp.arange(seq, dtype=jnp.int32).astype(F32)[:, None] * inv_freq[None, :]
    cos, sin = jnp.cos(ang), jnp.sin(ang)
    return jnp.concatenate([cos, cos], axis=1), jnp.concatenate([-sin, sin], axis=1)


def _col_scale(sizes_scales):
    return jnp.concatenate([jnp.full((size,), scale, F32) for size, scale in sizes_scales])


def kernel(x, ln_in_g, ln_in_b, w_in, b_forget, w_branch_fox, w_branch_ret, w_out, ln1_g, ln1_b,
           w_router_group, b_router_group, w_router_expert, b_router_expert, w_gate, w_up, w_down, ln2_g, ln2_b):
    batch, seq, d = x.shape
    n = batch * seq
    depth = w_in.shape[0]
    alpha = (2.0 * depth) ** 0.25
    fox_w = FOX_HEADS * FOX_HEAD_DIM
    ret_qk = RET_HEADS * RET_QK_DIM
    ret_v = RET_HEADS * RET_V_DIM
    d_expert = w_gate.shape[-1]
    o_ff = 3 * fox_w
    o_rq = o_ff + FOX_HEADS
    o_rv = o_rq + 2 * ret_qk
    o_rg = o_rv + ret_v
    o_gate = o_rg + ret_v
    tables = _rotary_tables(seq)
    fox_scale = _col_scale([(fox_w, FOX_HEAD_DIM ** -0.5 * LOG2E), (2 * fox_w, 1.0)])
    rqk_scale = _col_scale([(ret_qk, 1.0), (ret_qk, RET_QK_DIM ** -0.5)])
    ones_v = jnp.ones((ret_v,), F32)
    ones_g = jnp.ones((2 * d,), F32)
    n_tiles = (N_EXPERTS * (EXPERT_TILE - 1) + 2 * n) // EXPERT_TILE + 1
    n_slots = n_tiles * EXPERT_TILE

    h32, h16 = _layer_norm(x.reshape(n, d), ln_in_g, ln_in_b)
    for l in range(depth):
        w = w_in[l]
        qkv = _proj(h16, w[:, :o_ff].astype(BF16), fox_scale, "none", seq)
        aug = _forget_cumsum(h16, w[:, o_ff:o_rq], b_forget[l], batch, seq)
        rqk = _proj(h16, w[:, o_rq:o_rv].astype(BF16), rqk_scale, "rotary", seq, tables)
        rv = _proj(h16, w[:, o_rv:o_rg].astype(BF16), ones_v, "none", seq)
        rg = _proj(h16, w[:, o_rg:o_gate].astype(BF16), ones_v, "silu", seq)
        gates = _proj(h16, w[:, o_gate:].astype(BF16), ones_g, "sigmoid", seq)
        y_fox = _fox_attention(qkv, aug, batch, seq)
        y_ret = _retention(rqk, rv, rg, batch, seq)
        merged = _merge(y_fox, y_ret, w_branch_fox[l].astype(BF16), w_branch_ret[l].astype(BF16), gates)

        w_router = jnp.zeros((d, LANES), F32)
        w_router = w_router.at[:, :N_GROUPS].set(w_router_group[l])
        w_router = w_router.at[:, EXPERT_LANE0:EXPERT_LANE0 + N_EXPERTS].set(w_router_expert[l])
        b_router = jnp.zeros((1, LANES), F32)
        b_router = b_router.at[0, :N_GROUPS].set(b_router_group[l])
        b_router = b_router.at[0, EXPERT_LANE0:EXPERT_LANE0 + N_EXPERTS].set(b_router_expert[l])
        w_router_hi = lax.reduce_precision(w_router, exponent_bits=8, mantissa_bits=7)
        w_router_lo = w_router - w_router_hi
        h1, route, wcol, counts = _outproj_router(
            merged, w_out[l].astype(BF16), h32, ln1_g[l], ln1_b[l],
            jnp.concatenate([w_router_hi, w_router_lo], axis=1).astype(BF16), b_router, alpha)
        dest, tiles = _plan(route, counts, EXPERT_TILE, n_tiles)
        d1, d2 = dest[0], dest[1]
        tok = _invert(d1, d2, n_slots)
        y = _experts(tiles[0, :n_tiles], tiles[1, :1], tok, h1,
                     w_gate[l].reshape(N_EXPERTS, d, d_expert), w_up[l].reshape(N_EXPERTS, d, d_expert),
                     w_down[l].reshape(N_EXPERTS, d_expert, d), EXPERT_TILE)
        h32, h16 = _combine(d1, d2, h1, wcol, ln2_g[l], ln2_b[l], y, alpha)
    return h32.reshape(batch, seq, d)
```

```python
import functools

import jax
import jax.numpy as jnp
from jax import lax
from jax.experimental import pallas as pl
from jax.experimental.pallas import tpu as pltpu

F32 = jnp.float32
BF16 = jnp.bfloat16

CHUNK = 64
FOX_HEADS = 8
FOX_HEAD_DIM = 128
RET_HEADS = 8
RET_QK_DIM = 128
RET_V_DIM = 256
ROPE_BASE = 10000.0
N_GROUPS = 4
EXPERTS_PER_GROUP = 8
N_EXPERTS = N_GROUPS * EXPERTS_PER_GROUP
LOG2_EXPERTS_PER_GROUP = EXPERTS_PER_GROUP.bit_length() - 1
LN_EPS = 1e-5
RMS_EPS = 1e-6

LANES = 128
EXPERT_LANE0 = N_GROUPS
VMEM_LIMIT = 56 * 1024 * 1024

ROW_TILE = 512
MATMUL_ROWS = 1024
DMA_UNROLL = 32
LOG2E = 1.4426950408889634
ATTN_TILE = 256
RET_TILE = 256
EXPERT_TILE = 256
GATHER_TILE = 256


def _params(*sem):
    return pltpu.CompilerParams(dimension_semantics=sem, vmem_limit_bytes=VMEM_LIMIT)


def _pack_halves(x):
    w = x.shape[1] // 2
    lo = lax.bitcast_convert_type(x[:, :w].astype(BF16).astype(F32), jnp.uint32)
    hi = lax.bitcast_convert_type(x[:, w:].astype(BF16).astype(F32), jnp.uint32)
    return (lo >> 16) | hi


def _unpack_halves(u):
    lo = lax.bitcast_convert_type(u << 16, F32)
    hi = lax.bitcast_convert_type(u & jnp.uint32(0xFFFF0000), F32)
    return lo, hi


def _layer_norm_rows(x, g, b):
    mu = jnp.mean(x, axis=-1, keepdims=True)
    xc = x - mu
    var = jnp.mean(xc * xc, axis=-1, keepdims=True)
    return xc * lax.rsqrt(var + LN_EPS) * g + b


def _ln_kernel(x_ref, g_ref, b_ref, o32_ref, o16_ref):
    y = _layer_norm_rows(x_ref[...], g_ref[...], b_ref[...])
    o32_ref[...] = y
    o16_ref[...] = y.astype(BF16)


def _layer_norm(x, g, b):
    n, d = x.shape
    tm = min(ROW_TILE, n)
    row = pl.BlockSpec((tm, d), lambda i: (i, 0))
    vec = pl.BlockSpec((1, d), lambda i: (0, 0))
    return pl.pallas_call(
        _ln_kernel, grid=(n // tm,), in_specs=[row, vec, vec], out_specs=[row, row],
        out_shape=[jax.ShapeDtypeStruct((n, d), F32), jax.ShapeDtypeStruct((n, d), BF16)],
        compiler_params=_params("parallel"), name="ln_in",
    )(x, g.reshape(1, d), b.reshape(1, d))


def _proj_kernel(x_ref, w_ref, s_ref, *rest, act):
    acc = jnp.dot(x_ref[...], w_ref[...], preferred_element_type=F32) * s_ref[...]
    if act == "rotary":
        cos_ref, sin_ref, o_ref = rest
        cos, sin = cos_ref[...], sin_ref[...]
        for c in range(acc.shape[1] // LANES):
            blk = acc[:, c * LANES:(c + 1) * LANES]
            rot = blk * cos + pltpu.roll(blk, LANES // 2, 1) * sin
            o_ref[:, c * LANES:(c + 1) * LANES] = rot.astype(o_ref.dtype)
        return
    (o_ref,) = rest
    if act == "silu":
        acc = acc * jax.nn.sigmoid(acc)
    elif act == "sigmoid":
        acc = jax.nn.sigmoid(acc)
    o_ref[...] = acc.astype(o_ref.dtype)


def _proj(x16, w16, colscale, act, seq, tables=None, tn=1024):
    n, k = x16.shape
    m = w16.shape[1]
    tm = min(MATMUL_ROWS, seq)
    in_specs = [pl.BlockSpec((tm, k), lambda i, j: (i, 0)),
                pl.BlockSpec((k, tn), lambda i, j: (0, j)),
                pl.BlockSpec((1, tn), lambda i, j: (0, j))]
    args = [x16, w16, colscale.reshape(1, m)]
    if act == "rotary":
        per_seq = seq // tm
        tab = pl.BlockSpec((tm, LANES), lambda i, j: (i % per_seq, 0))
        in_specs += [tab, tab]
        args += list(tables)
    return pl.pallas_call(
        functools.partial(_proj_kernel, act=act), grid=(n // tm, m // tn),
        in_specs=in_specs, out_specs=pl.BlockSpec((tm, tn), lambda i, j: (i, j)),
        out_shape=jax.ShapeDtypeStruct((n, m), BF16),
        compiler_params=_params("parallel", "parallel"), name="proj_" + act,
    )(*args)


def _forget_kernel(h_ref, w_ref, b_ref, o_ref, *, heads, block):
    z = jnp.dot(h_ref[...], w_ref[...], preferred_element_type=F32) + b_ref[...]
    lf = (jnp.minimum(z, 0.0) - jnp.log1p(jnp.exp(-jnp.abs(z)))) * LOG2E
    seq = lf.shape[0]
    r_i = lax.broadcasted_iota(jnp.int32, (block, block), 0)
    c_i = lax.broadcasted_iota(jnp.int32, (block, block), 1)
    upto = (c_i <= r_i).astype(F32)
    lane = lax.broadcasted_iota(jnp.int32, (block, LANES), 1)
    carry = jnp.zeros((1, LANES), F32)
    for blk in range(seq // block):
        rows = slice(blk * block, (blk + 1) * block)
        cum = jnp.dot(upto, lf[rows], preferred_element_type=F32, precision=lax.Precision.HIGHEST) + carry
        carry = cum[block - 1:block, :]
        for h in range(heads):
            col = jnp.broadcast_to(cum[:, h:h + 1], (block, LANES))
            hi = col.astype(BF16).astype(F32)
            mid = (col - hi).astype(BF16).astype(F32)
            lo = col - hi - mid
            split = jnp.where(lane == 0, hi, jnp.where(lane == 1, mid, jnp.where(lane == 2, lo, 0.0)))
            o_ref[rows, h * LANES:(h + 1) * LANES] = split.astype(BF16)


def _forget_cumsum(h16, w_forget, b_forget, batch, seq):
    d, heads = w_forget.shape
    w_pad = jnp.zeros((d, LANES), BF16).at[:, :heads].set(w_forget.astype(BF16))
    b_pad = jnp.zeros((1, LANES), F32).at[0, :heads].set(b_forget.astype(F32))
    return pl.pallas_call(
        functools.partial(_forget_kernel, heads=heads, block=min(256, seq)), grid=(batch,),
        in_specs=[pl.BlockSpec((seq, d), lambda b: (b, 0)),
                  pl.BlockSpec((d, LANES), lambda b: (0, 0)),
                  pl.BlockSpec((1, LANES), lambda b: (0, 0))],
        out_specs=pl.BlockSpec((seq, heads * LANES), lambda b: (b, 0)),
        out_shape=jax.ShapeDtypeStruct((batch * seq, heads * LANES), BF16),
        compiler_params=_params("parallel"), name="forget_cumsum",
    )(h16, w_pad, b_pad)


N_SPLIT = 3


def _fox_kernel(q_ref, k_ref, v_ref, a_ref, o_ref, acc_sc, kaug_sc, *, tile, hp):
    qi = pl.program_id(2)
    dh = FOX_HEAD_DIM

    @pl.when(qi == 0)
    def _():
        for hh in range(hp):
            kaug_sc[:, 2 * hh * dh:(2 * hh + 1) * dh] = k_ref[:, hh * dh:(hh + 1) * dh]
            kaug_sc[:, (2 * hh + 1) * dh:(2 * hh + 2) * dh] = a_ref[:, hh * dh:(hh + 1) * dh]

    lane = lax.broadcasted_iota(jnp.int32, (tile, dh), 1)
    q_extra = jnp.where(lane < N_SPLIT, -1.0, 0.0).astype(BF16)
    q_aug = [jnp.concatenate([q_ref[:, hh * dh:(hh + 1) * dh], q_extra], axis=1) for hh in range(hp)]
    acc_sc[...] = jnp.zeros(acc_sc.shape, F32)

    def block(j, stats, diagonal):
        rows = pl.ds(pl.multiple_of(j * tile, tile), tile)
        s_t = []
        for hh in range(hp):
            s = lax.dot_general(kaug_sc[rows, 2 * hh * dh:(2 * hh + 2) * dh], q_aug[hh],
                                (((1,), (1,)), ((), ())), preferred_element_type=F32)
            if diagonal:
                key = lax.broadcasted_iota(jnp.int32, s.shape, 0)
                qry = lax.broadcasted_iota(jnp.int32, s.shape, 1)
                s = jnp.where(key <= qry, s, -jnp.inf)
            s_t.append(s)
        out, scale, pv = [], [], []
        for hh in range(hp):
            m_prev, l_prev = stats[hh]
            m_new = jnp.maximum(m_prev, jnp.max(s_t[hh], axis=0, keepdims=True))
            a = jnp.exp2(m_prev - m_new)
            p = jnp.exp2(s_t[hh] - m_new)
            out.append((m_new, a * l_prev + jnp.sum(p, axis=0, keepdims=True)))
            scale.append(a)
            pv.append(lax.dot_general(v_ref[rows, hh * dh:(hh + 1) * dh], p.astype(BF16),
                                      (((0,), (0,)), ((), ())), preferred_element_type=F32))
        for hh in range(hp):
            acc_sc[hh] = scale[hh] * acc_sc[hh] + pv[hh]
        return tuple(out)

    init = tuple((jnp.full((1, tile), -jnp.inf, F32), jnp.zeros((1, tile), F32)) for _ in range(hp))
    stats = lax.fori_loop(0, qi, lambda j, st: block(j, st, False), init)
    stats = block(qi, stats, True)
    for hh in range(hp):
        o_ref[:, hh * dh:(hh + 1) * dh] = (acc_sc[hh] / stats[hh][1]).T.astype(o_ref.dtype)


def _fox_attention(qkv16, aug16, batch, seq, hp=8):
    heads, dh = FOX_HEADS, FOX_HEAD_DIM
    tile = min(ATTN_TILE, seq)
    nq = seq // tile
    groups = heads // hp
    w = hp * dh
    return pl.pallas_call(
        functools.partial(_fox_kernel, tile=tile, hp=hp), grid=(batch, groups, nq),
        in_specs=[pl.BlockSpec((tile, w), lambda b, g, i: (b * nq + i, g)),
                  pl.BlockSpec((seq, w), lambda b, g, i: (b, groups + g)),
                  pl.BlockSpec((seq, w), lambda b, g, i: (b, 2 * groups + g)),
                  pl.BlockSpec((seq, w), lambda b, g, i: (b, g))],
        out_specs=pl.BlockSpec((tile, w), lambda b, g, i: (b * nq + i, g)),
        out_shape=jax.ShapeDtypeStruct((batch * seq, heads * dh), BF16),
        scratch_shapes=[pltpu.VMEM((hp, dh, tile), F32), pltpu.VMEM((seq, 2 * w), BF16)],
        compiler_params=_params("parallel", "parallel", "arbitrary"), name="fox_attention",
    )(qkv16, qkv16, qkv16, aug16)


def _ret_kernel(q_ref, k_ref, v_ref, g_ref, dm_ref, qd_ref, kd_ref, cd_ref, o_ref, st_sc, *, hp):
    @pl.when(pl.program_id(2) == 0)
    def _():
        st_sc[...] = jnp.zeros(st_sc.shape, F32)

    dk, dv = RET_QK_DIM, RET_V_DIM
    for hh in range(hp):
        q = q_ref[:, hh * dk:(hh + 1) * dk]
        k = k_ref[:, hh * dk:(hh + 1) * dk]
        v = v_ref[:, hh * dv:(hh + 1) * dv]
        scores = lax.dot_general(q, k, (((1,), (1,)), ((), ())), preferred_element_type=F32) * dm_ref[hh]
        intra = jnp.dot(scores.astype(BF16), v, preferred_element_type=F32)
        state = st_sc[hh]
        cross = jnp.dot(q, state.astype(BF16), preferred_element_type=F32) * qd_ref[hh]
        k_dec = (k.astype(F32) * kd_ref[hh]).astype(BF16)
        st_sc[hh] = state * cd_ref[hh] + lax.dot_general(
            k_dec, v, (((0,), (0,)), ((), ())), preferred_element_type=F32)
        o = intra + cross
        o = o * lax.rsqrt(jnp.mean(o * o, axis=-1, keepdims=True) + RMS_EPS)
        o_ref[:, hh * dv:(hh + 1) * dv] = (g_ref[:, hh * dv:(hh + 1) * dv].astype(F32) * o).astype(o_ref.dtype)


def _retention_tables(tile):
    heads = RET_HEADS
    log_gamma = jnp.log1p(-(2.0 ** (-5.0 - jnp.arange(heads, dtype=F32))))
    pos = jnp.arange(tile, dtype=F32)
    chunk = jnp.arange(tile, dtype=jnp.int32) // CHUNK
    dist = pos[:, None] - pos[None, :]
    same = chunk[:, None] == chunk[None, :]
    earlier = chunk[None, :] < chunk[:, None]
    expo = jnp.where(same, jnp.abs(dist), dist)
    dmask = jnp.where((same | earlier)[None], jnp.exp(log_gamma[:, None, None] * expo[None]), 0.0)
    q_dec = jnp.broadcast_to(jnp.exp(log_gamma[:, None] * (pos + 1.0))[:, :, None], (heads, tile, RET_V_DIM))
    k_dec = jnp.broadcast_to(jnp.exp(log_gamma[:, None] * (tile - 1 - pos))[:, :, None], (heads, tile, RET_QK_DIM))
    c_dec = jnp.broadcast_to(jnp.exp(log_gamma * tile)[:, None, None], (heads, 1, RET_V_DIM))
    return dmask, q_dec, k_dec, c_dec


def _retention(rqk16, rv16, rg16, batch, seq, hp=4):
    heads, dk, dv = RET_HEADS, RET_QK_DIM, RET_V_DIM
    tile = min(RET_TILE, seq)
    nt = seq // tile
    groups = heads // hp
    dmask, q_dec, k_dec, c_dec = _retention_tables(tile)
    val = pl.BlockSpec((tile, hp * dv), lambda b, g, t: (b * nt + t, g))
    return pl.pallas_call(
        functools.partial(_ret_kernel, hp=hp), grid=(batch, groups, nt),
        in_specs=[pl.BlockSpec((tile, hp * dk), lambda b, g, t: (b * nt + t, g)),
                  pl.BlockSpec((tile, hp * dk), lambda b, g, t: (b * nt + t, groups + g)),
                  val, val,
                  pl.BlockSpec((hp, tile, tile), lambda b, g, t: (g, 0, 0)),
                  pl.BlockSpec((hp, tile, dv), lambda b, g, t: (g, 0, 0)),
                  pl.BlockSpec((hp, tile, dk), lambda b, g, t: (g, 0, 0)),
                  pl.BlockSpec((hp, 1, dv), lambda b, g, t: (g, 0, 0))],
        out_specs=val,
        out_shape=jax.ShapeDtypeStruct((batch * seq, heads * dv), BF16),
        scratch_shapes=[pltpu.VMEM((hp, dk, dv), F32)],
        compiler_params=_params("parallel", "parallel", "arbitrary"), name="retention",
    )(rqk16, rqk16, rv16, rg16, dmask, q_dec, k_dec, c_dec)


def _merge_kernel(yf_ref, yr_ref, wf_ref, wr_ref, gf_ref, gr_ref, o_ref):
    fox = jnp.dot(yf_ref[...], wf_ref[...], preferred_element_type=F32)
    ret = jnp.dot(yr_ref[...], wr_ref[...], preferred_element_type=F32)
    o_ref[...] = (gf_ref[...].astype(F32) * fox + gr_ref[...].astype(F32) * ret).astype(o_ref.dtype)


def _merge(y_fox, y_ret, w_fox16, w_ret16, gates16, tn=512):
    n, kf = y_fox.shape
    kr = y_ret.shape[1]
    d = w_fox16.shape[1]
    tm = min(MATMUL_ROWS, n)
    nj = d // tn
    return pl.pallas_call(
        _merge_kernel, grid=(n // tm, nj),
        in_specs=[pl.BlockSpec((tm, kf), lambda i, j: (i, 0)),
                  pl.BlockSpec((tm, kr), lambda i, j: (i, 0)),
                  pl.BlockSpec((kf, tn), lambda i, j: (0, j)),
                  pl.BlockSpec((kr, tn), lambda i, j: (0, j)),
                  pl.BlockSpec((tm, tn), lambda i, j: (i, j)),
                  pl.BlockSpec((tm, tn), lambda i, j: (i, nj + j))],
        out_specs=pl.BlockSpec((tm, tn), lambda i, j: (i, j)),
        out_shape=jax.ShapeDtypeStruct((n, d), BF16),
        compiler_params=_params("parallel", "parallel"), name="merge",
    )(y_fox, y_ret, w_fox16, w_ret16, gates16, gates16)


def _lane_pick(mask, values):
    return jnp.sum(jnp.where(mask, values, 0.0), axis=1, keepdims=True)


def _outproj_router_kernel(m_ref, w_ref, h_ref, g_ref, b_ref, wr_ref, br_ref,
                           h1_ref, h1p_ref, route_ref, wcol_ref, cnt_ref, carry_sc, *, alpha):
    i = pl.program_id(0)

    @pl.when(i == 0)
    def _():
        carry_sc[...] = jnp.zeros(carry_sc.shape, F32)

    mix = jnp.dot(m_ref[...], w_ref[...], preferred_element_type=F32)
    hn = _layer_norm_rows(alpha * h_ref[...] + mix, g_ref[...], b_ref[...])
    h1_ref[...] = hn
    h1p_ref[...] = _pack_halves(hn)

    hn_hi = hn.astype(BF16)
    hn_lo = (hn - hn_hi.astype(F32)).astype(BF16)
    part = jnp.dot(hn_hi, wr_ref[...], preferred_element_type=F32)
    logits = (part[:, :LANES] + part[:, LANES:] + br_ref[...]
              + jnp.dot(hn_lo, wr_ref[:, :LANES], preferred_element_type=F32))
    tm = logits.shape[0]
    lane = lax.broadcasted_iota(jnp.int32, logits.shape, 1)
    neg = -jnp.inf
    gl = jnp.where(lane < N_GROUPS, logits, neg)
    gmax = jnp.max(gl, axis=1, keepdims=True)
    grp_w = 1.0 / jnp.sum(jnp.exp(gl - gmax), axis=1, keepdims=True)
    gidx = jnp.min(jnp.where(gl == gmax, lane, LANES), axis=1, keepdims=True)
    e_lane = lane - EXPERT_LANE0
    in_grp = (e_lane >= 0) & (e_lane < N_EXPERTS) & ((e_lane >> LOG2_EXPERTS_PER_GROUP) == gidx)
    el = jnp.where(in_grp, logits, neg)
    max1 = jnp.max(el, axis=1, keepdims=True)
    i1 = jnp.min(jnp.where(el == max1, lane, LANES), axis=1, keepdims=True)
    el2 = jnp.where(lane == i1, neg, el)
    max2 = jnp.max(el2, axis=1, keepdims=True)
    i2 = jnp.min(jnp.where(el2 == max2, lane, LANES), axis=1, keepdims=True)
    ratio = jnp.exp(max2 - max1)
    w1 = grp_w / (1.0 + ratio)
    w2 = grp_w * ratio / (1.0 + ratio)
    sel1 = lane == i1
    sel2 = lane == i2
    sel = sel1.astype(F32) + sel2.astype(F32)
    r_i = lax.broadcasted_iota(jnp.int32, (tm, tm), 0)
    c_i = lax.broadcasted_iota(jnp.int32, (tm, tm), 1)
    before = (c_i < r_i).astype(BF16)
    rank = carry_sc[...] + jnp.dot(before, sel.astype(BF16), preferred_element_type=F32)
    new_carry = carry_sc[...] + jnp.sum(sel, axis=0, keepdims=True)
    carry_sc[...] = new_carry
    cnt_ref[...] = jnp.broadcast_to(new_carry, cnt_ref.shape)
    r1 = _lane_pick(sel1, rank)
    r2 = _lane_pick(sel2, rank)
    e1 = (i1 - EXPERT_LANE0).astype(F32)
    e2 = (i2 - EXPERT_LANE0).astype(F32)
    rec = jnp.where(lane == 0, e1, jnp.where(lane == 1, e2, jnp.where(lane == 2, r1, jnp.where(lane == 3, r2, 0.0))))
    route_ref[...] = rec.T[0:8, :]
    wcol_ref[...] = jnp.where(lane == 0, w1, jnp.where(lane == 1, w2, 0.0))


def _outproj_router(merged16, w_out16, h32, g, b, w_router, b_router, alpha):
    n, d = merged16.shape
    tm = min(ROW_TILE, n)
    row16 = pl.BlockSpec((tm, d), lambda i: (i, 0))
    vec = pl.BlockSpec((1, d), lambda i: (0, 0))
    return pl.pallas_call(
        functools.partial(_outproj_router_kernel, alpha=alpha), grid=(n // tm,),
        in_specs=[row16, pl.BlockSpec((d, d), lambda i: (0, 0)), row16, vec, vec,
                  pl.BlockSpec((d, 2 * LANES), lambda i: (0, 0)), pl.BlockSpec((1, LANES), lambda i: (0, 0))],
        out_specs=[row16, pl.BlockSpec((tm, d // 2), lambda i: (i, 0)), pl.BlockSpec((8, tm), lambda i: (0, i)),
                   pl.BlockSpec((tm, LANES), lambda i: (i, 0)), pl.BlockSpec((8, LANES), lambda i: (0, 0))],
        out_shape=[jax.ShapeDtypeStruct((n, d), F32), jax.ShapeDtypeStruct((n, d // 2), jnp.uint32),
                   jax.ShapeDtypeStruct((8, n), F32),
                   jax.ShapeDtypeStruct((n, LANES), F32), jax.ShapeDtypeStruct((8, LANES), F32)],
        scratch_shapes=[pltpu.VMEM((1, LANES), F32)],
        compiler_params=_params("arbitrary"), name="outproj_router",
    )(merged16, w_out16, h32, g.reshape(1, d), b.reshape(1, d), w_router, b_router)


def _plan_kernel(route_ref, cnt_ref, dest_ref, tiles_ref, *, tile, log2_tile):
    cnt = cnt_ref[...].astype(jnp.int32)
    padded = ((cnt + (tile - 1)) >> log2_tile) << log2_tile
    lane = lax.broadcasted_iota(jnp.int32, padded.shape, 1)
    end = padded
    k = 1
    while k < LANES:
        end = end + jnp.where(lane >= k, pltpu.roll(end, k, 1), 0)
        k *= 2
    start = end - padded
    rows = EXPERT_LANE0 + N_EXPERTS
    rows = -(-rows // 8) * 8
    def lane_to_sublane(v):
        return jnp.broadcast_to(v[0:1, :].astype(F32), (LANES, LANES)).T[0:rows, 0:1]

    start_col = lane_to_sublane(start)
    end_col = lane_to_sublane(end)
    route = route_ref[...]
    n = route.shape[1]
    sub = lax.broadcasted_iota(jnp.int32, (rows, n), 0)
    e1 = route[0:1, :].astype(jnp.int32) + EXPERT_LANE0
    e2 = route[1:2, :].astype(jnp.int32) + EXPERT_LANE0
    d1 = jnp.sum(jnp.where(sub == e1, start_col, 0.0), axis=0, keepdims=True) + route[2:3, :]
    d2 = jnp.sum(jnp.where(sub == e2, start_col, 0.0), axis=0, keepdims=True) + route[3:4, :]
    row8 = lax.broadcasted_iota(jnp.int32, dest_ref.shape, 0)
    dest_ref[...] = jnp.where(row8 == 0, d1, jnp.where(row8 == 1, d2, 0.0)).astype(jnp.int32)
    nt = tiles_ref.shape[1]
    t_start = (lax.broadcasted_iota(jnp.int32, (rows, nt), 1) * tile).astype(F32)
    sub_t = lax.broadcasted_iota(jnp.int32, (rows, nt), 0)
    is_expert = (sub_t >= EXPERT_LANE0) & (sub_t < EXPERT_LANE0 + N_EXPERTS)
    done = jnp.sum(jnp.where(is_expert & (end_col <= t_start), 1.0, 0.0), axis=0, keepdims=True)
    expert = jnp.minimum(done, float(N_EXPERTS - 1))
    used = jnp.max(end_col, axis=0, keepdims=True) * (1.0 / tile)
    row8t = lax.broadcasted_iota(jnp.int32, tiles_ref.shape, 0)
    tiles_ref[...] = jnp.where(row8t == 0, expert, jnp.broadcast_to(used, tiles_ref.shape)).astype(jnp.int32)


def _plan(route, counts, tile, n_tiles):
    n = route.shape[1]
    nt_pad = -(-n_tiles // LANES) * LANES
    return pl.pallas_call(
        functools.partial(_plan_kernel, tile=tile, log2_tile=tile.bit_length() - 1),
        out_shape=[jax.ShapeDtypeStruct((8, n), jnp.int32), jax.ShapeDtypeStruct((8, nt_pad), jnp.int32)],
        compiler_params=pltpu.CompilerParams(vmem_limit_bytes=VMEM_LIMIT), name="plan",
    )(route, counts)


def _start_rows(copies, n_rows):
    def start(o, c):
        for u in range(DMA_UNROLL):
            for cp in copies(o * DMA_UNROLL + u):
                cp.start()
        return c

    lax.fori_loop(0, n_rows // DMA_UNROLL, start, 0)


def _wait_rows(copies, n_rows):
    def wait(o, c):
        for u in range(DMA_UNROLL):
            for cp in copies(o * DMA_UNROLL + u):
                cp.wait()
        return c

    lax.fori_loop(0, n_rows // DMA_UNROLL, wait, 0)


def _invert_kernel(d1_ref, d2_ref, tok_ref, *, unroll):
    def clear(o, c):
        for u in range(unroll):
            tok_ref[o * unroll + u] = 0
        return c

    def put(o, c):
        for u in range(unroll):
            n = o * unroll + u
            tok_ref[d1_ref[n]] = n
            tok_ref[d2_ref[n]] = n
        return c

    lax.fori_loop(0, tok_ref.shape[0] // unroll, clear, 0)
    lax.fori_loop(0, d1_ref.shape[0] // unroll, put, 0)


def _invert(d1, d2, n_slots):
    return pl.pallas_call(
        functools.partial(_invert_kernel, unroll=DMA_UNROLL),
        grid_spec=pltpu.PrefetchScalarGridSpec(
            num_scalar_prefetch=2, grid=(),
            in_specs=[], out_specs=pl.BlockSpec(memory_space=pltpu.SMEM)),
        out_shape=jax.ShapeDtypeStruct((n_slots,), jnp.int32), name="invert",
    )(d1, d2)


def _expert_kernel(te_ref, nu_ref, tok_ref, h_hbm, wg_ref, wu_ref, wd_ref, y_ref,
                   xbuf0, xbuf1, wg_sc, wu_sc, wd_sc, sem, *, tile):
    t = pl.program_id(0)
    n_used = nu_ref[0]
    bufs = (xbuf0, xbuf1)

    def row_copy(tt, slot, r):
        src = h_hbm.at[pl.ds(tok_ref[tt * tile + r], 1), :]
        return pltpu.make_async_copy(src, bufs[slot].at[pl.ds(r, 1), :], sem.at[slot])

    @pl.when(t == 0)
    def _():
        _start_rows(lambda r: (row_copy(t, 0, r),), tile)

    nxt = jnp.minimum(t + 1, n_used - 1)

    def tile_step(slot):
        _wait_rows(lambda r: (row_copy(t, slot, r),), tile)

        @pl.when((t == 0) | (te_ref[t] != te_ref[jnp.maximum(t - 1, 0)]))
        def _():
            wg_sc[...] = wg_ref[0].astype(BF16)
            wu_sc[...] = wu_ref[0].astype(BF16)
            wd_sc[...] = wd_ref[0].astype(BF16)

        for r in range(tile):
            row_copy(nxt, 1 - slot, r).start()
        x_lo, x_hi = _unpack_halves(bufs[slot][...])
        x = jnp.concatenate([x_lo.astype(BF16), x_hi.astype(BF16)], axis=1)
        gate = jnp.dot(x, wg_sc[...], preferred_element_type=F32)
        up = jnp.dot(x, wu_sc[...], preferred_element_type=F32)
        hid = (gate * jax.nn.sigmoid(gate) * up).astype(BF16)
        y_ref[...] = _pack_halves(jnp.dot(hid, wd_sc[...], preferred_element_type=F32))

        @pl.when(t == n_used - 1)
        def _():
            _wait_rows(lambda r: (row_copy(nxt, 1 - slot, r),), tile)

    for slot in range(2):
        pl.when((t < n_used) & (t % 2 == slot))(functools.partial(tile_step, slot))

    @pl.when(t >= n_used)
    def _():
        y_ref[...] = jnp.zeros(y_ref.shape, y_ref.dtype)


def _experts(tile_expert, n_used, tok, h_packed, w_gate, w_up, w_down, layer, tile):
    dp = h_packed.shape[1]
    d, f = w_gate.shape[1], w_gate.shape[2]
    p = tok.shape[0]
    first = layer * N_EXPERTS

    def weights(shape):
        return pl.BlockSpec((1,) + shape, lambda t, te, nu, tk: (first + te[t], 0, 0))

    return pl.pallas_call(
        functools.partial(_expert_kernel, tile=tile),
        grid_spec=pltpu.PrefetchScalarGridSpec(
            num_scalar_prefetch=3, grid=(p // tile,),
            in_specs=[pl.BlockSpec(memory_space=pl.ANY), weights((d, f)), weights((d, f)), weights((f, d))],
            out_specs=pl.BlockSpec((tile, dp), lambda t, te, nu, tk: (t, 0)),
            scratch_shapes=[pltpu.VMEM((tile, dp), jnp.uint32), pltpu.VMEM((tile, dp), jnp.uint32),
                            pltpu.VMEM((d, f), BF16), pltpu.VMEM((d, f), BF16),
                            pltpu.VMEM((f, d), BF16), pltpu.SemaphoreType.DMA((2,))]),
        out_shape=jax.ShapeDtypeStruct((p, dp), jnp.uint32),
        compiler_params=_params("arbitrary"), name="experts",
    )(tile_expert, n_used, tok, h_packed, w_gate, w_up, w_down)


def _combine_kernel(d1_ref, d2_ref, h_ref, wc_ref, g_ref, b_ref, y_hbm, o32_ref, o16_ref,
                    a0, b0, a1, b1, sem, *, tile, alpha, n_steps):
    i = pl.program_id(0)
    bufs = ((a0, b0), (a1, b1))

    def copies(ii, slot, r):
        n = ii * tile + r
        first, second = bufs[slot]
        return (pltpu.make_async_copy(y_hbm.at[pl.ds(d1_ref[n], 1), :], first.at[pl.ds(r, 1), :], sem.at[slot]),
                pltpu.make_async_copy(y_hbm.at[pl.ds(d2_ref[n], 1), :], second.at[pl.ds(r, 1), :], sem.at[slot]))

    @pl.when(i == 0)
    def _():
        _start_rows(lambda r: copies(i, 0, r), tile)

    nxt = jnp.minimum(i + 1, n_steps - 1)

    def tile_step(slot):
        _wait_rows(lambda r: copies(i, slot, r), tile)
        for r in range(tile):
            for cp in copies(nxt, 1 - slot, r):
                cp.start()
        first, second = bufs[slot]
        wc = wc_ref[...]
        w1, w2 = wc[:, 0:1], wc[:, 1:2]
        f_lo, f_hi = _unpack_halves(first[...])
        s_lo, s_hi = _unpack_halves(second[...])
        ffn = jnp.concatenate([w1 * f_lo + w2 * s_lo, w1 * f_hi + w2 * s_hi], axis=1)
        y = _layer_norm_rows(alpha * h_ref[...] + ffn, g_ref[...], b_ref[...])
        o32_ref[...] = y
        o16_ref[...] = y.astype(BF16)

        @pl.when(i == n_steps - 1)
        def _():
            _wait_rows(lambda r: copies(nxt, 1 - slot, r), tile)

    for slot in range(2):
        pl.when(i % 2 == slot)(functools.partial(tile_step, slot))


def _combine(d1, d2, h32, wcol, g, b, y, alpha):
    n, d = h32.shape
    tile = min(GATHER_TILE, n)
    row = pl.BlockSpec((tile, d), lambda i, a, c: (i, 0))
    vec = pl.BlockSpec((1, d), lambda i, a, c: (0, 0))
    return pl.pallas_call(
        functools.partial(_combine_kernel, tile=tile, alpha=alpha, n_steps=n // tile),
        grid_spec=pltpu.PrefetchScalarGridSpec(
            num_scalar_prefetch=2, grid=(n // tile,),
            in_specs=[row, pl.BlockSpec((tile, LANES), lambda i, a, c: (i, 0)), vec, vec,
                      pl.BlockSpec(memory_space=pl.ANY)],
            out_specs=[row, row],
            scratch_shapes=[pltpu.VMEM((tile, y.shape[1]), y.dtype)] * 4 + [pltpu.SemaphoreType.DMA((2,))]),
        out_shape=[jax.ShapeDtypeStruct((n, d), F32), jax.ShapeDtypeStruct((n, d), BF16)],
        compiler_params=_params("arbitrary"), name="combine_ln",
    )(d1, d2, h32, wcol, g.reshape(1, d), b.reshape(1, d), y)


def _rotary_tables(seq):
    half = RET_QK_DIM // 2
    inv_freq = ROPE_BASE ** (-jnp.arange(half, dtype=F32) / half)
    ang = jnp.arange(seq, dtype=jnp.int32).astype(F32)[:, None] * inv_freq[None, :]
    cos, sin = jnp.cos(ang), jnp.sin(ang)
    return jnp.concatenate([cos, cos], axis=1), jnp.concatenate([-sin, sin], axis=1)


def _col_scale(sizes_scales):
    return jnp.concatenate([jnp.full((size,), scale, F32) for size, scale in sizes_scales])


def kernel(x, ln_in_g, ln_in_b, w_in, b_forget, w_branch_fox, w_branch_ret, w_out, ln1_g, ln1_b,
           w_router_group, b_router_group, w_router_expert, b_router_expert, w_gate, w_up, w_down, ln2_g, ln2_b):
    batch, seq, d = x.shape
    n = batch * seq
    depth = w_in.shape[0]
    alpha = (2.0 * depth) ** 0.25
    fox_w = FOX_HEADS * FOX_HEAD_DIM
    ret_qk = RET_HEADS * RET_QK_DIM
    ret_v = RET_HEADS * RET_V_DIM
    d_expert = w_gate.shape[-1]
    o_ff = 3 * fox_w
    o_rq = o_ff + FOX_HEADS
    o_rv = o_rq + 2 * ret_qk
    o_rg = o_rv + ret_v
    o_gate = o_rg + ret_v
    tables = _rotary_tables(seq)
    fox_scale = _col_scale([(fox_w, FOX_HEAD_DIM ** -0.5 * LOG2E), (2 * fox_w, 1.0)])
    rqk_scale = _col_scale([(ret_qk, 1.0), (ret_qk, RET_QK_DIM ** -0.5)])
    ones_v = jnp.ones((ret_v,), F32)
    ones_g = jnp.ones((2 * d,), F32)
    n_tiles = (N_EXPERTS * (EXPERT_TILE - 1) + 2 * n) // EXPERT_TILE + 1
    n_slots = n_tiles * EXPERT_TILE

    h32, h16 = _layer_norm(x.reshape(n, d), ln_in_g, ln_in_b)
    for l in range(depth):
        w = w_in[l]
        qkv = _proj(h16, w[:, :o_ff].astype(BF16), fox_scale, "none", seq)
        aug = _forget_cumsum(h16, w[:, o_ff:o_rq], b_forget[l], batch, seq)
        rqk = _proj(h16, w[:, o_rq:o_rv].astype(BF16), rqk_scale, "rotary", seq, tables)
        rv = _proj(h16, w[:, o_rv:o_rg].astype(BF16), ones_v, "none", seq)
        rg = _proj(h16, w[:, o_rg:o_gate].astype(BF16), ones_v, "silu", seq)
        gates = _proj(h16, w[:, o_gate:].astype(BF16), ones_g, "sigmoid", seq)
        y_fox = _fox_attention(qkv, aug, batch, seq)
        y_ret = _retention(rqk, rv, rg, batch, seq)
        merged = _merge(y_fox, y_ret, w_branch_fox[l].astype(BF16), w_branch_ret[l].astype(BF16), gates)

        w_router = jnp.zeros((d, LANES), F32)
        w_router = w_router.at[:, :N_GROUPS].set(w_router_group[l])
        w_router = w_router.at[:, EXPERT_LANE0:EXPERT_LANE0 + N_EXPERTS].set(w_router_expert[l])
        b_router = jnp.zeros((1, LANES), F32)
        b_router = b_router.at[0, :N_GROUPS].set(b_router_group[l])
        b_router = b_router.at[0, EXPERT_LANE0:EXPERT_LANE0 + N_EXPERTS].set(b_router_expert[l])
        w_router_hi = lax.reduce_precision(w_router, exponent_bits=8, mantissa_bits=7)
        w_router_lo = w_router - w_router_hi
        h1, h1_packed, route, wcol, counts = _outproj_router(
            merged, w_out[l].astype(BF16), h32, ln1_g[l], ln1_b[l],
            jnp.concatenate([w_router_hi, w_router_lo], axis=1).astype(BF16), b_router, alpha)
        dest, tiles = _plan(route, counts, EXPERT_TILE, n_tiles)
        d1, d2 = dest[0], dest[1]
        tok = _invert(d1, d2, n_slots)
        y = _experts(tiles[0, :n_tiles], tiles[1, :1], tok, h1_packed,
                     w_gate.reshape(depth * N_EXPERTS, d, d_expert), w_up.reshape(depth * N_EXPERTS, d, d_expert),
                     w_down.reshape(depth * N_EXPERTS, d_expert, d), l, EXPERT_TILE)
        h32, h16 = _combine(d1, d2, h1, wcol, ln2_g[l], ln2_b[l], y, alpha)
    return h32.reshape(batch, seq, d)
```

```python
import functools

import jax
import jax.numpy as jnp
from jax import lax
from jax.experimental import pallas as pl
from jax.experimental.pallas import tpu as pltpu

F32 = jnp.float32
BF16 = jnp.bfloat16

CHUNK = 64
FOX_HEADS = 8
FOX_HEAD_DIM = 128
RET_HEADS = 8
RET_QK_DIM = 128
RET_V_DIM = 256
ROPE_BASE = 10000.0
N_GROUPS = 4
EXPERTS_PER_GROUP = 8
N_EXPERTS = N_GROUPS * EXPERTS_PER_GROUP
LOG2_EXPERTS_PER_GROUP = EXPERTS_PER_GROUP.bit_length() - 1
LN_EPS = 1e-5
RMS_EPS = 1e-6

LANES = 128
EXPERT_LANE0 = N_GROUPS
VMEM_LIMIT = 56 * 1024 * 1024

ROW_TILE = 512
MATMUL_ROWS = 1024
DMA_UNROLL = 32
LOG2E = 1.4426950408889634
ATTN_TILE = 256
RET_TILE = 256
EXPERT_TILE = 256
GATHER_TILE = 256


def _params(*sem):
    return pltpu.CompilerParams(dimension_semantics=sem, vmem_limit_bytes=VMEM_LIMIT)


def _pack_halves(x):
    w = x.shape[1] // 2
    lo = lax.bitcast_convert_type(x[:, :w].astype(BF16).astype(F32), jnp.uint32)
    hi = lax.bitcast_convert_type(x[:, w:].astype(BF16).astype(F32), jnp.uint32)
    return (lo >> 16) | hi


def _unpack_halves(u):
    lo = lax.bitcast_convert_type(u << 16, F32)
    hi = lax.bitcast_convert_type(u & jnp.uint32(0xFFFF0000), F32)
    return lo, hi


def _layer_norm_rows(x, g, b):
    mu = jnp.mean(x, axis=-1, keepdims=True)
    xc = x - mu
    var = jnp.mean(xc * xc, axis=-1, keepdims=True)
    return xc * lax.rsqrt(var + LN_EPS) * g + b


def _ln_kernel(x_ref, g_ref, b_ref, o32_ref, o16_ref):
    y = _layer_norm_rows(x_ref[...], g_ref[...], b_ref[...])
    o32_ref[...] = y
    o16_ref[...] = y.astype(BF16)


def _layer_norm(x, g, b):
    n, d = x.shape
    tm = min(ROW_TILE, n)
    row = pl.BlockSpec((tm, d), lambda i: (i, 0))
    vec = pl.BlockSpec((1, d), lambda i: (0, 0))
    return pl.pallas_call(
        _ln_kernel, grid=(n // tm,), in_specs=[row, vec, vec], out_specs=[row, row],
        out_shape=[jax.ShapeDtypeStruct((n, d), F32), jax.ShapeDtypeStruct((n, d), BF16)],
        compiler_params=_params("parallel"), name="ln_in",
    )(x, g.reshape(1, d), b.reshape(1, d))


def _proj_kernel(x_ref, w_ref, s_ref, *rest, act):
    acc = jnp.dot(x_ref[...], w_ref[...], preferred_element_type=F32) * s_ref[...]
    if act == "rotary":
        cos_ref, sin_ref, o_ref = rest
        cos, sin = cos_ref[...], sin_ref[...]
        for c in range(acc.shape[1] // LANES):
            blk = acc[:, c * LANES:(c + 1) * LANES]
            rot = blk * cos + pltpu.roll(blk, LANES // 2, 1) * sin
            o_ref[:, c * LANES:(c + 1) * LANES] = rot.astype(o_ref.dtype)
        return
    (o_ref,) = rest
    if act == "silu":
        acc = acc * jax.nn.sigmoid(acc)
    elif act == "sigmoid":
        acc = jax.nn.sigmoid(acc)
    o_ref[...] = acc.astype(o_ref.dtype)


def _proj(x16, w16, colscale, act, seq, tables=None, tn=1024):
    n, k = x16.shape
    m = w16.shape[1]
    tm = min(MATMUL_ROWS, seq)
    in_specs = [pl.BlockSpec((tm, k), lambda i, j: (i, 0)),
                pl.BlockSpec((k, tn), lambda i, j: (0, j)),
                pl.BlockSpec((1, tn), lambda i, j: (0, j))]
    args = [x16, w16, colscale.reshape(1, m)]
    if act == "rotary":
        per_seq = seq // tm
        tab = pl.BlockSpec((tm, LANES), lambda i, j: (i % per_seq, 0))
        in_specs += [tab, tab]
        args += list(tables)
    return pl.pallas_call(
        functools.partial(_proj_kernel, act=act), grid=(n // tm, m // tn),
        in_specs=in_specs, out_specs=pl.BlockSpec((tm, tn), lambda i, j: (i, j)),
        out_shape=jax.ShapeDtypeStruct((n, m), BF16),
        compiler_params=_params("parallel", "parallel"), name="proj_" + act,
    )(*args)


def _forget_kernel(h_ref, w_ref, b_ref, o_ref, *, heads, block):
    z = jnp.dot(h_ref[...], w_ref[...], preferred_element_type=F32) + b_ref[...]
    lf = (jnp.minimum(z, 0.0) - jnp.log1p(jnp.exp(-jnp.abs(z)))) * LOG2E
    seq = lf.shape[0]
    r_i = lax.broadcasted_iota(jnp.int32, (block, block), 0)
    c_i = lax.broadcasted_iota(jnp.int32, (block, block), 1)
    upto = (c_i <= r_i).astype(BF16)
    lane = lax.broadcasted_iota(jnp.int32, (block, LANES), 1)
    carry = jnp.zeros((1, LANES), F32)
    for blk in range(seq // block):
        rows = slice(blk * block, (blk + 1) * block)
        part = lf[rows]
        cum = carry
        for _ in range(N_SPLIT):
            piece = part.astype(BF16)
            cum = cum + jnp.dot(upto, piece, preferred_element_type=F32)
            part = part - piece.astype(F32)
        carry = cum[block - 1:block, :]
        for h in range(heads):
            col = jnp.broadcast_to(cum[:, h:h + 1], (block, LANES))
            hi = col.astype(BF16).astype(F32)
            mid = (col - hi).astype(BF16).astype(F32)
            lo = col - hi - mid
            split = jnp.where(lane == 0, hi, jnp.where(lane == 1, mid, jnp.where(lane == 2, lo, 0.0)))
            o_ref[rows, h * LANES:(h + 1) * LANES] = split.astype(BF16)


def _forget_cumsum(h16, w_forget, b_forget, batch, seq):
    d, heads = w_forget.shape
    w_pad = jnp.zeros((d, LANES), BF16).at[:, :heads].set(w_forget.astype(BF16))
    b_pad = jnp.zeros((1, LANES), F32).at[0, :heads].set(b_forget.astype(F32))
    return pl.pallas_call(
        functools.partial(_forget_kernel, heads=heads, block=min(256, seq)), grid=(batch,),
        in_specs=[pl.BlockSpec((seq, d), lambda b: (b, 0)),
                  pl.BlockSpec((d, LANES), lambda b: (0, 0)),
                  pl.BlockSpec((1, LANES), lambda b: (0, 0))],
        out_specs=pl.BlockSpec((seq, heads * LANES), lambda b: (b, 0)),
        out_shape=jax.ShapeDtypeStruct((batch * seq, heads * LANES), BF16),
        compiler_params=_params("parallel"), name="forget_cumsum",
    )(h16, w_pad, b_pad)


N_SPLIT = 3


def _fox_kernel(q_ref, k_ref, v_ref, a_ref, o_ref, acc_sc, kaug_sc, *, tile, hp):
    qi = pl.program_id(2)
    dh = FOX_HEAD_DIM

    @pl.when(qi == 0)
    def _():
        for hh in range(hp):
            kaug_sc[:, 2 * hh * dh:(2 * hh + 1) * dh] = k_ref[:, hh * dh:(hh + 1) * dh]
            kaug_sc[:, (2 * hh + 1) * dh:(2 * hh + 2) * dh] = a_ref[:, hh * dh:(hh + 1) * dh]

    lane = lax.broadcasted_iota(jnp.int32, (tile, dh), 1)
    q_extra = jnp.where(lane < N_SPLIT, -1.0, 0.0).astype(BF16)
    q_aug = [jnp.concatenate([q_ref[:, hh * dh:(hh + 1) * dh], q_extra], axis=1) for hh in range(hp)]
    acc_sc[...] = jnp.zeros(acc_sc.shape, F32)

    def block(j, stats, diagonal):
        rows = pl.ds(pl.multiple_of(j * tile, tile), tile)
        s_t = []
        for hh in range(hp):
            s = lax.dot_general(kaug_sc[rows, 2 * hh * dh:(2 * hh + 2) * dh], q_aug[hh],
                                (((1,), (1,)), ((), ())), preferred_element_type=F32)
            if diagonal:
                key = lax.broadcasted_iota(jnp.int32, s.shape, 0)
                qry = lax.broadcasted_iota(jnp.int32, s.shape, 1)
                s = jnp.where(key <= qry, s, -jnp.inf)
            s_t.append(s)
        out, scale, pv = [], [], []
        for hh in range(hp):
            m_prev, l_prev = stats[hh]
            m_new = jnp.maximum(m_prev, jnp.max(s_t[hh], axis=0, keepdims=True))
            a = jnp.exp2(m_prev - m_new)
            p = jnp.exp2(s_t[hh] - m_new)
            out.append((m_new, a * l_prev + jnp.sum(p, axis=0, keepdims=True)))
            scale.append(a)
            pv.append(lax.dot_general(v_ref[rows, hh * dh:(hh + 1) * dh], p.astype(BF16),
                                      (((0,), (0,)), ((), ())), preferred_element_type=F32))
        for hh in range(hp):
            acc_sc[hh] = scale[hh] * acc_sc[hh] + pv[hh]
        return tuple(out)

    init = tuple((jnp.full((1, tile), -jnp.inf, F32), jnp.zeros((1, tile), F32)) for _ in range(hp))
    stats = lax.fori_loop(0, qi, lambda j, st: block(j, st, False), init)
    stats = block(qi, stats, True)
    for hh in range(hp):
        o_ref[:, hh * dh:(hh + 1) * dh] = (acc_sc[hh] / stats[hh][1]).T.astype(o_ref.dtype)


def _fox_attention(qkv16, aug16, batch, seq, hp=8):
    heads, dh = FOX_HEADS, FOX_HEAD_DIM
    tile = min(ATTN_TILE, seq)
    nq = seq // tile
    groups = heads // hp
    w = hp * dh
    return pl.pallas_call(
        functools.partial(_fox_kernel, tile=tile, hp=hp), grid=(batch, groups, nq),
        in_specs=[pl.BlockSpec((tile, w), lambda b, g, i: (b * nq + i, g)),
                  pl.BlockSpec((seq, w), lambda b, g, i: (b, groups + g)),
                  pl.BlockSpec((seq, w), lambda b, g, i: (b, 2 * groups + g)),
                  pl.BlockSpec((seq, w), lambda b, g, i: (b, g))],
        out_specs=pl.BlockSpec((tile, w), lambda b, g, i: (b * nq + i, g)),
        out_shape=jax.ShapeDtypeStruct((batch * seq, heads * dh), BF16),
        scratch_shapes=[pltpu.VMEM((hp, dh, tile), F32), pltpu.VMEM((seq, 2 * w), BF16)],
        compiler_params=_params("parallel", "parallel", "arbitrary"), name="fox_attention",
    )(qkv16, qkv16, qkv16, aug16)


def _ret_kernel(q_ref, k_ref, v_ref, g_ref, dm_ref, qd_ref, kd_ref, cd_ref, o_ref, st_sc, *, hp):
    @pl.when(pl.program_id(2) == 0)
    def _():
        st_sc[...] = jnp.zeros(st_sc.shape, F32)

    dk, dv = RET_QK_DIM, RET_V_DIM
    for hh in range(hp):
        q = q_ref[:, hh * dk:(hh + 1) * dk]
        k = k_ref[:, hh * dk:(hh + 1) * dk]
        v = v_ref[:, hh * dv:(hh + 1) * dv]
        scores = lax.dot_general(q, k, (((1,), (1,)), ((), ())), preferred_element_type=F32) * dm_ref[hh]
        intra = jnp.dot(scores.astype(BF16), v, preferred_element_type=F32)
        state = st_sc[hh]
        cross = jnp.dot(q, state.astype(BF16), preferred_element_type=F32) * qd_ref[hh]
        k_dec = (k.astype(F32) * kd_ref[hh]).astype(BF16)
        st_sc[hh] = state * cd_ref[hh] + lax.dot_general(
            k_dec, v, (((0,), (0,)), ((), ())), preferred_element_type=F32)
        o = intra + cross
        o = o * lax.rsqrt(jnp.mean(o * o, axis=-1, keepdims=True) + RMS_EPS)
        o_ref[:, hh * dv:(hh + 1) * dv] = (g_ref[:, hh * dv:(hh + 1) * dv].astype(F32) * o).astype(o_ref.dtype)


def _retention_tables(tile):
    heads = RET_HEADS
    log_gamma = jnp.log1p(-(2.0 ** (-5.0 - jnp.arange(heads, dtype=F32))))
    pos = jnp.arange(tile, dtype=F32)
    chunk = jnp.arange(tile, dtype=jnp.int32) // CHUNK
    dist = pos[:, None] - pos[None, :]
    same = chunk[:, None] == chunk[None, :]
    earlier = chunk[None, :] < chunk[:, None]
    expo = jnp.where(same, jnp.abs(dist), dist)
    dmask = jnp.where((same | earlier)[None], jnp.exp(log_gamma[:, None, None] * expo[None]), 0.0)
    q_dec = jnp.broadcast_to(jnp.exp(log_gamma[:, None] * (pos + 1.0))[:, :, None], (heads, tile, RET_V_DIM))
    k_dec = jnp.broadcast_to(jnp.exp(log_gamma[:, None] * (tile - 1 - pos))[:, :, None], (heads, tile, RET_QK_DIM))
    c_dec = jnp.broadcast_to(jnp.exp(log_gamma * tile)[:, None, None], (heads, 1, RET_V_DIM))
    return dmask, q_dec, k_dec, c_dec


def _retention(rqk16, rv16, rg16, batch, seq, hp=4):
    heads, dk, dv = RET_HEADS, RET_QK_DIM, RET_V_DIM
    tile = min(RET_TILE, seq)
    nt = seq // tile
    groups = heads // hp
    dmask, q_dec, k_dec, c_dec = _retention_tables(tile)
    val = pl.BlockSpec((tile, hp * dv), lambda b, g, t: (b * nt + t, g))
    return pl.pallas_call(
        functools.partial(_ret_kernel, hp=hp), grid=(batch, groups, nt),
        in_specs=[pl.BlockSpec((tile, hp * dk), lambda b, g, t: (b * nt + t, g)),
                  pl.BlockSpec((tile, hp * dk), lambda b, g, t: (b * nt + t, groups + g)),
                  val, val,
                  pl.BlockSpec((hp, tile, tile), lambda b, g, t: (g, 0, 0)),
                  pl.BlockSpec((hp, tile, dv), lambda b, g, t: (g, 0, 0)),
                  pl.BlockSpec((hp, tile, dk), lambda b, g, t: (g, 0, 0)),
                  pl.BlockSpec((hp, 1, dv), lambda b, g, t: (g, 0, 0))],
        out_specs=val,
        out_shape=jax.ShapeDtypeStruct((batch * seq, heads * dv), BF16),
        scratch_shapes=[pltpu.VMEM((hp, dk, dv), F32)],
        compiler_params=_params("parallel", "parallel", "arbitrary"), name="retention",
    )(rqk16, rqk16, rv16, rg16, dmask, q_dec, k_dec, c_dec)


def _merge_kernel(yf_ref, yr_ref, wf_ref, wr_ref, gf_ref, gr_ref, o_ref):
    fox = jnp.dot(yf_ref[...], wf_ref[...], preferred_element_type=F32)
    ret = jnp.dot(yr_ref[...], wr_ref[...], preferred_element_type=F32)
    o_ref[...] = (gf_ref[...].astype(F32) * fox + gr_ref[...].astype(F32) * ret).astype(o_ref.dtype)


def _merge(y_fox, y_ret, w_fox16, w_ret16, gates16, tn=512):
    n, kf = y_fox.shape
    kr = y_ret.shape[1]
    d = w_fox16.shape[1]
    tm = min(MATMUL_ROWS, n)
    nj = d // tn
    return pl.pallas_call(
        _merge_kernel, grid=(n // tm, nj),
        in_specs=[pl.BlockSpec((tm, kf), lambda i, j: (i, 0)),
                  pl.BlockSpec((tm, kr), lambda i, j: (i, 0)),
                  pl.BlockSpec((kf, tn), lambda i, j: (0, j)),
                  pl.BlockSpec((kr, tn), lambda i, j: (0, j)),
                  pl.BlockSpec((tm, tn), lambda i, j: (i, j)),
                  pl.BlockSpec((tm, tn), lambda i, j: (i, nj + j))],
        out_specs=pl.BlockSpec((tm, tn), lambda i, j: (i, j)),
        out_shape=jax.ShapeDtypeStruct((n, d), BF16),
        compiler_params=_params("parallel", "parallel"), name="merge",
    )(y_fox, y_ret, w_fox16, w_ret16, gates16, gates16)


def _lane_pick(mask, values):
    return jnp.sum(jnp.where(mask, values, 0.0), axis=1, keepdims=True)


def _outproj_router_kernel(m_ref, w_ref, h_ref, g_ref, b_ref, wr_ref, br_ref,
                           h1_ref, h1p_ref, route_ref, wcol_ref, cnt_ref, carry_sc, *, alpha):
    i = pl.program_id(0)

    @pl.when(i == 0)
    def _():
        carry_sc[...] = jnp.zeros(carry_sc.shape, F32)

    mix = jnp.dot(m_ref[...], w_ref[...], preferred_element_type=F32)
    hn = _layer_norm_rows(alpha * h_ref[...] + mix, g_ref[...], b_ref[...])
    h1_ref[...] = hn
    h1p_ref[...] = _pack_halves(hn)

    hn_hi = hn.astype(BF16)
    hn_lo = (hn - hn_hi.astype(F32)).astype(BF16)
    part = jnp.dot(hn_hi, wr_ref[...], preferred_element_type=F32)
    logits = (part[:, :LANES] + part[:, LANES:] + br_ref[...]
              + jnp.dot(hn_lo, wr_ref[:, :LANES], preferred_element_type=F32))
    tm = logits.shape[0]
    lane = lax.broadcasted_iota(jnp.int32, logits.shape, 1)
    neg = -jnp.inf
    gl = jnp.where(lane < N_GROUPS, logits, neg)
    gmax = jnp.max(gl, axis=1, keepdims=True)
    grp_w = 1.0 / jnp.sum(jnp.exp(gl - gmax), axis=1, keepdims=True)
    gidx = jnp.min(jnp.where(gl == gmax, lane, LANES), axis=1, keepdims=True)
    e_lane = lane - EXPERT_LANE0
    in_grp = (e_lane >= 0) & (e_lane < N_EXPERTS) & ((e_lane >> LOG2_EXPERTS_PER_GROUP) == gidx)
    el = jnp.where(in_grp, logits, neg)
    max1 = jnp.max(el, axis=1, keepdims=True)
    i1 = jnp.min(jnp.where(el == max1, lane, LANES), axis=1, keepdims=True)
    el2 = jnp.where(lane == i1, neg, el)
    max2 = jnp.max(el2, axis=1, keepdims=True)
    i2 = jnp.min(jnp.where(el2 == max2, lane, LANES), axis=1, keepdims=True)
    ratio = jnp.exp(max2 - max1)
    w1 = grp_w / (1.0 + ratio)
    w2 = grp_w * ratio / (1.0 + ratio)
    sel1 = lane == i1
    sel2 = lane == i2
    sel = sel1.astype(F32) + sel2.astype(F32)
    r_i = lax.broadcasted_iota(jnp.int32, (tm, tm), 0)
    c_i = lax.broadcasted_iota(jnp.int32, (tm, tm), 1)
    before = (c_i < r_i).astype(BF16)
    rank = carry_sc[...] + jnp.dot(before, sel.astype(BF16), preferred_element_type=F32)
    new_carry = carry_sc[...] + jnp.sum(sel, axis=0, keepdims=True)
    carry_sc[...] = new_carry
    cnt_ref[...] = jnp.broadcast_to(new_carry, cnt_ref.shape)
    r1 = _lane_pick(sel1, rank)
    r2 = _lane_pick(sel2, rank)
    e1 = (i1 - EXPERT_LANE0).astype(F32)
    e2 = (i2 - EXPERT_LANE0).astype(F32)
    rec = jnp.where(lane == 0, e1, jnp.where(lane == 1, e2, jnp.where(lane == 2, r1, jnp.where(lane == 3, r2, 0.0))))
    route_ref[...] = rec.T[0:8, :]
    wcol_ref[...] = jnp.where(lane == 0, w1, jnp.where(lane == 1, w2, 0.0))


def _outproj_router(merged16, w_out16, h32, g, b, w_router, b_router, alpha):
    n, d = merged16.shape
    tm = min(ROW_TILE, n)
    row16 = pl.BlockSpec((tm, d), lambda i: (i, 0))
    vec = pl.BlockSpec((1, d), lambda i: (0, 0))
    return pl.pallas_call(
        functools.partial(_outproj_router_kernel, alpha=alpha), grid=(n // tm,),
        in_specs=[row16, pl.BlockSpec((d, d), lambda i: (0, 0)), row16, vec, vec,
                  pl.BlockSpec((d, 2 * LANES), lambda i: (0, 0)), pl.BlockSpec((1, LANES), lambda i: (0, 0))],
        out_specs=[row16, pl.BlockSpec((tm, d // 2), lambda i: (i, 0)), pl.BlockSpec((8, tm), lambda i: (0, i)),
                   pl.BlockSpec((tm, LANES), lambda i: (i, 0)), pl.BlockSpec((8, LANES), lambda i: (0, 0))],
        out_shape=[jax.ShapeDtypeStruct((n, d), F32), jax.ShapeDtypeStruct((n, d // 2), jnp.uint32),
                   jax.ShapeDtypeStruct((8, n), F32),
                   jax.ShapeDtypeStruct((n, LANES), F32), jax.ShapeDtypeStruct((8, LANES), F32)],
        scratch_shapes=[pltpu.VMEM((1, LANES), F32)],
        compiler_params=_params("arbitrary"), name="outproj_router",
    )(merged16, w_out16, h32, g.reshape(1, d), b.reshape(1, d), w_router, b_router)


def _plan_kernel(route_ref, cnt_ref, dest_ref, tiles_ref, *, tile, log2_tile):
    cnt = cnt_ref[...].astype(jnp.int32)
    padded = ((cnt + (tile - 1)) >> log2_tile) << log2_tile
    lane = lax.broadcasted_iota(jnp.int32, padded.shape, 1)
    end = padded
    k = 1
    while k < LANES:
        end = end + jnp.where(lane >= k, pltpu.roll(end, k, 1), 0)
        k *= 2
    start = end - padded
    rows = EXPERT_LANE0 + N_EXPERTS
    rows = -(-rows // 8) * 8
    def lane_to_sublane(v):
        return jnp.broadcast_to(v[0:1, :].astype(F32), (LANES, LANES)).T[0:rows, 0:1]

    start_col = lane_to_sublane(start)
    end_col = lane_to_sublane(end)
    route = route_ref[...]
    n = route.shape[1]
    sub = lax.broadcasted_iota(jnp.int32, (rows, n), 0)
    e1 = route[0:1, :].astype(jnp.int32) + EXPERT_LANE0
    e2 = route[1:2, :].astype(jnp.int32) + EXPERT_LANE0
    d1 = jnp.sum(jnp.where(sub == e1, start_col, 0.0), axis=0, keepdims=True) + route[2:3, :]
    d2 = jnp.sum(jnp.where(sub == e2, start_col, 0.0), axis=0, keepdims=True) + route[3:4, :]
    row8 = lax.broadcasted_iota(jnp.int32, dest_ref.shape, 0)
    dest_ref[...] = jnp.where(row8 == 0, d1, jnp.where(row8 == 1, d2, 0.0)).astype(jnp.int32)
    nt = tiles_ref.shape[1]
    t_start = (lax.broadcasted_iota(jnp.int32, (rows, nt), 1) * tile).astype(F32)
    sub_t = lax.broadcasted_iota(jnp.int32, (rows, nt), 0)
    is_expert = (sub_t >= EXPERT_LANE0) & (sub_t < EXPERT_LANE0 + N_EXPERTS)
    done = jnp.sum(jnp.where(is_expert & (end_col <= t_start), 1.0, 0.0), axis=0, keepdims=True)
    expert = jnp.minimum(done, float(N_EXPERTS - 1))
    used = jnp.max(end_col, axis=0, keepdims=True) * (1.0 / tile)
    row8t = lax.broadcasted_iota(jnp.int32, tiles_ref.shape, 0)
    tiles_ref[...] = jnp.where(row8t == 0, expert, jnp.broadcast_to(used, tiles_ref.shape)).astype(jnp.int32)


def _plan(route, counts, tile, n_tiles):
    n = route.shape[1]
    nt_pad = -(-n_tiles // LANES) * LANES
    return pl.pallas_call(
        functools.partial(_plan_kernel, tile=tile, log2_tile=tile.bit_length() - 1),
        out_shape=[jax.ShapeDtypeStruct((8, n), jnp.int32), jax.ShapeDtypeStruct((8, nt_pad), jnp.int32)],
        compiler_params=pltpu.CompilerParams(vmem_limit_bytes=VMEM_LIMIT), name="plan",
    )(route, counts)


def _start_rows(copies, n_rows):
    def start(o, c):
        for u in range(DMA_UNROLL):
            for cp in copies(o * DMA_UNROLL + u):
                cp.start()
        return c

    lax.fori_loop(0, n_rows // DMA_UNROLL, start, 0)


def _wait_rows(copies, n_rows):
    def wait(o, c):
        for u in range(DMA_UNROLL):
            for cp in copies(o * DMA_UNROLL + u):
                cp.wait()
        return c

    lax.fori_loop(0, n_rows // DMA_UNROLL, wait, 0)


def _invert_kernel(d1_ref, d2_ref, tok_ref, *, unroll):
    def clear(o, c):
        for u in range(unroll):
            tok_ref[o * unroll + u] = 0
        return c

    def put(o, c):
        for u in range(unroll):
            n = o * unroll + u
            tok_ref[d1_ref[n]] = n
            tok_ref[d2_ref[n]] = n
        return c

    lax.fori_loop(0, tok_ref.shape[0] // unroll, clear, 0)
    lax.fori_loop(0, d1_ref.shape[0] // unroll, put, 0)


def _invert(d1, d2, n_slots):
    return pl.pallas_call(
        functools.partial(_invert_kernel, unroll=DMA_UNROLL),
        grid_spec=pltpu.PrefetchScalarGridSpec(
            num_scalar_prefetch=2, grid=(),
            in_specs=[], out_specs=pl.BlockSpec(memory_space=pltpu.SMEM)),
        out_shape=jax.ShapeDtypeStruct((n_slots,), jnp.int32), name="invert",
    )(d1, d2)


def _expert_kernel(te_ref, nu_ref, tok_ref, h_hbm, wg_ref, wu_ref, wd_ref, y_ref,
                   xbuf0, xbuf1, wg_sc, wu_sc, wd_sc, hid_sc, sem, *, tile):
    t = pl.program_id(0)
    n_used = nu_ref[0]
    bufs = (xbuf0, xbuf1)

    def row_copy(tt, slot, r):
        src = h_hbm.at[pl.ds(tok_ref[tt * tile + r], 1), :]
        return pltpu.make_async_copy(src, bufs[slot].at[pl.ds(r, 1), :], sem.at[slot])

    @pl.when(t == 0)
    def _():
        _start_rows(lambda r: (row_copy(t, 0, r),), tile)

    nxt = jnp.minimum(t + 1, n_used - 1)

    def tile_step(slot):
        _wait_rows(lambda r: (row_copy(t, slot, r),), tile)

        @pl.when((t == 0) | (te_ref[t] != te_ref[jnp.maximum(t - 1, 0)]))
        def _():
            wg_sc[...] = wg_ref[0].astype(BF16)
            wu_sc[...] = wu_ref[0].astype(BF16)
            wd_sc[...] = wd_ref[0].astype(BF16)

        once = jnp.minimum(n_used, 1)
        split = (tile * 5) // 8

        def up_phase(_, c):
            for r in range(split):
                row_copy(nxt, 1 - slot, r).start()
            x_lo, x_hi = _unpack_halves(bufs[slot][...])
            x = jnp.concatenate([x_lo.astype(BF16), x_hi.astype(BF16)], axis=1)
            gate = jnp.dot(x, wg_sc[...], preferred_element_type=F32)
            up = jnp.dot(x, wu_sc[...], preferred_element_type=F32)
            hid_sc[...] = (gate * jax.nn.sigmoid(gate) * up).astype(BF16)
            return c

        def down_phase(_, c):
            for r in range(split, tile):
                row_copy(nxt, 1 - slot, r).start()
            y_ref[...] = _pack_halves(jnp.dot(hid_sc[...], wd_sc[...], preferred_element_type=F32))
            return c

        lax.fori_loop(0, once, up_phase, 0)
        lax.fori_loop(0, once, down_phase, 0)

        @pl.when(t == n_used - 1)
        def _():
            _wait_rows(lambda r: (row_copy(nxt, 1 - slot, r),), tile)

    for slot in range(2):
        pl.when((t < n_used) & (t % 2 == slot))(functools.partial(tile_step, slot))

    @pl.when(t >= n_used)
    def _():
        y_ref[...] = jnp.zeros(y_ref.shape, y_ref.dtype)


def _experts(tile_expert, n_used, tok, h_packed, w_gate, w_up, w_down, layer, tile):
    dp = h_packed.shape[1]
    d, f = w_gate.shape[1], w_gate.shape[2]
    p = tok.shape[0]
    first = layer * N_EXPERTS

    def weights(shape):
        return pl.BlockSpec((1,) + shape, lambda t, te, nu, tk: (first + te[t], 0, 0))

    return pl.pallas_call(
        functools.partial(_expert_kernel, tile=tile),
        grid_spec=pltpu.PrefetchScalarGridSpec(
            num_scalar_prefetch=3, grid=(p // tile,),
            in_specs=[pl.BlockSpec(memory_space=pl.ANY), weights((d, f)), weights((d, f)), weights((f, d))],
            out_specs=pl.BlockSpec((tile, dp), lambda t, te, nu, tk: (t, 0)),
            scratch_shapes=[pltpu.VMEM((tile, dp), jnp.uint32), pltpu.VMEM((tile, dp), jnp.uint32),
                            pltpu.VMEM((d, f), BF16), pltpu.VMEM((d, f), BF16),
                            pltpu.VMEM((f, d), BF16), pltpu.VMEM((tile, f), BF16),
                            pltpu.SemaphoreType.DMA((2,))]),
        out_shape=jax.ShapeDtypeStruct((p, dp), jnp.uint32),
        compiler_params=_params("arbitrary"), name="experts",
    )(tile_expert, n_used, tok, h_packed, w_gate, w_up, w_down)


def _combine_kernel(d1_ref, d2_ref, h_ref, wc_ref, g_ref, b_ref, y_hbm, o32_ref, o16_ref,
                    a0, b0, a1, b1, sem, *, tile, alpha, n_steps):
    i = pl.program_id(0)
    bufs = ((a0, b0), (a1, b1))

    def copies(ii, slot, r):
        n = ii * tile + r
        first, second = bufs[slot]
        return (pltpu.make_async_copy(y_hbm.at[pl.ds(d1_ref[n], 1), :], first.at[pl.ds(r, 1), :], sem.at[slot]),
                pltpu.make_async_copy(y_hbm.at[pl.ds(d2_ref[n], 1), :], second.at[pl.ds(r, 1), :], sem.at[slot]))

    @pl.when(i == 0)
    def _():
        _start_rows(lambda r: copies(i, 0, r), tile)

    nxt = jnp.minimum(i + 1, n_steps - 1)

    def tile_step(slot):
        _wait_rows(lambda r: copies(i, slot, r), tile)
        for r in range(tile):
            for cp in copies(nxt, 1 - slot, r):
                cp.start()
        first, second = bufs[slot]
        wc = wc_ref[...]
        w1, w2 = wc[:, 0:1], wc[:, 1:2]
        f_lo, f_hi = _unpack_halves(first[...])
        s_lo, s_hi = _unpack_halves(second[...])
        ffn = jnp.concatenate([w1 * f_lo + w2 * s_lo, w1 * f_hi + w2 * s_hi], axis=1)
        y = _layer_norm_rows(alpha * h_ref[...] + ffn, g_ref[...], b_ref[...])
        o32_ref[...] = y
        o16_ref[...] = y.astype(BF16)

        @pl.when(i == n_steps - 1)
        def _():
            _wait_rows(lambda r: copies(nxt, 1 - slot, r), tile)

    for slot in range(2):
        pl.when(i % 2 == slot)(functools.partial(tile_step, slot))


def _combine(d1, d2, h32, wcol, g, b, y, alpha):
    n, d = h32.shape
    tile = min(GATHER_TILE, n)
    row = pl.BlockSpec((tile, d), lambda i, a, c: (i, 0))
    vec = pl.BlockSpec((1, d), lambda i, a, c: (0, 0))
    return pl.pallas_call(
        functools.partial(_combine_kernel, tile=tile, alpha=alpha, n_steps=n // tile),
        grid_spec=pltpu.PrefetchScalarGridSpec(
            num_scalar_prefetch=2, grid=(n // tile,),
            in_specs=[row, pl.BlockSpec((tile, LANES), lambda i, a, c: (i, 0)), vec, vec,
                      pl.BlockSpec(memory_space=pl.ANY)],
            out_specs=[row, row],
            scratch_shapes=[pltpu.VMEM((tile, y.shape[1]), y.dtype)] * 4 + [pltpu.SemaphoreType.DMA((2,))]),
        out_shape=[jax.ShapeDtypeStruct((n, d), F32), jax.ShapeDtypeStruct((n, d), BF16)],
        compiler_params=_params("arbitrary"), name="combine_ln",
    )(d1, d2, h32, wcol, g.reshape(1, d), b.reshape(1, d), y)


def _rotary_tables(seq):
    half = RET_QK_DIM // 2
    inv_freq = ROPE_BASE ** (-jnp.arange(half, dtype=F32) / half)
    ang = jnp.arange(seq, dtype=jnp.int32).astype(F32)[:, None] * inv_freq[None, :]
    cos, sin = jnp.cos(ang), jnp.sin(ang)
    return jnp.concatenate([cos, cos], axis=1), jnp.concatenate([-sin, sin], axis=1)


def _col_scale(sizes_scales):
    return jnp.concatenate([jnp.full((size,), scale, F32) for size, scale in sizes_scales])


def kernel(x, ln_in_g, ln_in_b, w_in, b_forget, w_branch_fox, w_branch_ret, w_out, ln1_g, ln1_b,
           w_router_group, b_router_group, w_router_expert, b_router_expert, w_gate, w_up, w_down, ln2_g, ln2_b):
    batch, seq, d = x.shape
    n = batch * seq
    depth = w_in.shape[0]
    alpha = (2.0 * depth) ** 0.25
    fox_w = FOX_HEADS * FOX_HEAD_DIM
    ret_qk = RET_HEADS * RET_QK_DIM
    ret_v = RET_HEADS * RET_V_DIM
    d_expert = w_gate.shape[-1]
    o_ff = 3 * fox_w
    o_rq = o_ff + FOX_HEADS
    o_rv = o_rq + 2 * ret_qk
    o_rg = o_rv + ret_v
    o_gate = o_rg + ret_v
    tables = _rotary_tables(seq)
    fox_scale = _col_scale([(fox_w, FOX_HEAD_DIM ** -0.5 * LOG2E), (2 * fox_w, 1.0)])
    rqk_scale = _col_scale([(ret_qk, 1.0), (ret_qk, RET_QK_DIM ** -0.5)])
    ones_v = jnp.ones((ret_v,), F32)
    ones_g = jnp.ones((2 * d,), F32)
    n_tiles = (N_EXPERTS * (EXPERT_TILE - 1) + 2 * n) // EXPERT_TILE + 1
    n_slots = n_tiles * EXPERT_TILE

    h32, h16 = _layer_norm(x.reshape(n, d), ln_in_g, ln_in_b)
    for l in range(depth):
        w = w_in[l]
        qkv = _proj(h16, w[:, :o_ff].astype(BF16), fox_scale, "none", seq)
        aug = _forget_cumsum(h16, w[:, o_ff:o_rq], b_forget[l], batch, seq)
        rqk = _proj(h16, w[:, o_rq:o_rv].astype(BF16), rqk_scale, "rotary", seq, tables)
        rv = _proj(h16, w[:, o_rv:o_rg].astype(BF16), ones_v, "none", seq)
        rg = _proj(h16, w[:, o_rg:o_gate].astype(BF16), ones_v, "silu", seq)
        gates = _proj(h16, w[:, o_gate:].astype(BF16), ones_g, "sigmoid", seq)
        y_fox = _fox_attention(qkv, aug, batch, seq)
        y_ret = _retention(rqk, rv, rg, batch, seq)
        merged = _merge(y_fox, y_ret, w_branch_fox[l].astype(BF16), w_branch_ret[l].astype(BF16), gates)

        w_router = jnp.zeros((d, LANES), F32)
        w_router = w_router.at[:, :N_GROUPS].set(w_router_group[l])
        w_router = w_router.at[:, EXPERT_LANE0:EXPERT_LANE0 + N_EXPERTS].set(w_router_expert[l])
        b_router = jnp.zeros((1, LANES), F32)
        b_router = b_router.at[0, :N_GROUPS].set(b_router_group[l])
        b_router = b_router.at[0, EXPERT_LANE0:EXPERT_LANE0 + N_EXPERTS].set(b_router_expert[l])
        w_router_hi = lax.reduce_precision(w_router, exponent_bits=8, mantissa_bits=7)
        w_router_lo = w_router - w_router_hi
        h1, h1_packed, route, wcol, counts = _outproj_router(
            merged, w_out[l].astype(BF16), h32, ln1_g[l], ln1_b[l],
            jnp.concatenate([w_router_hi, w_router_lo], axis=1).astype(BF16), b_router, alpha)
        dest, tiles = _plan(route, counts, EXPERT_TILE, n_tiles)
        d1, d2 = dest[0], dest[1]
        tok = _invert(d1, d2, n_slots)
        y = _experts(tiles[0, :n_tiles], tiles[1, :1], tok, h1_packed,
                     w_gate.reshape(depth * N_EXPERTS, d, d_expert), w_up.reshape(depth * N_EXPERTS, d, d_expert),
                     w_down.reshape(depth * N_EXPERTS, d_expert, d), l, EXPERT_TILE)
        h32, h16 = _combine(d1, d2, h1, wcol, ln2_g[l], ln2_b[l], y, alpha)
    return h32.reshape(batch, seq, d)
```

```python
import functools

import jax
import jax.numpy as jnp
from jax import lax
from jax.experimental import pallas as pl
from jax.experimental.pallas import tpu as pltpu

F32 = jnp.float32
BF16 = jnp.bfloat16

CHUNK = 64
FOX_HEADS = 8
FOX_HEAD_DIM = 128
RET_HEADS = 8
RET_QK_DIM = 128
RET_V_DIM = 256
ROPE_BASE = 10000.0
N_GROUPS = 4
EXPERTS_PER_GROUP = 8
N_EXPERTS = N_GROUPS * EXPERTS_PER_GROUP
LOG2_EXPERTS_PER_GROUP = EXPERTS_PER_GROUP.bit_length() - 1
LN_EPS = 1e-5
RMS_EPS = 1e-6

LANES = 128
EXPERT_LANE0 = N_GROUPS
VMEM_LIMIT = 56 * 1024 * 1024

ROW_TILE = 512
MATMUL_ROWS = 2048
DMA_UNROLL = 32
N_DMA_THREADS = 2
GATHER_THREAD = 1
LOG2E = 1.4426950408889634
ATTN_TILE = 256
RET_TILE = 256
EXPERT_TILE = 256
GATHER_TILE = 256


def _params(*sem):
    return pltpu.CompilerParams(dimension_semantics=sem, vmem_limit_bytes=VMEM_LIMIT)


def _pack_halves(x):
    w = x.shape[1] // 2
    lo = lax.bitcast_convert_type(x[:, :w].astype(BF16).astype(F32), jnp.uint32)
    hi = lax.bitcast_convert_type(x[:, w:].astype(BF16).astype(F32), jnp.uint32)
    return (lo >> 16) | hi


def _unpack_halves(u):
    lo = lax.bitcast_convert_type(u << 16, F32)
    hi = lax.bitcast_convert_type(u & jnp.uint32(0xFFFF0000), F32)
    return lo, hi


def _layer_norm_rows(x, g, b):
    mu = jnp.mean(x, axis=-1, keepdims=True)
    xc = x - mu
    var = jnp.mean(xc * xc, axis=-1, keepdims=True)
    return xc * lax.rsqrt(var + LN_EPS) * g + b


def _ln_kernel(x_ref, g_ref, b_ref, o32_ref, o16_ref):
    y = _layer_norm_rows(x_ref[...], g_ref[...], b_ref[...])
    o32_ref[...] = y
    o16_ref[...] = y.astype(BF16)


def _layer_norm(x, g, b):
    n, d = x.shape
    tm = min(ROW_TILE, n)
    row = pl.BlockSpec((tm, d), lambda i: (i, 0))
    vec = pl.BlockSpec((1, d), lambda i: (0, 0))
    return pl.pallas_call(
        _ln_kernel, grid=(n // tm,), in_specs=[row, vec, vec], out_specs=[row, row],
        out_shape=[jax.ShapeDtypeStruct((n, d), F32), jax.ShapeDtypeStruct((n, d), BF16)],
        compiler_params=_params("parallel"), name="ln_in",
    )(x, g.reshape(1, d), b.reshape(1, d))


def _proj_kernel(x_ref, w_ref, s_ref, *rest, act):
    acc = jnp.dot(x_ref[...], w_ref[...], preferred_element_type=F32) * s_ref[...]
    if act == "rotary":
        cos_ref, sin_ref, o_ref = rest
        cos, sin = cos_ref[...], sin_ref[...]
        for c in range(acc.shape[1] // LANES):
            blk = acc[:, c * LANES:(c + 1) * LANES]
            rot = blk * cos + pltpu.roll(blk, LANES // 2, 1) * sin
            o_ref[:, c * LANES:(c + 1) * LANES] = rot.astype(o_ref.dtype)
        return
    (o_ref,) = rest
    if act == "silu":
        acc = acc * jax.nn.sigmoid(acc)
    elif act == "sigmoid":
        acc = jax.nn.sigmoid(acc)
    o_ref[...] = acc.astype(o_ref.dtype)


def _proj(x16, w16, colscale, act, seq, tables=None, tn=1024):
    n, k = x16.shape
    m = w16.shape[1]
    tm = min(MATMUL_ROWS, seq)
    in_specs = [pl.BlockSpec((tm, k), lambda i, j: (i, 0)),
                pl.BlockSpec((k, tn), lambda i, j: (0, j)),
                pl.BlockSpec((1, tn), lambda i, j: (0, j))]
    args = [x16, w16, colscale.reshape(1, m)]
    if act == "rotary":
        per_seq = seq // tm
        tab = pl.BlockSpec((tm, LANES), lambda i, j: (i % per_seq, 0))
        in_specs += [tab, tab]
        args += list(tables)
    return pl.pallas_call(
        functools.partial(_proj_kernel, act=act), grid=(n // tm, m // tn),
        in_specs=in_specs, out_specs=pl.BlockSpec((tm, tn), lambda i, j: (i, j)),
        out_shape=jax.ShapeDtypeStruct((n, m), BF16),
        compiler_params=_params("parallel", "parallel"), name="proj_" + act,
    )(*args)


def _forget_kernel(h_ref, w_ref, b_ref, o_ref, *, heads, block):
    z = jnp.dot(h_ref[...], w_ref[...], preferred_element_type=F32) + b_ref[...]
    lf = (jnp.minimum(z, 0.0) - jnp.log1p(jnp.exp(-jnp.abs(z)))) * LOG2E
    seq = lf.shape[0]
    r_i = lax.broadcasted_iota(jnp.int32, (block, block), 0)
    c_i = lax.broadcasted_iota(jnp.int32, (block, block), 1)
    upto = (c_i <= r_i).astype(BF16)
    lane = lax.broadcasted_iota(jnp.int32, (block, LANES), 1)
    carry = jnp.zeros((1, LANES), F32)
    for blk in range(seq // block):
        rows = slice(blk * block, (blk + 1) * block)
        part = lf[rows]
        cum = carry
        for _ in range(N_SPLIT):
            piece = part.astype(BF16)
            cum = cum + jnp.dot(upto, piece, preferred_element_type=F32)
            part = part - piece.astype(F32)
        carry = cum[block - 1:block, :]
        for h in range(heads):
            col = jnp.broadcast_to(cum[:, h:h + 1], (block, LANES))
            hi = col.astype(BF16).astype(F32)
            mid = (col - hi).astype(BF16).astype(F32)
            lo = col - hi - mid
            split = jnp.where(lane == 0, hi, jnp.where(lane == 1, mid, jnp.where(lane == 2, lo, 0.0)))
            o_ref[rows, h * LANES:(h + 1) * LANES] = split.astype(BF16)


def _forget_cumsum(h16, w_forget, b_forget, batch, seq):
    d, heads = w_forget.shape
    w_pad = jnp.zeros((d, LANES), BF16).at[:, :heads].set(w_forget.astype(BF16))
    b_pad = jnp.zeros((1, LANES), F32).at[0, :heads].set(b_forget.astype(F32))
    return pl.pallas_call(
        functools.partial(_forget_kernel, heads=heads, block=min(256, seq)), grid=(batch,),
        in_specs=[pl.BlockSpec((seq, d), lambda b: (b, 0)),
                  pl.BlockSpec((d, LANES), lambda b: (0, 0)),
                  pl.BlockSpec((1, LANES), lambda b: (0, 0))],
        out_specs=pl.BlockSpec((seq, heads * LANES), lambda b: (b, 0)),
        out_shape=jax.ShapeDtypeStruct((batch * seq, heads * LANES), BF16),
        compiler_params=_params("parallel"), name="forget_cumsum",
    )(h16, w_pad, b_pad)


N_SPLIT = 3


def _fox_kernel(q_ref, k_ref, v_ref, a_ref, o_ref, acc_sc, kaug_sc, *, tile, hp):
    qi = pl.program_id(2)
    dh = FOX_HEAD_DIM

    @pl.when(qi == 0)
    def _():
        for hh in range(hp):
            kaug_sc[:, 2 * hh * dh:(2 * hh + 1) * dh] = k_ref[:, hh * dh:(hh + 1) * dh]
            kaug_sc[:, (2 * hh + 1) * dh:(2 * hh + 2) * dh] = a_ref[:, hh * dh:(hh + 1) * dh]

    lane = lax.broadcasted_iota(jnp.int32, (tile, dh), 1)
    q_extra = jnp.where(lane < N_SPLIT, -1.0, 0.0).astype(BF16)
    q_aug = [jnp.concatenate([q_ref[:, hh * dh:(hh + 1) * dh], q_extra], axis=1) for hh in range(hp)]
    acc_sc[...] = jnp.zeros(acc_sc.shape, F32)

    def block(j, stats, diagonal):
        rows = pl.ds(pl.multiple_of(j * tile, tile), tile)
        s_t = []
        for hh in range(hp):
            s = lax.dot_general(kaug_sc[rows, 2 * hh * dh:(2 * hh + 2) * dh], q_aug[hh],
                                (((1,), (1,)), ((), ())), preferred_element_type=F32)
            if diagonal:
                key = lax.broadcasted_iota(jnp.int32, s.shape, 0)
                qry = lax.broadcasted_iota(jnp.int32, s.shape, 1)
                s = jnp.where(key <= qry, s, -jnp.inf)
            s_t.append(s)
        out, scale, pv = [], [], []
        for hh in range(hp):
            m_prev, l_prev = stats[hh]
            m_new = jnp.maximum(m_prev, jnp.max(s_t[hh], axis=0, keepdims=True))
            a = jnp.exp2(m_prev - m_new)
            p = jnp.exp2(s_t[hh] - m_new)
            out.append((m_new, a * l_prev + jnp.sum(p, axis=0, keepdims=True)))
            scale.append(a)
            pv.append(lax.dot_general(v_ref[rows, hh * dh:(hh + 1) * dh], p.astype(BF16),
                                      (((0,), (0,)), ((), ())), preferred_element_type=F32))
        for hh in range(hp):
            acc_sc[hh] = scale[hh] * acc_sc[hh] + pv[hh]
        return tuple(out)

    init = tuple((jnp.full((1, tile), -jnp.inf, F32), jnp.zeros((1, tile), F32)) for _ in range(hp))
    stats = lax.fori_loop(0, qi, lambda j, st: block(j, st, False), init)
    stats = block(qi, stats, True)
    for hh in range(hp):
        o_ref[:, hh * dh:(hh + 1) * dh] = (acc_sc[hh] / stats[hh][1]).T.astype(o_ref.dtype)


def _fox_attention(qkv16, aug16, batch, seq, hp=8):
    heads, dh = FOX_HEADS, FOX_HEAD_DIM
    tile = min(ATTN_TILE, seq)
    nq = seq // tile
    groups = heads // hp
    w = hp * dh
    return pl.pallas_call(
        functools.partial(_fox_kernel, tile=tile, hp=hp), grid=(batch, groups, nq),
        in_specs=[pl.BlockSpec((tile, w), lambda b, g, i: (b * nq + i, g)),
                  pl.BlockSpec((seq, w), lambda b, g, i: (b, groups + g)),
                  pl.BlockSpec((seq, w), lambda b, g, i: (b, 2 * groups + g)),
                  pl.BlockSpec((seq, w), lambda b, g, i: (b, g))],
        out_specs=pl.BlockSpec((tile, w), lambda b, g, i: (b * nq + i, g)),
        out_shape=jax.ShapeDtypeStruct((batch * seq, heads * dh), BF16),
        scratch_shapes=[pltpu.VMEM((hp, dh, tile), F32), pltpu.VMEM((seq, 2 * w), BF16)],
        compiler_params=_params("parallel", "parallel", "arbitrary"), name="fox_attention",
    )(qkv16, qkv16, qkv16, aug16)


def _ret_kernel(q_ref, k_ref, v_ref, g_ref, dm_ref, qd_ref, kd_ref, cd_ref, o_ref, st_sc, *, hp):
    @pl.when(pl.program_id(2) == 0)
    def _():
        st_sc[...] = jnp.zeros(st_sc.shape, F32)

    dk, dv = RET_QK_DIM, RET_V_DIM
    for hh in range(hp):
        q = q_ref[:, hh * dk:(hh + 1) * dk]
        k = k_ref[:, hh * dk:(hh + 1) * dk]
        v = v_ref[:, hh * dv:(hh + 1) * dv]
        scores = lax.dot_general(q, k, (((1,), (1,)), ((), ())), preferred_element_type=F32) * dm_ref[hh]
        intra = jnp.dot(scores.astype(BF16), v, preferred_element_type=F32)
        state = st_sc[hh]
        cross = jnp.dot(q, state.astype(BF16), preferred_element_type=F32) * qd_ref[hh]
        k_dec = (k.astype(F32) * kd_ref[hh]).astype(BF16)
        st_sc[hh] = state * cd_ref[hh] + lax.dot_general(
            k_dec, v, (((0,), (0,)), ((), ())), preferred_element_type=F32)
        o = intra + cross
        o = o * lax.rsqrt(jnp.mean(o * o, axis=-1, keepdims=True) + RMS_EPS)
        o_ref[:, hh * dv:(hh + 1) * dv] = (g_ref[:, hh * dv:(hh + 1) * dv].astype(F32) * o).astype(o_ref.dtype)


def _retention_tables(tile):
    heads = RET_HEADS
    log_gamma = jnp.log1p(-(2.0 ** (-5.0 - jnp.arange(heads, dtype=F32))))
    pos = jnp.arange(tile, dtype=F32)
    chunk = jnp.arange(tile, dtype=jnp.int32) // CHUNK
    dist = pos[:, None] - pos[None, :]
    same = chunk[:, None] == chunk[None, :]
    earlier = chunk[None, :] < chunk[:, None]
    expo = jnp.where(same, jnp.abs(dist), dist)
    dmask = jnp.where((same | earlier)[None], jnp.exp(log_gamma[:, None, None] * expo[None]), 0.0)
    q_dec = jnp.broadcast_to(jnp.exp(log_gamma[:, None] * (pos + 1.0))[:, :, None], (heads, tile, RET_V_DIM))
    k_dec = jnp.broadcast_to(jnp.exp(log_gamma[:, None] * (tile - 1 - pos))[:, :, None], (heads, tile, RET_QK_DIM))
    c_dec = jnp.broadcast_to(jnp.exp(log_gamma * tile)[:, None, None], (heads, 1, RET_V_DIM))
    return dmask, q_dec, k_dec, c_dec


def _retention(rqk16, rv16, rg16, batch, seq, hp=4):
    heads, dk, dv = RET_HEADS, RET_QK_DIM, RET_V_DIM
    tile = min(RET_TILE, seq)
    nt = seq // tile
    groups = heads // hp
    dmask, q_dec, k_dec, c_dec = _retention_tables(tile)
    val = pl.BlockSpec((tile, hp * dv), lambda b, g, t: (b * nt + t, g))
    return pl.pallas_call(
        functools.partial(_ret_kernel, hp=hp), grid=(batch, groups, nt),
        in_specs=[pl.BlockSpec((tile, hp * dk), lambda b, g, t: (b * nt + t, g)),
                  pl.BlockSpec((tile, hp * dk), lambda b, g, t: (b * nt + t, groups + g)),
                  val, val,
                  pl.BlockSpec((hp, tile, tile), lambda b, g, t: (g, 0, 0)),
                  pl.BlockSpec((hp, tile, dv), lambda b, g, t: (g, 0, 0)),
                  pl.BlockSpec((hp, tile, dk), lambda b, g, t: (g, 0, 0)),
                  pl.BlockSpec((hp, 1, dv), lambda b, g, t: (g, 0, 0))],
        out_specs=val,
        out_shape=jax.ShapeDtypeStruct((batch * seq, heads * dv), BF16),
        scratch_shapes=[pltpu.VMEM((hp, dk, dv), F32)],
        compiler_params=_params("parallel", "parallel", "arbitrary"), name="retention",
    )(rqk16, rqk16, rv16, rg16, dmask, q_dec, k_dec, c_dec)


def _merge_kernel(yf_ref, yr_ref, wf_ref, wr_ref, gf_ref, gr_ref, o_ref):
    fox = jnp.dot(yf_ref[...], wf_ref[...], preferred_element_type=F32)
    ret = jnp.dot(yr_ref[...], wr_ref[...], preferred_element_type=F32)
    o_ref[...] = (gf_ref[...].astype(F32) * fox + gr_ref[...].astype(F32) * ret).astype(o_ref.dtype)


def _merge(y_fox, y_ret, w_fox16, w_ret16, gates16, tn=512):
    n, kf = y_fox.shape
    kr = y_ret.shape[1]
    d = w_fox16.shape[1]
    tm = min(MATMUL_ROWS, n)
    nj = d // tn
    return pl.pallas_call(
        _merge_kernel, grid=(n // tm, nj),
        in_specs=[pl.BlockSpec((tm, kf), lambda i, j: (i, 0)),
                  pl.BlockSpec((tm, kr), lambda i, j: (i, 0)),
                  pl.BlockSpec((kf, tn), lambda i, j: (0, j)),
                  pl.BlockSpec((kr, tn), lambda i, j: (0, j)),
                  pl.BlockSpec((tm, tn), lambda i, j: (i, j)),
                  pl.BlockSpec((tm, tn), lambda i, j: (i, nj + j))],
        out_specs=pl.BlockSpec((tm, tn), lambda i, j: (i, j)),
        out_shape=jax.ShapeDtypeStruct((n, d), BF16),
        compiler_params=_params("parallel", "parallel"), name="merge",
    )(y_fox, y_ret, w_fox16, w_ret16, gates16, gates16)


def _lane_pick(mask, values):
    return jnp.sum(jnp.where(mask, values, 0.0), axis=1, keepdims=True)


def _outproj_router_kernel(m_ref, w_ref, h_ref, g_ref, b_ref, wr_ref, br_ref,
                           h1_ref, h1p_ref, route_ref, wcol_ref, cnt_ref, carry_sc, *, alpha):
    i = pl.program_id(0)

    @pl.when(i == 0)
    def _():
        carry_sc[...] = jnp.zeros(carry_sc.shape, F32)

    mix = jnp.dot(m_ref[...], w_ref[...], preferred_element_type=F32)
    hn = _layer_norm_rows(alpha * h_ref[...] + mix, g_ref[...], b_ref[...])
    h1_ref[...] = hn
    h1p_ref[...] = _pack_halves(hn)

    hn_hi = hn.astype(BF16)
    hn_lo = (hn - hn_hi.astype(F32)).astype(BF16)
    part = jnp.dot(hn_hi, wr_ref[...], preferred_element_type=F32)
    logits = (part[:, :LANES] + part[:, LANES:] + br_ref[...]
              + jnp.dot(hn_lo, wr_ref[:, :LANES], preferred_element_type=F32))
    tm = logits.shape[0]
    lane = lax.broadcasted_iota(jnp.int32, logits.shape, 1)
    neg = -jnp.inf
    gl = jnp.where(lane < N_GROUPS, logits, neg)
    gmax = jnp.max(gl, axis=1, keepdims=True)
    grp_w = 1.0 / jnp.sum(jnp.exp(gl - gmax), axis=1, keepdims=True)
    gidx = jnp.min(jnp.where(gl == gmax, lane, LANES), axis=1, keepdims=True)
    e_lane = lane - EXPERT_LANE0
    in_grp = (e_lane >= 0) & (e_lane < N_EXPERTS) & ((e_lane >> LOG2_EXPERTS_PER_GROUP) == gidx)
    el = jnp.where(in_grp, logits, neg)
    max1 = jnp.max(el, axis=1, keepdims=True)
    i1 = jnp.min(jnp.where(el == max1, lane, LANES), axis=1, keepdims=True)
    el2 = jnp.where(lane == i1, neg, el)
    max2 = jnp.max(el2, axis=1, keepdims=True)
    i2 = jnp.min(jnp.where(el2 == max2, lane, LANES), axis=1, keepdims=True)
    ratio = jnp.exp(max2 - max1)
    w1 = grp_w / (1.0 + ratio)
    w2 = grp_w * ratio / (1.0 + ratio)
    sel1 = lane == i1
    sel2 = lane == i2
    sel = sel1.astype(F32) + sel2.astype(F32)
    r_i = lax.broadcasted_iota(jnp.int32, (tm, tm), 0)
    c_i = lax.broadcasted_iota(jnp.int32, (tm, tm), 1)
    before = (c_i < r_i).astype(BF16)
    rank = carry_sc[...] + jnp.dot(before, sel.astype(BF16), preferred_element_type=F32)
    new_carry = carry_sc[...] + jnp.sum(sel, axis=0, keepdims=True)
    carry_sc[...] = new_carry
    cnt_ref[...] = jnp.broadcast_to(new_carry, cnt_ref.shape)
    r1 = _lane_pick(sel1, rank)
    r2 = _lane_pick(sel2, rank)
    e1 = (i1 - EXPERT_LANE0).astype(F32)
    e2 = (i2 - EXPERT_LANE0).astype(F32)
    rec = jnp.where(lane == 0, e1, jnp.where(lane == 1, e2, jnp.where(lane == 2, r1, jnp.where(lane == 3, r2, 0.0))))
    route_ref[...] = rec.T[0:8, :]
    wcol_ref[...] = jnp.where(lane == 0, w1, jnp.where(lane == 1, w2, 0.0))


def _outproj_router(merged16, w_out16, h32, g, b, w_router, b_router, alpha):
    n, d = merged16.shape
    tm = min(ROW_TILE, n)
    row16 = pl.BlockSpec((tm, d), lambda i: (i, 0))
    vec = pl.BlockSpec((1, d), lambda i: (0, 0))
    return pl.pallas_call(
        functools.partial(_outproj_router_kernel, alpha=alpha), grid=(n // tm,),
        in_specs=[row16, pl.BlockSpec((d, d), lambda i: (0, 0)), row16, vec, vec,
                  pl.BlockSpec((d, 2 * LANES), lambda i: (0, 0)), pl.BlockSpec((1, LANES), lambda i: (0, 0))],
        out_specs=[row16, pl.BlockSpec((tm, d // 2), lambda i: (i, 0)), pl.BlockSpec((8, tm), lambda i: (0, i)),
                   pl.BlockSpec((tm, LANES), lambda i: (i, 0)), pl.BlockSpec((8, LANES), lambda i: (0, 0))],
        out_shape=[jax.ShapeDtypeStruct((n, d), F32), jax.ShapeDtypeStruct((n, d // 2), jnp.uint32),
                   jax.ShapeDtypeStruct((8, n), F32),
                   jax.ShapeDtypeStruct((n, LANES), F32), jax.ShapeDtypeStruct((8, LANES), F32)],
        scratch_shapes=[pltpu.VMEM((1, LANES), F32)],
        compiler_params=_params("arbitrary"), name="outproj_router",
    )(merged16, w_out16, h32, g.reshape(1, d), b.reshape(1, d), w_router, b_router)


def _plan_kernel(route_ref, cnt_ref, dest_ref, tiles_ref, *, tile, log2_tile):
    cnt = cnt_ref[...].astype(jnp.int32)
    padded = ((cnt + (tile - 1)) >> log2_tile) << log2_tile
    lane = lax.broadcasted_iota(jnp.int32, padded.shape, 1)
    end = padded
    k = 1
    while k < LANES:
        end = end + jnp.where(lane >= k, pltpu.roll(end, k, 1), 0)
        k *= 2
    start = end - padded
    rows = EXPERT_LANE0 + N_EXPERTS
    rows = -(-rows // 8) * 8
    def lane_to_sublane(v):
        return jnp.broadcast_to(v[0:1, :].astype(F32), (LANES, LANES)).T[0:rows, 0:1]

    start_col = lane_to_sublane(start)
    end_col = lane_to_sublane(end)
    route = route_ref[...]
    n = route.shape[1]
    sub = lax.broadcasted_iota(jnp.int32, (rows, n), 0)
    e1 = route[0:1, :].astype(jnp.int32) + EXPERT_LANE0
    e2 = route[1:2, :].astype(jnp.int32) + EXPERT_LANE0
    d1 = jnp.sum(jnp.where(sub == e1, start_col, 0.0), axis=0, keepdims=True) + route[2:3, :]
    d2 = jnp.sum(jnp.where(sub == e2, start_col, 0.0), axis=0, keepdims=True) + route[3:4, :]
    row8 = lax.broadcasted_iota(jnp.int32, dest_ref.shape, 0)
    dest_ref[...] = jnp.where(row8 == 0, d1, jnp.where(row8 == 1, d2, 0.0)).astype(jnp.int32)
    nt = tiles_ref.shape[1]
    t_start = (lax.broadcasted_iota(jnp.int32, (rows, nt), 1) * tile).astype(F32)
    sub_t = lax.broadcasted_iota(jnp.int32, (rows, nt), 0)
    is_expert = (sub_t >= EXPERT_LANE0) & (sub_t < EXPERT_LANE0 + N_EXPERTS)
    done = jnp.sum(jnp.where(is_expert & (end_col <= t_start), 1.0, 0.0), axis=0, keepdims=True)
    expert = jnp.minimum(done, float(N_EXPERTS - 1))
    used = jnp.max(end_col, axis=0, keepdims=True) * (1.0 / tile)
    row8t = lax.broadcasted_iota(jnp.int32, tiles_ref.shape, 0)
    tiles_ref[...] = jnp.where(row8t == 0, expert, jnp.broadcast_to(used, tiles_ref.shape)).astype(jnp.int32)


def _plan(route, counts, tile, n_tiles):
    n = route.shape[1]
    nt_pad = -(-n_tiles // LANES) * LANES
    return pl.pallas_call(
        functools.partial(_plan_kernel, tile=tile, log2_tile=tile.bit_length() - 1),
        out_shape=[jax.ShapeDtypeStruct((8, n), jnp.int32), jax.ShapeDtypeStruct((8, nt_pad), jnp.int32)],
        compiler_params=pltpu.CompilerParams(vmem_limit_bytes=VMEM_LIMIT), name="plan",
    )(route, counts)


def _start_rows(copies, n_rows, thread=None):
    def start(o, c):
        for u in range(DMA_UNROLL):
            for cp in copies(o * DMA_UNROLL + u):
                cp.start(priority=u % N_DMA_THREADS if thread is None else thread)
        return c

    lax.fori_loop(0, n_rows // DMA_UNROLL, start, 0)


def _wait_rows(copies, n_rows):
    def wait(o, c):
        for u in range(DMA_UNROLL):
            for cp in copies(o * DMA_UNROLL + u):
                cp.wait()
        return c

    lax.fori_loop(0, n_rows // DMA_UNROLL, wait, 0)


def _invert_kernel(d1_ref, d2_ref, tok_ref, *, unroll):
    def clear(o, c):
        for u in range(unroll):
            tok_ref[o * unroll + u] = 0
        return c

    def put(o, c):
        for u in range(unroll):
            n = o * unroll + u
            tok_ref[d1_ref[n]] = n
            tok_ref[d2_ref[n]] = n
        return c

    lax.fori_loop(0, tok_ref.shape[0] // unroll, clear, 0)
    lax.fori_loop(0, d1_ref.shape[0] // unroll, put, 0)


def _invert(d1, d2, n_slots):
    return pl.pallas_call(
        functools.partial(_invert_kernel, unroll=DMA_UNROLL),
        grid_spec=pltpu.PrefetchScalarGridSpec(
            num_scalar_prefetch=2, grid=(),
            in_specs=[], out_specs=pl.BlockSpec(memory_space=pltpu.SMEM)),
        out_shape=jax.ShapeDtypeStruct((n_slots,), jnp.int32), name="invert",
    )(d1, d2)


def _expert_kernel(te_ref, nu_ref, tok_ref, h_hbm, wg_ref, wu_ref, wd_ref, y_ref,
                   xbuf0, xbuf1, wg_sc, wu_sc, wd_sc, sem, *, tile):
    t = pl.program_id(0)
    n_used = nu_ref[0]
    bufs = (xbuf0, xbuf1)

    def row_copy(tt, slot, r):
        src = h_hbm.at[pl.ds(tok_ref[tt * tile + r], 1), :]
        return pltpu.make_async_copy(src, bufs[slot].at[pl.ds(r, 1), :], sem.at[slot])

    @pl.when(t == 0)
    def _():
        _start_rows(lambda r: (row_copy(t, 0, r),), tile, thread=GATHER_THREAD)

    nxt = jnp.minimum(t + 1, n_used - 1)

    def tile_step(slot):
        _wait_rows(lambda r: (row_copy(t, slot, r),), tile)

        @pl.when((t == 0) | (te_ref[t] != te_ref[jnp.maximum(t - 1, 0)]))
        def _():
            wg_sc[...] = wg_ref[0].astype(BF16)
            wu_sc[...] = wu_ref[0].astype(BF16)
            wd_sc[...] = wd_ref[0].astype(BF16)

        for r in range(tile):
            row_copy(nxt, 1 - slot, r).start(priority=GATHER_THREAD)
        x_lo, x_hi = _unpack_halves(bufs[slot][...])
        x = jnp.concatenate([x_lo.astype(BF16), x_hi.astype(BF16)], axis=1)
        gate = jnp.dot(x, wg_sc[...], preferred_element_type=F32)
        up = jnp.dot(x, wu_sc[...], preferred_element_type=F32)
        hid = (gate * jax.nn.sigmoid(gate) * up).astype(BF16)
        y_ref[...] = _pack_halves(jnp.dot(hid, wd_sc[...], preferred_element_type=F32))

        @pl.when(t == n_used - 1)
        def _():
            _wait_rows(lambda r: (row_copy(nxt, 1 - slot, r),), tile)

    for slot in range(2):
        pl.when((t < n_used) & (t % 2 == slot))(functools.partial(tile_step, slot))

    @pl.when(t >= n_used)
    def _():
        y_ref[...] = jnp.zeros(y_ref.shape, y_ref.dtype)


def _experts(tile_expert, n_used, tok, h_packed, w_gate, w_up, w_down, layer, tile):
    dp = h_packed.shape[1]
    d, f = w_gate.shape[1], w_gate.shape[2]
    p = tok.shape[0]
    first = layer * N_EXPERTS

    def weights(shape):
        return pl.BlockSpec((1,) + shape, lambda t, te, nu, tk: (first + te[t], 0, 0))

    return pl.pallas_call(
        functools.partial(_expert_kernel, tile=tile),
        grid_spec=pltpu.PrefetchScalarGridSpec(
            num_scalar_prefetch=3, grid=(p // tile,),
            in_specs=[pl.BlockSpec(memory_space=pl.ANY), weights((d, f)), weights((d, f)), weights((f, d))],
            out_specs=pl.BlockSpec((tile, dp), lambda t, te, nu, tk: (t, 0)),
            scratch_shapes=[pltpu.VMEM((tile, dp), jnp.uint32), pltpu.VMEM((tile, dp), jnp.uint32),
                            pltpu.VMEM((d, f), BF16), pltpu.VMEM((d, f), BF16),
                            pltpu.VMEM((f, d), BF16), pltpu.SemaphoreType.DMA((2,))]),
        out_shape=jax.ShapeDtypeStruct((p, dp), jnp.uint32),
        compiler_params=_params("arbitrary"), name="experts",
    )(tile_expert, n_used, tok, h_packed, w_gate, w_up, w_down)


def _combine_kernel(d1_ref, d2_ref, h_ref, wc_ref, g_ref, b_ref, y_hbm, o32_ref, o16_ref,
                    a0, b0, a1, b1, sem, *, tile, alpha, n_steps):
    i = pl.program_id(0)
    bufs = ((a0, b0), (a1, b1))

    def copies(ii, slot, r):
        n = ii * tile + r
        first, second = bufs[slot]
        return (pltpu.make_async_copy(y_hbm.at[pl.ds(d1_ref[n], 1), :], first.at[pl.ds(r, 1), :], sem.at[slot]),
                pltpu.make_async_copy(y_hbm.at[pl.ds(d2_ref[n], 1), :], second.at[pl.ds(r, 1), :], sem.at[slot]))

    @pl.when(i == 0)
    def _():
        _start_rows(lambda r: copies(i, 0, r), tile)

    nxt = jnp.minimum(i + 1, n_steps - 1)

    def tile_step(slot):
        _wait_rows(lambda r: copies(i, slot, r), tile)
        for r in range(tile):
            for cp in copies(nxt, 1 - slot, r):
                cp.start(priority=r % N_DMA_THREADS)
        first, second = bufs[slot]
        wc = wc_ref[...]
        w1, w2 = wc[:, 0:1], wc[:, 1:2]
        f_lo, f_hi = _unpack_halves(first[...])
        s_lo, s_hi = _unpack_halves(second[...])
        ffn = jnp.concatenate([w1 * f_lo + w2 * s_lo, w1 * f_hi + w2 * s_hi], axis=1)
        y = _layer_norm_rows(alpha * h_ref[...] + ffn, g_ref[...], b_ref[...])
        o32_ref[...] = y
        o16_ref[...] = y.astype(BF16)

        @pl.when(i == n_steps - 1)
        def _():
            _wait_rows(lambda r: copies(nxt, 1 - slot, r), tile)

    for slot in range(2):
        pl.when(i % 2 == slot)(functools.partial(tile_step, slot))


def _combine(d1, d2, h32, wcol, g, b, y, alpha):
    n, d = h32.shape
    tile = min(GATHER_TILE, n)
    row = pl.BlockSpec((tile, d), lambda i, a, c: (i, 0))
    vec = pl.BlockSpec((1, d), lambda i, a, c: (0, 0))
    return pl.pallas_call(
        functools.partial(_combine_kernel, tile=tile, alpha=alpha, n_steps=n // tile),
        grid_spec=pltpu.PrefetchScalarGridSpec(
            num_scalar_prefetch=2, grid=(n // tile,),
            in_specs=[row, pl.BlockSpec((tile, LANES), lambda i, a, c: (i, 0)), vec, vec,
                      pl.BlockSpec(memory_space=pl.ANY)],
            out_specs=[row, row],
            scratch_shapes=[pltpu.VMEM((tile, y.shape[1]), y.dtype)] * 4 + [pltpu.SemaphoreType.DMA((2,))]),
        out_shape=[jax.ShapeDtypeStruct((n, d), F32), jax.ShapeDtypeStruct((n, d), BF16)],
        compiler_params=_params("arbitrary"), name="combine_ln",
    )(d1, d2, h32, wcol, g.reshape(1, d), b.reshape(1, d), y)


def _rotary_tables(seq):
    half = RET_QK_DIM // 2
    inv_freq = ROPE_BASE ** (-jnp.arange(half, dtype=F32) / half)
    ang = jnp.arange(seq, dtype=jnp.int32).astype(F32)[:, None] * inv_freq[None, :]
    cos, sin = jnp.cos(ang), jnp.sin(ang)
    return jnp.concatenate([cos, cos], axis=1), jnp.concatenate([-sin, sin], axis=1)


def _col_scale(sizes_scales):
    return jnp.concatenate([jnp.full((size,), scale, F32) for size, scale in sizes_scales])


def kernel(x, ln_in_g, ln_in_b, w_in, b_forget, w_branch_fox, w_branch_ret, w_out, ln1_g, ln1_b,
           w_router_group, b_router_group, w_router_expert, b_router_expert, w_gate, w_up, w_down, ln2_g, ln2_b):
    batch, seq, d = x.shape
    n = batch * seq
    depth = w_in.shape[0]
    alpha = (2.0 * depth) ** 0.25
    fox_w = FOX_HEADS * FOX_HEAD_DIM
    ret_qk = RET_HEADS * RET_QK_DIM
    ret_v = RET_HEADS * RET_V_DIM
    d_expert = w_gate.shape[-1]
    o_ff = 3 * fox_w
    o_rq = o_ff + FOX_HEADS
    o_rv = o_rq + 2 * ret_qk
    o_rg = o_rv + ret_v
    o_gate = o_rg + ret_v
    tables = _rotary_tables(seq)
    fox_scale = _col_scale([(fox_w, FOX_HEAD_DIM ** -0.5 * LOG2E), (2 * fox_w, 1.0)])
    rqk_scale = _col_scale([(ret_qk, 1.0), (ret_qk, RET_QK_DIM ** -0.5)])
    ones_v = jnp.ones((ret_v,), F32)
    ones_g = jnp.ones((2 * d,), F32)
    n_tiles = (N_EXPERTS * (EXPERT_TILE - 1) + 2 * n) // EXPERT_TILE + 1
    n_slots = n_tiles * EXPERT_TILE

    h32, h16 = _layer_norm(x.reshape(n, d), ln_in_g, ln_in_b)
    for l in range(depth):
        w = w_in[l]
        qkv = _proj(h16, w[:, :o_ff].astype(BF16), fox_scale, "none", seq)
        aug = _forget_cumsum(h16, w[:, o_ff:o_rq], b_forget[l], batch, seq)
        rqk = _proj(h16, w[:, o_rq:o_rv].astype(BF16), rqk_scale, "rotary", seq, tables)
        rv = _proj(h16, w[:, o_rv:o_rg].astype(BF16), ones_v, "none", seq)
        rg = _proj(h16, w[:, o_rg:o_gate].astype(BF16), ones_v, "silu", seq)
        gates = _proj(h16, w[:, o_gate:].astype(BF16), ones_g, "sigmoid", seq)
        y_fox = _fox_attention(qkv, aug, batch, seq)
        y_ret = _retention(rqk, rv, rg, batch, seq)
        merged = _merge(y_fox, y_ret, w_branch_fox[l].astype(BF16), w_branch_ret[l].astype(BF16), gates)

        w_router = jnp.zeros((d, LANES), F32)
        w_router = w_router.at[:, :N_GROUPS].set(w_router_group[l])
        w_router = w_router.at[:, EXPERT_LANE0:EXPERT_LANE0 + N_EXPERTS].set(w_router_expert[l])
        b_router = jnp.zeros((1, LANES), F32)
        b_router = b_router.at[0, :N_GROUPS].set(b_router_group[l])
        b_router = b_router.at[0, EXPERT_LANE0:EXPERT_LANE0 + N_EXPERTS].set(b_router_expert[l])
        w_router_hi = lax.reduce_precision(w_router, exponent_bits=8, mantissa_bits=7)
        w_router_lo = w_router - w_router_hi
        h1, h1_packed, route, wcol, counts = _outproj_router(
            merged, w_out[l].astype(BF16), h32, ln1_g[l], ln1_b[l],
            jnp.concatenate([w_router_hi, w_router_lo], axis=1).astype(BF16), b_router, alpha)
        dest, tiles = _plan(route, counts, EXPERT_TILE, n_tiles)
        d1, d2 = dest[0], dest[1]
        tok = _invert(d1, d2, n_slots)
        y = _experts(tiles[0, :n_tiles], tiles[1, :1], tok, h1_packed,
                     w_gate.reshape(depth * N_EXPERTS, d, d_expert), w_up.reshape(depth * N_EXPERTS, d, d_expert),
                     w_down.reshape(depth * N_EXPERTS, d_expert, d), l, EXPERT_TILE)
        h32, h16 = _combine(d1, d2, h1, wcol, ln2_g[l], ln2_b[l], y, alpha)
    return h32.reshape(batch, seq, d)
```

```python
import functools

import jax
import jax.numpy as jnp
from jax import lax
from jax.experimental import pallas as pl
from jax.experimental.pallas import tpu as pltpu

F32 = jnp.float32
BF16 = jnp.bfloat16

CHUNK = 64
FOX_HEADS = 8
FOX_HEAD_DIM = 128
RET_HEADS = 8
RET_QK_DIM = 128
RET_V_DIM = 256
ROPE_BASE = 10000.0
N_GROUPS = 4
EXPERTS_PER_GROUP = 8
N_EXPERTS = N_GROUPS * EXPERTS_PER_GROUP
LOG2_EXPERTS_PER_GROUP = EXPERTS_PER_GROUP.bit_length() - 1
LN_EPS = 1e-5
RMS_EPS = 1e-6

LANES = 128
EXPERT_LANE0 = N_GROUPS
VMEM_LIMIT = 56 * 1024 * 1024

ROW_TILE = 512
MATMUL_ROWS = 1024
MERGE_ROWS = 2048
DMA_UNROLL = 32
N_DMA_THREADS = 2
GATHER_THREAD = 1
LOG2E = 1.4426950408889634
ATTN_TILE = 256
RET_TILE = 256
EXPERT_TILE = 256
GATHER_TILE = 256


def _params(*sem):
    return pltpu.CompilerParams(dimension_semantics=sem, vmem_limit_bytes=VMEM_LIMIT)


def _pack_halves(x):
    w = x.shape[1] // 2
    lo = lax.bitcast_convert_type(x[:, :w].astype(BF16).astype(F32), jnp.uint32)
    hi = lax.bitcast_convert_type(x[:, w:].astype(BF16).astype(F32), jnp.uint32)
    return (lo >> 16) | hi


def _unpack_halves(u):
    lo = lax.bitcast_convert_type(u << 16, F32)
    hi = lax.bitcast_convert_type(u & jnp.uint32(0xFFFF0000), F32)
    return lo, hi


def _layer_norm_rows(x, g, b):
    mu = jnp.mean(x, axis=-1, keepdims=True)
    xc = x - mu
    var = jnp.mean(xc * xc, axis=-1, keepdims=True)
    return xc * lax.rsqrt(var + LN_EPS) * g + b


def _ln_kernel(x_ref, g_ref, b_ref, o32_ref, o16_ref):
    y = _layer_norm_rows(x_ref[...], g_ref[...], b_ref[...])
    o32_ref[...] = y
    o16_ref[...] = y.astype(BF16)


def _layer_norm(x, g, b):
    n, d = x.shape
    tm = min(ROW_TILE, n)
    row = pl.BlockSpec((tm, d), lambda i: (i, 0))
    vec = pl.BlockSpec((1, d), lambda i: (0, 0))
    return pl.pallas_call(
        _ln_kernel, grid=(n // tm,), in_specs=[row, vec, vec], out_specs=[row, row],
        out_shape=[jax.ShapeDtypeStruct((n, d), F32), jax.ShapeDtypeStruct((n, d), BF16)],
        compiler_params=_params("parallel"), name="ln_in",
    )(x, g.reshape(1, d), b.reshape(1, d))


def _proj_kernel(x_ref, w_ref, s_ref, *rest, act):
    acc = jnp.dot(x_ref[...], w_ref[...], preferred_element_type=F32) * s_ref[...]
    if act == "rotary":
        cos_ref, sin_ref, o_ref = rest
        cos, sin = cos_ref[...], sin_ref[...]
        for c in range(acc.shape[1] // LANES):
            blk = acc[:, c * LANES:(c + 1) * LANES]
            rot = blk * cos + pltpu.roll(blk, LANES // 2, 1) * sin
            o_ref[:, c * LANES:(c + 1) * LANES] = rot.astype(o_ref.dtype)
        return
    (o_ref,) = rest
    if act == "silu":
        acc = acc * jax.nn.sigmoid(acc)
    elif act == "sigmoid":
        acc = jax.nn.sigmoid(acc)
    o_ref[...] = acc.astype(o_ref.dtype)


def _proj(x16, w16, colscale, act, seq, tables=None, tn=1024):
    n, k = x16.shape
    m = w16.shape[1]
    tm = min(MATMUL_ROWS, seq)
    in_specs = [pl.BlockSpec((tm, k), lambda i, j: (i, 0)),
                pl.BlockSpec((k, tn), lambda i, j: (0, j)),
                pl.BlockSpec((1, tn), lambda i, j: (0, j))]
    args = [x16, w16, colscale.reshape(1, m)]
    if act == "rotary":
        per_seq = seq // tm
        tab = pl.BlockSpec((tm, LANES), lambda i, j: (i % per_seq, 0))
        in_specs += [tab, tab]
        args += list(tables)
    return pl.pallas_call(
        functools.partial(_proj_kernel, act=act), grid=(n // tm, m // tn),
        in_specs=in_specs, out_specs=pl.BlockSpec((tm, tn), lambda i, j: (i, j)),
        out_shape=jax.ShapeDtypeStruct((n, m), BF16),
        compiler_params=_params("parallel", "parallel"), name="proj_" + act,
    )(*args)


def _forget_kernel(h_ref, w_ref, b_ref, o_ref, *, heads, block):
    z = jnp.dot(h_ref[...], w_ref[...], preferred_element_type=F32) + b_ref[...]
    lf = (jnp.minimum(z, 0.0) - jnp.log1p(jnp.exp(-jnp.abs(z)))) * LOG2E
    seq = lf.shape[0]
    r_i = lax.broadcasted_iota(jnp.int32, (block, block), 0)
    c_i = lax.broadcasted_iota(jnp.int32, (block, block), 1)
    upto = (c_i <= r_i).astype(BF16)
    lane = lax.broadcasted_iota(jnp.int32, (block, LANES), 1)
    carry = jnp.zeros((1, LANES), F32)
    for blk in range(seq // block):
        rows = slice(blk * block, (blk + 1) * block)
        part = lf[rows]
        cum = carry
        for _ in range(N_SPLIT):
            piece = part.astype(BF16)
            cum = cum + jnp.dot(upto, piece, preferred_element_type=F32)
            part = part - piece.astype(F32)
        carry = cum[block - 1:block, :]
        for h in range(heads):
            col = jnp.broadcast_to(cum[:, h:h + 1], (block, LANES))
            hi = col.astype(BF16).astype(F32)
            mid = (col - hi).astype(BF16).astype(F32)
            lo = col - hi - mid
            split = jnp.where(lane == 0, hi, jnp.where(lane == 1, mid, jnp.where(lane == 2, lo, 0.0)))
            o_ref[rows, h * LANES:(h + 1) * LANES] = split.astype(BF16)


def _forget_cumsum(h16, w_forget, b_forget, batch, seq):
    d, heads = w_forget.shape
    w_pad = jnp.zeros((d, LANES), BF16).at[:, :heads].set(w_forget.astype(BF16))
    b_pad = jnp.zeros((1, LANES), F32).at[0, :heads].set(b_forget.astype(F32))
    return pl.pallas_call(
        functools.partial(_forget_kernel, heads=heads, block=min(256, seq)), grid=(batch,),
        in_specs=[pl.BlockSpec((seq, d), lambda b: (b, 0)),
                  pl.BlockSpec((d, LANES), lambda b: (0, 0)),
                  pl.BlockSpec((1, LANES), lambda b: (0, 0))],
        out_specs=pl.BlockSpec((seq, heads * LANES), lambda b: (b, 0)),
        out_shape=jax.ShapeDtypeStruct((batch * seq, heads * LANES), BF16),
        compiler_params=_params("parallel"), name="forget_cumsum",
    )(h16, w_pad, b_pad)


N_SPLIT = 3


def _fox_kernel(q_ref, k_ref, v_ref, a_ref, o_ref, acc_sc, kaug_sc, *, tile, hp):
    qi = pl.program_id(2)
    dh = FOX_HEAD_DIM

    @pl.when(qi == 0)
    def _():
        for hh in range(hp):
            kaug_sc[:, 2 * hh * dh:(2 * hh + 1) * dh] = k_ref[:, hh * dh:(hh + 1) * dh]
            kaug_sc[:, (2 * hh + 1) * dh:(2 * hh + 2) * dh] = a_ref[:, hh * dh:(hh + 1) * dh]

    lane = lax.broadcasted_iota(jnp.int32, (tile, dh), 1)
    q_extra = jnp.where(lane < N_SPLIT, -1.0, 0.0).astype(BF16)
    q_aug = [jnp.concatenate([q_ref[:, hh * dh:(hh + 1) * dh], q_extra], axis=1) for hh in range(hp)]
    acc_sc[...] = jnp.zeros(acc_sc.shape, F32)

    def block(j, stats, diagonal):
        rows = pl.ds(pl.multiple_of(j * tile, tile), tile)
        s_t = []
        for hh in range(hp):
            s = lax.dot_general(kaug_sc[rows, 2 * hh * dh:(2 * hh + 2) * dh], q_aug[hh],
                                (((1,), (1,)), ((), ())), preferred_element_type=F32)
            if diagonal:
                key = lax.broadcasted_iota(jnp.int32, s.shape, 0)
                qry = lax.broadcasted_iota(jnp.int32, s.shape, 1)
                s = jnp.where(key <= qry, s, -jnp.inf)
            s_t.append(s)
        out, scale, pv = [], [], []
        for hh in range(hp):
            m_prev, l_prev = stats[hh]
            m_new = jnp.maximum(m_prev, jnp.max(s_t[hh], axis=0, keepdims=True))
            a = jnp.exp2(m_prev - m_new)
            p = jnp.exp2(s_t[hh] - m_new)
            out.append((m_new, a * l_prev + jnp.sum(p, axis=0, keepdims=True)))
            scale.append(a)
            pv.append(lax.dot_general(v_ref[rows, hh * dh:(hh + 1) * dh], p.astype(BF16),
                                      (((0,), (0,)), ((), ())), preferred_element_type=F32))
        for hh in range(hp):
            acc_sc[hh] = scale[hh] * acc_sc[hh] + pv[hh]
        return tuple(out)

    init = tuple((jnp.full((1, tile), -jnp.inf, F32), jnp.zeros((1, tile), F32)) for _ in range(hp))
    stats = lax.fori_loop(0, qi, lambda j, st: block(j, st, False), init)
    stats = block(qi, stats, True)
    for hh in range(hp):
        o_ref[:, hh * dh:(hh + 1) * dh] = (acc_sc[hh] / stats[hh][1]).T.astype(o_ref.dtype)


def _fox_attention(qkv16, aug16, batch, seq, hp=8):
    heads, dh = FOX_HEADS, FOX_HEAD_DIM
    tile = min(ATTN_TILE, seq)
    nq = seq // tile
    groups = heads // hp
    w = hp * dh
    return pl.pallas_call(
        functools.partial(_fox_kernel, tile=tile, hp=hp), grid=(batch, groups, nq),
        in_specs=[pl.BlockSpec((tile, w), lambda b, g, i: (b * nq + i, g)),
                  pl.BlockSpec((seq, w), lambda b, g, i: (b, groups + g)),
                  pl.BlockSpec((seq, w), lambda b, g, i: (b, 2 * groups + g)),
                  pl.BlockSpec((seq, w), lambda b, g, i: (b, g))],
        out_specs=pl.BlockSpec((tile, w), lambda b, g, i: (b * nq + i, g)),
        out_shape=jax.ShapeDtypeStruct((batch * seq, heads * dh), BF16),
        scratch_shapes=[pltpu.VMEM((hp, dh, tile), F32), pltpu.VMEM((seq, 2 * w), BF16)],
        compiler_params=_params("parallel", "parallel", "arbitrary"), name="fox_attention",
    )(qkv16, qkv16, qkv16, aug16)


def _ret_kernel(q_ref, k_ref, v_ref, g_ref, dm_ref, qd_ref, kd_ref, cd_ref, o_ref, st_sc, *, hp):
    @pl.when(pl.program_id(2) == 0)
    def _():
        st_sc[...] = jnp.zeros(st_sc.shape, F32)

    dk, dv = RET_QK_DIM, RET_V_DIM
    for hh in range(hp):
        q = q_ref[:, hh * dk:(hh + 1) * dk]
        k = k_ref[:, hh * dk:(hh + 1) * dk]
        v = v_ref[:, hh * dv:(hh + 1) * dv]
        scores = lax.dot_general(q, k, (((1,), (1,)), ((), ())), preferred_element_type=F32) * dm_ref[hh]
        intra = jnp.dot(scores.astype(BF16), v, preferred_element_type=F32)
        state = st_sc[hh]
        cross = jnp.dot(q, state.astype(BF16), preferred_element_type=F32) * qd_ref[hh]
        k_dec = (k.astype(F32) * kd_ref[hh]).astype(BF16)
        st_sc[hh] = state * cd_ref[hh] + lax.dot_general(
            k_dec, v, (((0,), (0,)), ((), ())), preferred_element_type=F32)
        o = intra + cross
        o = o * lax.rsqrt(jnp.mean(o * o, axis=-1, keepdims=True) + RMS_EPS)
        o_ref[:, hh * dv:(hh + 1) * dv] = (g_ref[:, hh * dv:(hh + 1) * dv].astype(F32) * o).astype(o_ref.dtype)


def _retention_tables(tile):
    heads = RET_HEADS
    log_gamma = jnp.log1p(-(2.0 ** (-5.0 - jnp.arange(heads, dtype=F32))))
    pos = jnp.arange(tile, dtype=F32)
    chunk = jnp.arange(tile, dtype=jnp.int32) // CHUNK
    dist = pos[:, None] - pos[None, :]
    same = chunk[:, None] == chunk[None, :]
    earlier = chunk[None, :] < chunk[:, None]
    expo = jnp.where(same, jnp.abs(dist), dist)
    dmask = jnp.where((same | earlier)[None], jnp.exp(log_gamma[:, None, None] * expo[None]), 0.0)
    q_dec = jnp.broadcast_to(jnp.exp(log_gamma[:, None] * (pos + 1.0))[:, :, None], (heads, tile, RET_V_DIM))
    k_dec = jnp.broadcast_to(jnp.exp(log_gamma[:, None] * (tile - 1 - pos))[:, :, None], (heads, tile, RET_QK_DIM))
    c_dec = jnp.broadcast_to(jnp.exp(log_gamma * tile)[:, None, None], (heads, 1, RET_V_DIM))
    return dmask, q_dec, k_dec, c_dec


def _retention(rqk16, rv16, rg16, batch, seq, hp=4):
    heads, dk, dv = RET_HEADS, RET_QK_DIM, RET_V_DIM
    tile = min(RET_TILE, seq)
    nt = seq // tile
    groups = heads // hp
    dmask, q_dec, k_dec, c_dec = _retention_tables(tile)
    val = pl.BlockSpec((tile, hp * dv), lambda b, g, t: (b * nt + t, g))
    return pl.pallas_call(
        functools.partial(_ret_kernel, hp=hp), grid=(batch, groups, nt),
        in_specs=[pl.BlockSpec((tile, hp * dk), lambda b, g, t: (b * nt + t, g)),
                  pl.BlockSpec((tile, hp * dk), lambda b, g, t: (b * nt + t, groups + g)),
                  val, val,
                  pl.BlockSpec((hp, tile, tile), lambda b, g, t: (g, 0, 0)),
                  pl.BlockSpec((hp, tile, dv), lambda b, g, t: (g, 0, 0)),
                  pl.BlockSpec((hp, tile, dk), lambda b, g, t: (g, 0, 0)),
                  pl.BlockSpec((hp, 1, dv), lambda b, g, t: (g, 0, 0))],
        out_specs=val,
        out_shape=jax.ShapeDtypeStruct((batch * seq, heads * dv), BF16),
        scratch_shapes=[pltpu.VMEM((hp, dk, dv), F32)],
        compiler_params=_params("parallel", "parallel", "arbitrary"), name="retention",
    )(rqk16, rqk16, rv16, rg16, dmask, q_dec, k_dec, c_dec)


def _merge_kernel(yf_ref, yr_ref, wf_ref, wr_ref, gf_ref, gr_ref, o_ref):
    fox = jnp.dot(yf_ref[...], wf_ref[...], preferred_element_type=F32)
    ret = jnp.dot(yr_ref[...], wr_ref[...], preferred_element_type=F32)
    o_ref[...] = (gf_ref[...].astype(F32) * fox + gr_ref[...].astype(F32) * ret).astype(o_ref.dtype)


def _merge(y_fox, y_ret, w_fox16, w_ret16, gates16, tn=512):
    n, kf = y_fox.shape
    kr = y_ret.shape[1]
    d = w_fox16.shape[1]
    tm = min(MERGE_ROWS, n)
    nj = d // tn
    return pl.pallas_call(
        _merge_kernel, grid=(n // tm, nj),
        in_specs=[pl.BlockSpec((tm, kf), lambda i, j: (i, 0)),
                  pl.BlockSpec((tm, kr), lambda i, j: (i, 0)),
                  pl.BlockSpec((kf, tn), lambda i, j: (0, j)),
                  pl.BlockSpec((kr, tn), lambda i, j: (0, j)),
                  pl.BlockSpec((tm, tn), lambda i, j: (i, j)),
                  pl.BlockSpec((tm, tn), lambda i, j: (i, nj + j))],
        out_specs=pl.BlockSpec((tm, tn), lambda i, j: (i, j)),
        out_shape=jax.ShapeDtypeStruct((n, d), BF16),
        compiler_params=_params("parallel", "parallel"), name="merge",
    )(y_fox, y_ret, w_fox16, w_ret16, gates16, gates16)


def _lane_pick(mask, values):
    return jnp.sum(jnp.where(mask, values, 0.0), axis=1, keepdims=True)


def _outproj_router_kernel(m_ref, w_ref, h_ref, g_ref, b_ref, wr_ref, br_ref,
                           h1_ref, h1p_ref, route_ref, wcol_ref, cnt_ref, carry_sc, *, alpha):
    i = pl.program_id(0)

    @pl.when(i == 0)
    def _():
        carry_sc[...] = jnp.zeros(carry_sc.shape, F32)

    mix = jnp.dot(m_ref[...], w_ref[...], preferred_element_type=F32)
    hn = _layer_norm_rows(alpha * h_ref[...] + mix, g_ref[...], b_ref[...])
    h1_ref[...] = hn
    h1p_ref[...] = _pack_halves(hn)

    hn_hi = hn.astype(BF16)
    hn_lo = (hn - hn_hi.astype(F32)).astype(BF16)
    part = jnp.dot(hn_hi, wr_ref[...], preferred_element_type=F32)
    logits = (part[:, :LANES] + part[:, LANES:] + br_ref[...]
              + jnp.dot(hn_lo, wr_ref[:, :LANES], preferred_element_type=F32))
    tm = logits.shape[0]
    lane = lax.broadcasted_iota(jnp.int32, logits.shape, 1)
    neg = -jnp.inf
    gl = jnp.where(lane < N_GROUPS, logits, neg)
    gmax = jnp.max(gl, axis=1, keepdims=True)
    grp_w = 1.0 / jnp.sum(jnp.exp(gl - gmax), axis=1, keepdims=True)
    gidx = jnp.min(jnp.where(gl == gmax, lane, LANES), axis=1, keepdims=True)
    e_lane = lane - EXPERT_LANE0
    in_grp = (e_lane >= 0) & (e_lane < N_EXPERTS) & ((e_lane >> LOG2_EXPERTS_PER_GROUP) == gidx)
    el = jnp.where(in_grp, logits, neg)
    max1 = jnp.max(el, axis=1, keepdims=True)
    i1 = jnp.min(jnp.where(el == max1, lane, LANES), axis=1, keepdims=True)
    el2 = jnp.where(lane == i1, neg, el)
    max2 = jnp.max(el2, axis=1, keepdims=True)
    i2 = jnp.min(jnp.where(el2 == max2, lane, LANES), axis=1, keepdims=True)
    ratio = jnp.exp(max2 - max1)
    w1 = grp_w / (1.0 + ratio)
    w2 = grp_w * ratio / (1.0 + ratio)
    sel1 = lane == i1
    sel2 = lane == i2
    sel = sel1.astype(F32) + sel2.astype(F32)
    r_i = lax.broadcasted_iota(jnp.int32, (tm, tm), 0)
    c_i = lax.broadcasted_iota(jnp.int32, (tm, tm), 1)
    before = (c_i < r_i).astype(BF16)
    rank = carry_sc[...] + jnp.dot(before, sel.astype(BF16), preferred_element_type=F32)
    new_carry = carry_sc[...] + jnp.sum(sel, axis=0, keepdims=True)
    carry_sc[...] = new_carry
    cnt_ref[...] = jnp.broadcast_to(new_carry, cnt_ref.shape)
    r1 = _lane_pick(sel1, rank)
    r2 = _lane_pick(sel2, rank)
    e1 = (i1 - EXPERT_LANE0).astype(F32)
    e2 = (i2 - EXPERT_LANE0).astype(F32)
    rec = jnp.where(lane == 0, e1, jnp.where(lane == 1, e2, jnp.where(lane == 2, r1, jnp.where(lane == 3, r2, 0.0))))
    route_ref[...] = rec.T[0:8, :]
    wcol_ref[...] = jnp.where(lane == 0, w1, jnp.where(lane == 1, w2, 0.0))


def _outproj_router(merged16, w_out16, h32, g, b, w_router, b_router, alpha):
    n, d = merged16.shape
    tm = min(ROW_TILE, n)
    row16 = pl.BlockSpec((tm, d), lambda i: (i, 0))
    vec = pl.BlockSpec((1, d), lambda i: (0, 0))
    return pl.pallas_call(
        functools.partial(_outproj_router_kernel, alpha=alpha), grid=(n // tm,),
        in_specs=[row16, pl.BlockSpec((d, d), lambda i: (0, 0)), row16, vec, vec,
                  pl.BlockSpec((d, 2 * LANES), lambda i: (0, 0)), pl.BlockSpec((1, LANES), lambda i: (0, 0))],
        out_specs=[row16, pl.BlockSpec((tm, d // 2), lambda i: (i, 0)), pl.BlockSpec((8, tm), lambda i: (0, i)),
                   pl.BlockSpec((tm, LANES), lambda i: (i, 0)), pl.BlockSpec((8, LANES), lambda i: (0, 0))],
        out_shape=[jax.ShapeDtypeStruct((n, d), F32), jax.ShapeDtypeStruct((n, d // 2), jnp.uint32),
                   jax.ShapeDtypeStruct((8, n), F32),
                   jax.ShapeDtypeStruct((n, LANES), F32), jax.ShapeDtypeStruct((8, LANES), F32)],
        scratch_shapes=[pltpu.VMEM((1, LANES), F32)],
        compiler_params=_params("arbitrary"), name="outproj_router",
    )(merged16, w_out16, h32, g.reshape(1, d), b.reshape(1, d), w_router, b_router)


def _plan_kernel(route_ref, cnt_ref, dest_ref, tiles_ref, *, tile, log2_tile):
    cnt = cnt_ref[...].astype(jnp.int32)
    padded = ((cnt + (tile - 1)) >> log2_tile) << log2_tile
    lane = lax.broadcasted_iota(jnp.int32, padded.shape, 1)
    end = padded
    k = 1
    while k < LANES:
        end = end + jnp.where(lane >= k, pltpu.roll(end, k, 1), 0)
        k *= 2
    start = end - padded
    rows = EXPERT_LANE0 + N_EXPERTS
    rows = -(-rows // 8) * 8
    def lane_to_sublane(v):
        return jnp.broadcast_to(v[0:1, :].astype(F32), (LANES, LANES)).T[0:rows, 0:1]

    start_col = lane_to_sublane(start)
    end_col = lane_to_sublane(end)
    route = route_ref[...]
    n = route.shape[1]
    sub = lax.broadcasted_iota(jnp.int32, (rows, n), 0)
    e1 = route[0:1, :].astype(jnp.int32) + EXPERT_LANE0
    e2 = route[1:2, :].astype(jnp.int32) + EXPERT_LANE0
    d1 = jnp.sum(jnp.where(sub == e1, start_col, 0.0), axis=0, keepdims=True) + route[2:3, :]
    d2 = jnp.sum(jnp.where(sub == e2, start_col, 0.0), axis=0, keepdims=True) + route[3:4, :]
    row8 = lax.broadcasted_iota(jnp.int32, dest_ref.shape, 0)
    dest_ref[...] = jnp.where(row8 == 0, d1, jnp.where(row8 == 1, d2, 0.0)).astype(jnp.int32)
    nt = tiles_ref.shape[1]
    t_start = (lax.broadcasted_iota(jnp.int32, (rows, nt), 1) * tile).astype(F32)
    sub_t = lax.broadcasted_iota(jnp.int32, (rows, nt), 0)
    is_expert = (sub_t >= EXPERT_LANE0) & (sub_t < EXPERT_LANE0 + N_EXPERTS)
    done = jnp.sum(jnp.where(is_expert & (end_col <= t_start), 1.0, 0.0), axis=0, keepdims=True)
    expert = jnp.minimum(done, float(N_EXPERTS - 1))
    used = jnp.max(end_col, axis=0, keepdims=True) * (1.0 / tile)
    row8t = lax.broadcasted_iota(jnp.int32, tiles_ref.shape, 0)
    tiles_ref[...] = jnp.where(row8t == 0, expert, jnp.broadcast_to(used, tiles_ref.shape)).astype(jnp.int32)


def _plan(route, counts, tile, n_tiles):
    n = route.shape[1]
    nt_pad = -(-n_tiles // LANES) * LANES
    return pl.pallas_call(
        functools.partial(_plan_kernel, tile=tile, log2_tile=tile.bit_length() - 1),
        out_shape=[jax.ShapeDtypeStruct((8, n), jnp.int32), jax.ShapeDtypeStruct((8, nt_pad), jnp.int32)],
        compiler_params=pltpu.CompilerParams(vmem_limit_bytes=VMEM_LIMIT), name="plan",
    )(route, counts)


def _start_rows(copies, n_rows, thread=None):
    def start(o, c):
        for u in range(DMA_UNROLL):
            for cp in copies(o * DMA_UNROLL + u):
                cp.start(priority=u % N_DMA_THREADS if thread is None else thread)
        return c

    lax.fori_loop(0, n_rows // DMA_UNROLL, start, 0)


def _wait_rows(copies, n_rows):
    def wait(o, c):
        for u in range(DMA_UNROLL):
            for cp in copies(o * DMA_UNROLL + u):
                cp.wait()
        return c

    lax.fori_loop(0, n_rows // DMA_UNROLL, wait, 0)


def _invert_kernel(d1_ref, d2_ref, tok_ref, *, unroll):
    def clear(o, c):
        for u in range(unroll):
            tok_ref[o * unroll + u] = 0
        return c

    def put(o, c):
        for u in range(unroll):
            n = o * unroll + u
            tok_ref[d1_ref[n]] = n
            tok_ref[d2_ref[n]] = n
        return c

    lax.fori_loop(0, tok_ref.shape[0] // unroll, clear, 0)
    lax.fori_loop(0, d1_ref.shape[0] // unroll, put, 0)


def _invert(d1, d2, n_slots):
    return pl.pallas_call(
        functools.partial(_invert_kernel, unroll=DMA_UNROLL),
        grid_spec=pltpu.PrefetchScalarGridSpec(
            num_scalar_prefetch=2, grid=(),
            in_specs=[], out_specs=pl.BlockSpec(memory_space=pltpu.SMEM)),
        out_shape=jax.ShapeDtypeStruct((n_slots,), jnp.int32), name="invert",
    )(d1, d2)


def _expert_kernel(te_ref, nu_ref, tok_ref, h_hbm, wg_hbm, wu_hbm, wd_hbm, y_ref,
                   xbuf0, xbuf1, wg_in, wu_in, wd_in, wg_sc, wu_sc, wd_sc, sem, wsem, *, tile, first):
    t = pl.program_id(0)
    n_used = nu_ref[0]
    bufs = (xbuf0, xbuf1)

    def row_copy(tt, slot, r):
        src = h_hbm.at[pl.ds(tok_ref[tt * tile + r], 1), :]
        return pltpu.make_async_copy(src, bufs[slot].at[pl.ds(r, 1), :], sem.at[slot])

    def weight_copies(expert):
        return (pltpu.make_async_copy(wg_hbm.at[first + expert], wg_in, wsem.at[0]),
                pltpu.make_async_copy(wu_hbm.at[first + expert], wu_in, wsem.at[1]),
                pltpu.make_async_copy(wd_hbm.at[first + expert], wd_in, wsem.at[2]))

    @pl.when(t == 0)
    def _():
        for cp in weight_copies(te_ref[0]):
            cp.start()
        _start_rows(lambda r: (row_copy(t, 0, r),), tile, thread=GATHER_THREAD)

    @pl.when((t < n_used) & ((t == 0) | (te_ref[t] != te_ref[jnp.maximum(t - 1, 0)])))
    def _():
        for cp in weight_copies(te_ref[t]):
            cp.wait()
        wg_sc[...] = wg_in[...].astype(BF16)
        wu_sc[...] = wu_in[...].astype(BF16)
        wd_sc[...] = wd_in[...].astype(BF16)
        run_end = lax.while_loop(lambda j: (j < n_used) & (te_ref[jnp.minimum(j, n_used - 1)] == te_ref[t]),
                                 lambda j: j + 1, t + 1)

        @pl.when(run_end < n_used)
        def _():
            for cp in weight_copies(te_ref[run_end]):
                cp.start()

    nxt = jnp.minimum(t + 1, n_used - 1)

    def tile_step(slot):
        _wait_rows(lambda r: (row_copy(t, slot, r),), tile)
        for r in range(tile):
            row_copy(nxt, 1 - slot, r).start(priority=r % N_DMA_THREADS)
        x_lo, x_hi = _unpack_halves(bufs[slot][...])
        x = jnp.concatenate([x_lo.astype(BF16), x_hi.astype(BF16)], axis=1)
        gate = jnp.dot(x, wg_sc[...], preferred_element_type=F32)
        up = jnp.dot(x, wu_sc[...], preferred_element_type=F32)
        hid = (gate * jax.nn.sigmoid(gate) * up).astype(BF16)
        y_ref[...] = _pack_halves(jnp.dot(hid, wd_sc[...], preferred_element_type=F32))

        @pl.when(t == n_used - 1)
        def _():
            _wait_rows(lambda r: (row_copy(nxt, 1 - slot, r),), tile)

    for slot in range(2):
        pl.when((t < n_used) & (t % 2 == slot))(functools.partial(tile_step, slot))

    @pl.when(t >= n_used)
    def _():
        y_ref[...] = jnp.zeros(y_ref.shape, y_ref.dtype)


def _experts(tile_expert, n_used, tok, h_packed, w_gate, w_up, w_down, layer, tile):
    dp = h_packed.shape[1]
    d, f = w_gate.shape[1], w_gate.shape[2]
    p = tok.shape[0]
    hbm = pl.BlockSpec(memory_space=pl.ANY)
    return pl.pallas_call(
        functools.partial(_expert_kernel, tile=tile, first=layer * N_EXPERTS),
        grid_spec=pltpu.PrefetchScalarGridSpec(
            num_scalar_prefetch=3, grid=(p // tile,),
            in_specs=[hbm, hbm, hbm, hbm],
            out_specs=pl.BlockSpec((tile, dp), lambda t, te, nu, tk: (t, 0)),
            scratch_shapes=[pltpu.VMEM((tile, dp), jnp.uint32), pltpu.VMEM((tile, dp), jnp.uint32),
                            pltpu.VMEM((d, f), F32), pltpu.VMEM((d, f), F32), pltpu.VMEM((f, d), F32),
                            pltpu.VMEM((d, f), BF16), pltpu.VMEM((d, f), BF16), pltpu.VMEM((f, d), BF16),
                            pltpu.SemaphoreType.DMA((2,)), pltpu.SemaphoreType.DMA((3,))]),
        out_shape=jax.ShapeDtypeStruct((p, dp), jnp.uint32),
        compiler_params=_params("arbitrary"), name="experts",
    )(tile_expert, n_used, tok, h_packed, w_gate, w_up, w_down)


def _combine_kernel(d1_ref, d2_ref, h_ref, wc_ref, g_ref, b_ref, y_hbm, o32_ref, o16_ref,
                    a0, b0, a1, b1, sem, *, tile, alpha, n_steps):
    i = pl.program_id(0)
    bufs = ((a0, b0), (a1, b1))

    def copies(ii, slot, r):
        n = ii * tile + r
        first, second = bufs[slot]
        return (pltpu.make_async_copy(y_hbm.at[pl.ds(d1_ref[n], 1), :], first.at[pl.ds(r, 1), :], sem.at[slot]),
                pltpu.make_async_copy(y_hbm.at[pl.ds(d2_ref[n], 1), :], second.at[pl.ds(r, 1), :], sem.at[slot]))

    @pl.when(i == 0)
    def _():
        _start_rows(lambda r: copies(i, 0, r), tile)

    nxt = jnp.minimum(i + 1, n_steps - 1)

    def tile_step(slot):
        _wait_rows(lambda r: copies(i, slot, r), tile)
        for r in range(tile):
            for cp in copies(nxt, 1 - slot, r):
                cp.start(priority=r % N_DMA_THREADS)
        first, second = bufs[slot]
        wc = wc_ref[...]
        w1, w2 = wc[:, 0:1], wc[:, 1:2]
        f_lo, f_hi = _unpack_halves(first[...])
        s_lo, s_hi = _unpack_halves(second[...])
        ffn = jnp.concatenate([w1 * f_lo + w2 * s_lo, w1 * f_hi + w2 * s_hi], axis=1)
        y = _layer_norm_rows(alpha * h_ref[...] + ffn, g_ref[...], b_ref[...])
        o32_ref[...] = y
        o16_ref[...] = y.astype(BF16)

        @pl.when(i == n_steps - 1)
        def _():
            _wait_rows(lambda r: copies(nxt, 1 - slot, r), tile)

    for slot in range(2):
        pl.when(i % 2 == slot)(functools.partial(tile_step, slot))


def _combine(d1, d2, h32, wcol, g, b, y, alpha):
    n, d = h32.shape
    tile = min(GATHER_TILE, n)
    row = pl.BlockSpec((tile, d), lambda i, a, c: (i, 0))
    vec = pl.BlockSpec((1, d), lambda i, a, c: (0, 0))
    return pl.pallas_call(
        functools.partial(_combine_kernel, tile=tile, alpha=alpha, n_steps=n // tile),
        grid_spec=pltpu.PrefetchScalarGridSpec(
            num_scalar_prefetch=2, grid=(n // tile,),
            in_specs=[row, pl.BlockSpec((tile, LANES), lambda i, a, c: (i, 0)), vec, vec,
                      pl.BlockSpec(memory_space=pl.ANY)],
            out_specs=[row, row],
            scratch_shapes=[pltpu.VMEM((tile, y.shape[1]), y.dtype)] * 4 + [pltpu.SemaphoreType.DMA((2,))]),
        out_shape=[jax.ShapeDtypeStruct((n, d), F32), jax.ShapeDtypeStruct((n, d), BF16)],
        compiler_params=_params("arbitrary"), name="combine_ln",
    )(d1, d2, h32, wcol, g.reshape(1, d), b.reshape(1, d), y)


def _rotary_tables(seq):
    half = RET_QK_DIM // 2
    inv_freq = ROPE_BASE ** (-jnp.arange(half, dtype=F32) / half)
    ang = jnp.arange(seq, dtype=jnp.int32).astype(F32)[:, None] * inv_freq[None, :]
    cos, sin = jnp.cos(ang), jnp.sin(ang)
    return jnp.concatenate([cos, cos], axis=1), jnp.concatenate([-sin, sin], axis=1)


def _col_scale(sizes_scales):
    return jnp.concatenate([jnp.full((size,), scale, F32) for size, scale in sizes_scales])


def kernel(x, ln_in_g, ln_in_b, w_in, b_forget, w_branch_fox, w_branch_ret, w_out, ln1_g, ln1_b,
           w_router_group, b_router_group, w_router_expert, b_router_expert, w_gate, w_up, w_down, ln2_g, ln2_b):
    batch, seq, d = x.shape
    n = batch * seq
    depth = w_in.shape[0]
    alpha = (2.0 * depth) ** 0.25
    fox_w = FOX_HEADS * FOX_HEAD_DIM
    ret_qk = RET_HEADS * RET_QK_DIM
    ret_v = RET_HEADS * RET_V_DIM
    d_expert = w_gate.shape[-1]
    o_ff = 3 * fox_w
    o_rq = o_ff + FOX_HEADS
    o_rv = o_rq + 2 * ret_qk
    o_rg = o_rv + ret_v
    o_gate = o_rg + ret_v
    tables = _rotary_tables(seq)
    fox_scale = _col_scale([(fox_w, FOX_HEAD_DIM ** -0.5 * LOG2E), (2 * fox_w, 1.0)])
    rqk_scale = _col_scale([(ret_qk, 1.0), (ret_qk, RET_QK_DIM ** -0.5)])
    ones_v = jnp.ones((ret_v,), F32)
    ones_g = jnp.ones((2 * d,), F32)
    n_tiles = (N_EXPERTS * (EXPERT_TILE - 1) + 2 * n) // EXPERT_TILE + 1
    n_slots = n_tiles * EXPERT_TILE

    h32, h16 = _layer_norm(x.reshape(n, d), ln_in_g, ln_in_b)
    for l in range(depth):
        w = w_in[l]
        qkv = _proj(h16, w[:, :o_ff].astype(BF16), fox_scale, "none", seq)
        aug = _forget_cumsum(h16, w[:, o_ff:o_rq], b_forget[l], batch, seq)
        rqk = _proj(h16, w[:, o_rq:o_rv].astype(BF16), rqk_scale, "rotary", seq, tables)
        rv = _proj(h16, w[:, o_rv:o_rg].astype(BF16), ones_v, "none", seq)
        rg = _proj(h16, w[:, o_rg:o_gate].astype(BF16), ones_v, "silu", seq)
        gates = _proj(h16, w[:, o_gate:].astype(BF16), ones_g, "sigmoid", seq)
        y_fox = _fox_attention(qkv, aug, batch, seq)
        y_ret = _retention(rqk, rv, rg, batch, seq)
        merged = _merge(y_fox, y_ret, w_branch_fox[l].astype(BF16), w_branch_ret[l].astype(BF16), gates)

        w_router = jnp.zeros((d, LANES), F32)
        w_router = w_router.at[:, :N_GROUPS].set(w_router_group[l])
        w_router = w_router.at[:, EXPERT_LANE0:EXPERT_LANE0 + N_EXPERTS].set(w_router_expert[l])
        b_router = jnp.zeros((1, LANES), F32)
        b_router = b_router.at[0, :N_GROUPS].set(b_router_group[l])
        b_router = b_router.at[0, EXPERT_LANE0:EXPERT_LANE0 + N_EXPERTS].set(b_router_expert[l])
        w_router_hi = lax.reduce_precision(w_router, exponent_bits=8, mantissa_bits=7)
        w_router_lo = w_router - w_router_hi
        h1, h1_packed, route, wcol, counts = _outproj_router(
            merged, w_out[l].astype(BF16), h32, ln1_g[l], ln1_b[l],
            jnp.concatenate([w_router_hi, w_router_lo], axis=1).astype(BF16), b_router, alpha)
        dest, tiles = _plan(route, counts, EXPERT_TILE, n_tiles)
        d1, d2 = dest[0], dest[1]
        tok = _invert(d1, d2, n_slots)
        y = _experts(tiles[0, :n_tiles], tiles[1, :1], tok, h1_packed,
                     w_gate.reshape(depth * N_EXPERTS, d, d_expert), w_up.reshape(depth * N_EXPERTS, d, d_expert),
                     w_down.reshape(depth * N_EXPERTS, d_expert, d), l, EXPERT_TILE)
        h32, h16 = _combine(d1, d2, h1, wcol, ln2_g[l], ln2_b[l], y, alpha)
    return h32.reshape(batch, seq, d)
```

```python
import functools

import jax
import jax.numpy as jnp
from jax import lax
from jax.experimental import pallas as pl
from jax.experimental.pallas import tpu as pltpu

F32 = jnp.float32
BF16 = jnp.bfloat16

CHUNK = 64
FOX_HEADS = 8
FOX_HEAD_DIM = 128
RET_HEADS = 8
RET_QK_DIM = 128
RET_V_DIM = 256
ROPE_BASE = 10000.0
N_GROUPS = 4
EXPERTS_PER_GROUP = 8
N_EXPERTS = N_GROUPS * EXPERTS_PER_GROUP
LOG2_EXPERTS_PER_GROUP = EXPERTS_PER_GROUP.bit_length() - 1
LN_EPS = 1e-5
RMS_EPS = 1e-6

LANES = 128
EXPERT_LANE0 = N_GROUPS
VMEM_LIMIT = 56 * 1024 * 1024

ROW_TILE = 512
MATMUL_ROWS = 1024
MERGE_ROWS = 2048
DMA_UNROLL = 32
N_DMA_THREADS = 2
GATHER_AHEAD = 2
LOG2E = 1.4426950408889634
ATTN_TILE = 256
RET_TILE = 256
EXPERT_TILE = 256
GATHER_TILE = 256


def _params(*sem):
    return pltpu.CompilerParams(dimension_semantics=sem, vmem_limit_bytes=VMEM_LIMIT)


def _pack_halves(x):
    w = x.shape[1] // 2
    lo = lax.bitcast_convert_type(x[:, :w].astype(BF16).astype(F32), jnp.uint32)
    hi = lax.bitcast_convert_type(x[:, w:].astype(BF16).astype(F32), jnp.uint32)
    return (lo >> 16) | hi


def _unpack_halves(u):
    lo = lax.bitcast_convert_type(u << 16, F32)
    hi = lax.bitcast_convert_type(u & jnp.uint32(0xFFFF0000), F32)
    return lo, hi


def _layer_norm_rows(x, g, b):
    mu = jnp.mean(x, axis=-1, keepdims=True)
    xc = x - mu
    var = jnp.mean(xc * xc, axis=-1, keepdims=True)
    return xc * lax.rsqrt(var + LN_EPS) * g + b


def _ln_kernel(x_ref, g_ref, b_ref, o32_ref, o16_ref):
    y = _layer_norm_rows(x_ref[...], g_ref[...], b_ref[...])
    o32_ref[...] = y
    o16_ref[...] = y.astype(BF16)


def _layer_norm(x, g, b):
    n, d = x.shape
    tm = min(ROW_TILE, n)
    row = pl.BlockSpec((tm, d), lambda i: (i, 0))
    vec = pl.BlockSpec((1, d), lambda i: (0, 0))
    return pl.pallas_call(
        _ln_kernel, grid=(n // tm,), in_specs=[row, vec, vec], out_specs=[row, row],
        out_shape=[jax.ShapeDtypeStruct((n, d), F32), jax.ShapeDtypeStruct((n, d), BF16)],
        compiler_params=_params("parallel"), name="ln_in",
    )(x, g.reshape(1, d), b.reshape(1, d))


def _proj_kernel(x_ref, w_ref, s_ref, *rest, act):
    acc = jnp.dot(x_ref[...], w_ref[...], preferred_element_type=F32) * s_ref[...]
    if act == "rotary":
        cos_ref, sin_ref, o_ref = rest
        cos, sin = cos_ref[...], sin_ref[...]
        for c in range(acc.shape[1] // LANES):
            blk = acc[:, c * LANES:(c + 1) * LANES]
            rot = blk * cos + pltpu.roll(blk, LANES // 2, 1) * sin
            o_ref[:, c * LANES:(c + 1) * LANES] = rot.astype(o_ref.dtype)
        return
    (o_ref,) = rest
    if act == "silu":
        acc = acc * jax.nn.sigmoid(acc)
    elif act == "sigmoid":
        acc = jax.nn.sigmoid(acc)
    o_ref[...] = acc.astype(o_ref.dtype)


def _proj(x16, w16, colscale, act, seq, tables=None, tn=1024):
    n, k = x16.shape
    m = w16.shape[1]
    tm = min(MATMUL_ROWS, seq)
    in_specs = [pl.BlockSpec((tm, k), lambda i, j: (i, 0)),
                pl.BlockSpec((k, tn), lambda i, j: (0, j)),
                pl.BlockSpec((1, tn), lambda i, j: (0, j))]
    args = [x16, w16, colscale.reshape(1, m)]
    if act == "rotary":
        per_seq = seq // tm
        tab = pl.BlockSpec((tm, LANES), lambda i, j: (i % per_seq, 0))
        in_specs += [tab, tab]
        args += list(tables)
    return pl.pallas_call(
        functools.partial(_proj_kernel, act=act), grid=(n // tm, m // tn),
        in_specs=in_specs, out_specs=pl.BlockSpec((tm, tn), lambda i, j: (i, j)),
        out_shape=jax.ShapeDtypeStruct((n, m), BF16),
        compiler_params=_params("parallel", "parallel"), name="proj_" + act,
    )(*args)


def _forget_kernel(h_ref, w_ref, b_ref, o_ref, *, heads, block):
    z = jnp.dot(h_ref[...], w_ref[...], preferred_element_type=F32) + b_ref[...]
    lf = (jnp.minimum(z, 0.0) - jnp.log1p(jnp.exp(-jnp.abs(z)))) * LOG2E
    seq = lf.shape[0]
    r_i = lax.broadcasted_iota(jnp.int32, (block, block), 0)
    c_i = lax.broadcasted_iota(jnp.int32, (block, block), 1)
    upto = (c_i <= r_i).astype(BF16)
    lane = lax.broadcasted_iota(jnp.int32, (block, LANES), 1)
    carry = jnp.zeros((1, LANES), F32)
    for blk in range(seq // block):
        rows = slice(blk * block, (blk + 1) * block)
        part = lf[rows]
        cum = carry
        for _ in range(N_SPLIT):
            piece = part.astype(BF16)
            cum = cum + jnp.dot(upto, piece, preferred_element_type=F32)
            part = part - piece.astype(F32)
        carry = cum[block - 1:block, :]
        for h in range(heads):
            col = jnp.broadcast_to(cum[:, h:h + 1], (block, LANES))
            hi = col.astype(BF16).astype(F32)
            mid = (col - hi).astype(BF16).astype(F32)
            lo = col - hi - mid
            split = jnp.where(lane == 0, hi, jnp.where(lane == 1, mid, jnp.where(lane == 2, lo, 0.0)))
            o_ref[rows, h * LANES:(h + 1) * LANES] = split.astype(BF16)


def _forget_cumsum(h16, w_forget, b_forget, batch, seq):
    d, heads = w_forget.shape
    w_pad = jnp.zeros((d, LANES), BF16).at[:, :heads].set(w_forget.astype(BF16))
    b_pad = jnp.zeros((1, LANES), F32).at[0, :heads].set(b_forget.astype(F32))
    return pl.pallas_call(
        functools.partial(_forget_kernel, heads=heads, block=min(256, seq)), grid=(batch,),
        in_specs=[pl.BlockSpec((seq, d), lambda b: (b, 0)),
                  pl.BlockSpec((d, LANES), lambda b: (0, 0)),
                  pl.BlockSpec((1, LANES), lambda b: (0, 0))],
        out_specs=pl.BlockSpec((seq, heads * LANES), lambda b: (b, 0)),
        out_shape=jax.ShapeDtypeStruct((batch * seq, heads * LANES), BF16),
        compiler_params=_params("parallel"), name="forget_cumsum",
    )(h16, w_pad, b_pad)


N_SPLIT = 3


def _fox_kernel(q_ref, k_ref, v_ref, a_ref, o_ref, acc_sc, kaug_sc, *, tile, hp):
    qi = pl.program_id(2)
    dh = FOX_HEAD_DIM

    @pl.when(qi == 0)
    def _():
        for hh in range(hp):
            kaug_sc[:, 2 * hh * dh:(2 * hh + 1) * dh] = k_ref[:, hh * dh:(hh + 1) * dh]
            kaug_sc[:, (2 * hh + 1) * dh:(2 * hh + 2) * dh] = a_ref[:, hh * dh:(hh + 1) * dh]

    lane = lax.broadcasted_iota(jnp.int32, (tile, dh), 1)
    q_extra = jnp.where(lane < N_SPLIT, -1.0, 0.0).astype(BF16)
    q_aug = [jnp.concatenate([q_ref[:, hh * dh:(hh + 1) * dh], q_extra], axis=1) for hh in range(hp)]
    acc_sc[...] = jnp.zeros(acc_sc.shape, F32)

    def block(j, stats, diagonal):
        rows = pl.ds(pl.multiple_of(j * tile, tile), tile)
        s_t = []
        for hh in range(hp):
            s = lax.dot_general(kaug_sc[rows, 2 * hh * dh:(2 * hh + 2) * dh], q_aug[hh],
                                (((1,), (1,)), ((), ())), preferred_element_type=F32)
            if diagonal:
                key = lax.broadcasted_iota(jnp.int32, s.shape, 0)
                qry = lax.broadcasted_iota(jnp.int32, s.shape, 1)
                s = jnp.where(key <= qry, s, -jnp.inf)
            s_t.append(s)
        out, scale, pv = [], [], []
        for hh in range(hp):
            m_prev, l_prev = stats[hh]
            m_new = jnp.maximum(m_prev, jnp.max(s_t[hh], axis=0, keepdims=True))
            a = jnp.exp2(m_prev - m_new)
            p = jnp.exp2(s_t[hh] - m_new)
            out.append((m_new, a * l_prev + jnp.sum(p, axis=0, keepdims=True)))
            scale.append(a)
            pv.append(lax.dot_general(v_ref[rows, hh * dh:(hh + 1) * dh], p.astype(BF16),
                                      (((0,), (0,)), ((), ())), preferred_element_type=F32))
        for hh in range(hp):
            acc_sc[hh] = scale[hh] * acc_sc[hh] + pv[hh]
        return tuple(out)

    init = tuple((jnp.full((1, tile), -jnp.inf, F32), jnp.zeros((1, tile), F32)) for _ in range(hp))
    stats = lax.fori_loop(0, qi, lambda j, st: block(j, st, False), init)
    stats = block(qi, stats, True)
    for hh in range(hp):
        o_ref[:, hh * dh:(hh + 1) * dh] = (acc_sc[hh] / stats[hh][1]).T.astype(o_ref.dtype)


def _fox_attention(qkv16, aug16, batch, seq, hp=8):
    heads, dh = FOX_HEADS, FOX_HEAD_DIM
    tile = min(ATTN_TILE, seq)
    nq = seq // tile
    groups = heads // hp
    w = hp * dh
    return pl.pallas_call(
        functools.partial(_fox_kernel, tile=tile, hp=hp), grid=(batch, groups, nq),
        in_specs=[pl.BlockSpec((tile, w), lambda b, g, i: (b * nq + i, g)),
                  pl.BlockSpec((seq, w), lambda b, g, i: (b, groups + g)),
                  pl.BlockSpec((seq, w), lambda b, g, i: (b, 2 * groups + g)),
                  pl.BlockSpec((seq, w), lambda b, g, i: (b, g))],
        out_specs=pl.BlockSpec((tile, w), lambda b, g, i: (b * nq + i, g)),
        out_shape=jax.ShapeDtypeStruct((batch * seq, heads * dh), BF16),
        scratch_shapes=[pltpu.VMEM((hp, dh, tile), F32), pltpu.VMEM((seq, 2 * w), BF16)],
        compiler_params=_params("parallel", "parallel", "arbitrary"), name="fox_attention",
    )(qkv16, qkv16, qkv16, aug16)


def _ret_kernel(q_ref, k_ref, v_ref, g_ref, dm_ref, qd_ref, kd_ref, cd_ref, o_ref, st_sc, *, hp):
    @pl.when(pl.program_id(2) == 0)
    def _():
        st_sc[...] = jnp.zeros(st_sc.shape, F32)

    dk, dv = RET_QK_DIM, RET_V_DIM
    for hh in range(hp):
        q = q_ref[:, hh * dk:(hh + 1) * dk]
        k = k_ref[:, hh * dk:(hh + 1) * dk]
        v = v_ref[:, hh * dv:(hh + 1) * dv]
        scores = lax.dot_general(q, k, (((1,), (1,)), ((), ())), preferred_element_type=F32) * dm_ref[hh]
        intra = jnp.dot(scores.astype(BF16), v, preferred_element_type=F32)
        state = st_sc[hh]
        cross = jnp.dot(q, state.astype(BF16), preferred_element_type=F32) * qd_ref[hh]
        k_dec = (k.astype(F32) * kd_ref[hh]).astype(BF16)
        st_sc[hh] = state * cd_ref[hh] + lax.dot_general(
            k_dec, v, (((0,), (0,)), ((), ())), preferred_element_type=F32)
        o = intra + cross
        o = o * lax.rsqrt(jnp.mean(o * o, axis=-1, keepdims=True) + RMS_EPS)
        o_ref[:, hh * dv:(hh + 1) * dv] = (g_ref[:, hh * dv:(hh + 1) * dv].astype(F32) * o).astype(o_ref.dtype)


def _retention_tables(tile):
    heads = RET_HEADS
    log_gamma = jnp.log1p(-(2.0 ** (-5.0 - jnp.arange(heads, dtype=F32))))
    pos = jnp.arange(tile, dtype=F32)
    chunk = jnp.arange(tile, dtype=jnp.int32) // CHUNK
    dist = pos[:, None] - pos[None, :]
    same = chunk[:, None] == chunk[None, :]
    earlier = chunk[None, :] < chunk[:, None]
    expo = jnp.where(same, jnp.abs(dist), dist)
    dmask = jnp.where((same | earlier)[None], jnp.exp(log_gamma[:, None, None] * expo[None]), 0.0)
    q_dec = jnp.broadcast_to(jnp.exp(log_gamma[:, None] * (pos + 1.0))[:, :, None], (heads, tile, RET_V_DIM))
    k_dec = jnp.broadcast_to(jnp.exp(log_gamma[:, None] * (tile - 1 - pos))[:, :, None], (heads, tile, RET_QK_DIM))
    c_dec = jnp.broadcast_to(jnp.exp(log_gamma * tile)[:, None, None], (heads, 1, RET_V_DIM))
    return dmask, q_dec, k_dec, c_dec


def _retention(rqk16, rv16, rg16, batch, seq, hp=4):
    heads, dk, dv = RET_HEADS, RET_QK_DIM, RET_V_DIM
    tile = min(RET_TILE, seq)
    nt = seq // tile
    groups = heads // hp
    dmask, q_dec, k_dec, c_dec = _retention_tables(tile)
    val = pl.BlockSpec((tile, hp * dv), lambda b, g, t: (b * nt + t, g))
    return pl.pallas_call(
        functools.partial(_ret_kernel, hp=hp), grid=(batch, groups, nt),
        in_specs=[pl.BlockSpec((tile, hp * dk), lambda b, g, t: (b * nt + t, g)),
                  pl.BlockSpec((tile, hp * dk), lambda b, g, t: (b * nt + t, groups + g)),
                  val, val,
                  pl.BlockSpec((hp, tile, tile), lambda b, g, t: (g, 0, 0)),
                  pl.BlockSpec((hp, tile, dv), lambda b, g, t: (g, 0, 0)),
                  pl.BlockSpec((hp, tile, dk), lambda b, g, t: (g, 0, 0)),
                  pl.BlockSpec((hp, 1, dv), lambda b, g, t: (g, 0, 0))],
        out_specs=val,
        out_shape=jax.ShapeDtypeStruct((batch * seq, heads * dv), BF16),
        scratch_shapes=[pltpu.VMEM((hp, dk, dv), F32)],
        compiler_params=_params("parallel", "parallel", "arbitrary"), name="retention",
    )(rqk16, rqk16, rv16, rg16, dmask, q_dec, k_dec, c_dec)


def _merge_kernel(yf_ref, yr_ref, wf_ref, wr_ref, gf_ref, gr_ref, o_ref):
    fox = jnp.dot(yf_ref[...], wf_ref[...], preferred_element_type=F32)
    ret = jnp.dot(yr_ref[...], wr_ref[...], preferred_element_type=F32)
    o_ref[...] = (gf_ref[...].astype(F32) * fox + gr_ref[...].astype(F32) * ret).astype(o_ref.dtype)


def _merge(y_fox, y_ret, w_fox16, w_ret16, gates16, tn=512):
    n, kf = y_fox.shape
    kr = y_ret.shape[1]
    d = w_fox16.shape[1]
    tm = min(MERGE_ROWS, n)
    nj = d // tn
    return pl.pallas_call(
        _merge_kernel, grid=(n // tm, nj),
        in_specs=[pl.BlockSpec((tm, kf), lambda i, j: (i, 0)),
                  pl.BlockSpec((tm, kr), lambda i, j: (i, 0)),
                  pl.BlockSpec((kf, tn), lambda i, j: (0, j)),
                  pl.BlockSpec((kr, tn), lambda i, j: (0, j)),
                  pl.BlockSpec((tm, tn), lambda i, j: (i, j)),
                  pl.BlockSpec((tm, tn), lambda i, j: (i, nj + j))],
        out_specs=pl.BlockSpec((tm, tn), lambda i, j: (i, j)),
        out_shape=jax.ShapeDtypeStruct((n, d), BF16),
        compiler_params=_params("parallel", "parallel"), name="merge",
    )(y_fox, y_ret, w_fox16, w_ret16, gates16, gates16)


def _lane_pick(mask, values):
    return jnp.sum(jnp.where(mask, values, 0.0), axis=1, keepdims=True)


def _outproj_router_kernel(m_ref, w_ref, h_ref, g_ref, b_ref, wr_ref, br_ref,
                           h1_ref, h1p_ref, route_ref, wcol_ref, cnt_ref, carry_sc, *, alpha):
    i = pl.program_id(0)

    @pl.when(i == 0)
    def _():
        carry_sc[...] = jnp.zeros(carry_sc.shape, F32)

    mix = jnp.dot(m_ref[...], w_ref[...], preferred_element_type=F32)
    hn = _layer_norm_rows(alpha * h_ref[...] + mix, g_ref[...], b_ref[...])
    h1_ref[...] = hn
    h1p_ref[...] = _pack_halves(hn)

    hn_hi = hn.astype(BF16)
    hn_lo = (hn - hn_hi.astype(F32)).astype(BF16)
    part = jnp.dot(hn_hi, wr_ref[...], preferred_element_type=F32)
    logits = (part[:, :LANES] + part[:, LANES:] + br_ref[...]
              + jnp.dot(hn_lo, wr_ref[:, :LANES], preferred_element_type=F32))
    tm = logits.shape[0]
    lane = lax.broadcasted_iota(jnp.int32, logits.shape, 1)
    neg = -jnp.inf
    gl = jnp.where(lane < N_GROUPS, logits, neg)
    gmax = jnp.max(gl, axis=1, keepdims=True)
    grp_w = 1.0 / jnp.sum(jnp.exp(gl - gmax), axis=1, keepdims=True)
    gidx = jnp.min(jnp.where(gl == gmax, lane, LANES), axis=1, keepdims=True)
    e_lane = lane - EXPERT_LANE0
    in_grp = (e_lane >= 0) & (e_lane < N_EXPERTS) & ((e_lane >> LOG2_EXPERTS_PER_GROUP) == gidx)
    el = jnp.where(in_grp, logits, neg)
    max1 = jnp.max(el, axis=1, keepdims=True)
    i1 = jnp.min(jnp.where(el == max1, lane, LANES), axis=1, keepdims=True)
    el2 = jnp.where(lane == i1, neg, el)
    max2 = jnp.max(el2, axis=1, keepdims=True)
    i2 = jnp.min(jnp.where(el2 == max2, lane, LANES), axis=1, keepdims=True)
    ratio = jnp.exp(max2 - max1)
    w1 = grp_w / (1.0 + ratio)
    w2 = grp_w * ratio / (1.0 + ratio)
    sel1 = lane == i1
    sel2 = lane == i2
    sel = sel1.astype(F32) + sel2.astype(F32)
    r_i = lax.broadcasted_iota(jnp.int32, (tm, tm), 0)
    c_i = lax.broadcasted_iota(jnp.int32, (tm, tm), 1)
    before = (c_i < r_i).astype(BF16)
    rank = carry_sc[...] + jnp.dot(before, sel.astype(BF16), preferred_element_type=F32)
    new_carry = carry_sc[...] + jnp.sum(sel, axis=0, keepdims=True)
    carry_sc[...] = new_carry
    cnt_ref[...] = jnp.broadcast_to(new_carry, cnt_ref.shape)
    r1 = _lane_pick(sel1, rank)
    r2 = _lane_pick(sel2, rank)
    e1 = (i1 - EXPERT_LANE0).astype(F32)
    e2 = (i2 - EXPERT_LANE0).astype(F32)
    rec = jnp.where(lane == 0, e1, jnp.where(lane == 1, e2, jnp.where(lane == 2, r1, jnp.where(lane == 3, r2, 0.0))))
    route_ref[...] = rec.T[0:8, :]
    wcol_ref[...] = jnp.where(lane == 0, w1, jnp.where(lane == 1, w2, 0.0))


def _outproj_router(merged16, w_out16, h32, g, b, w_router, b_router, alpha):
    n, d = merged16.shape
    tm = min(ROW_TILE, n)
    row16 = pl.BlockSpec((tm, d), lambda i: (i, 0))
    vec = pl.BlockSpec((1, d), lambda i: (0, 0))
    return pl.pallas_call(
        functools.partial(_outproj_router_kernel, alpha=alpha), grid=(n // tm,),
        in_specs=[row16, pl.BlockSpec((d, d), lambda i: (0, 0)), row16, vec, vec,
                  pl.BlockSpec((d, 2 * LANES), lambda i: (0, 0)), pl.BlockSpec((1, LANES), lambda i: (0, 0))],
        out_specs=[row16, pl.BlockSpec((tm, d // 2), lambda i: (i, 0)), pl.BlockSpec((8, tm), lambda i: (0, i)),
                   pl.BlockSpec((tm, LANES), lambda i: (i, 0)), pl.BlockSpec((8, LANES), lambda i: (0, 0))],
        out_shape=[jax.ShapeDtypeStruct((n, d), F32), jax.ShapeDtypeStruct((n, d // 2), jnp.uint32),
                   jax.ShapeDtypeStruct((8, n), F32),
                   jax.ShapeDtypeStruct((n, LANES), F32), jax.ShapeDtypeStruct((8, LANES), F32)],
        scratch_shapes=[pltpu.VMEM((1, LANES), F32)],
        compiler_params=_params("arbitrary"), name="outproj_router",
    )(merged16, w_out16, h32, g.reshape(1, d), b.reshape(1, d), w_router, b_router)


def _plan_kernel(route_ref, cnt_ref, dest_ref, tiles_ref, *, tile, log2_tile):
    cnt = cnt_ref[...].astype(jnp.int32)
    padded = ((cnt + (tile - 1)) >> log2_tile) << log2_tile
    lane = lax.broadcasted_iota(jnp.int32, padded.shape, 1)
    end = padded
    k = 1
    while k < LANES:
        end = end + jnp.where(lane >= k, pltpu.roll(end, k, 1), 0)
        k *= 2
    start = end - padded
    rows = EXPERT_LANE0 + N_EXPERTS
    rows = -(-rows // 8) * 8
    def lane_to_sublane(v):
        return jnp.broadcast_to(v[0:1, :].astype(F32), (LANES, LANES)).T[0:rows, 0:1]

    start_col = lane_to_sublane(start)
    end_col = lane_to_sublane(end)
    route = route_ref[...]
    n = route.shape[1]
    sub = lax.broadcasted_iota(jnp.int32, (rows, n), 0)
    e1 = route[0:1, :].astype(jnp.int32) + EXPERT_LANE0
    e2 = route[1:2, :].astype(jnp.int32) + EXPERT_LANE0
    d1 = jnp.sum(jnp.where(sub == e1, start_col, 0.0), axis=0, keepdims=True) + route[2:3, :]
    d2 = jnp.sum(jnp.where(sub == e2, start_col, 0.0), axis=0, keepdims=True) + route[3:4, :]
    row8 = lax.broadcasted_iota(jnp.int32, dest_ref.shape, 0)
    dest_ref[...] = jnp.where(row8 == 0, d1, jnp.where(row8 == 1, d2, 0.0)).astype(jnp.int32)
    nt = tiles_ref.shape[1]
    t_start = (lax.broadcasted_iota(jnp.int32, (rows, nt), 1) * tile).astype(F32)
    sub_t = lax.broadcasted_iota(jnp.int32, (rows, nt), 0)
    is_expert = (sub_t >= EXPERT_LANE0) & (sub_t < EXPERT_LANE0 + N_EXPERTS)
    done = jnp.sum(jnp.where(is_expert & (end_col <= t_start), 1.0, 0.0), axis=0, keepdims=True)
    expert = jnp.minimum(done, float(N_EXPERTS - 1))
    used = jnp.max(end_col, axis=0, keepdims=True) * (1.0 / tile)
    row8t = lax.broadcasted_iota(jnp.int32, tiles_ref.shape, 0)
    tiles_ref[...] = jnp.where(row8t == 0, expert, jnp.broadcast_to(used, tiles_ref.shape)).astype(jnp.int32)


def _plan(route, counts, tile, n_tiles):
    n = route.shape[1]
    nt_pad = -(-n_tiles // LANES) * LANES
    return pl.pallas_call(
        functools.partial(_plan_kernel, tile=tile, log2_tile=tile.bit_length() - 1),
        out_shape=[jax.ShapeDtypeStruct((8, n), jnp.int32), jax.ShapeDtypeStruct((8, nt_pad), jnp.int32)],
        compiler_params=pltpu.CompilerParams(vmem_limit_bytes=VMEM_LIMIT), name="plan",
    )(route, counts)


def _start_rows(copies, n_rows, thread=None):
    def start(o, c):
        for u in range(DMA_UNROLL):
            for cp in copies(o * DMA_UNROLL + u):
                cp.start(priority=u % N_DMA_THREADS if thread is None else thread)
        return c

    lax.fori_loop(0, n_rows // DMA_UNROLL, start, 0)


def _wait_rows(copies, n_rows):
    def wait(o, c):
        for u in range(DMA_UNROLL):
            for cp in copies(o * DMA_UNROLL + u):
                cp.wait()
        return c

    lax.fori_loop(0, n_rows // DMA_UNROLL, wait, 0)


def _invert_kernel(d1_ref, d2_ref, tok_ref, *, unroll):
    def clear(o, c):
        for u in range(unroll):
            tok_ref[o * unroll + u] = 0
        return c

    def put(o, c):
        for u in range(unroll):
            n = o * unroll + u
            tok_ref[d1_ref[n]] = n
            tok_ref[d2_ref[n]] = n
        return c

    lax.fori_loop(0, tok_ref.shape[0] // unroll, clear, 0)
    lax.fori_loop(0, d1_ref.shape[0] // unroll, put, 0)


def _invert(d1, d2, n_slots):
    return pl.pallas_call(
        functools.partial(_invert_kernel, unroll=DMA_UNROLL),
        grid_spec=pltpu.PrefetchScalarGridSpec(
            num_scalar_prefetch=2, grid=(),
            in_specs=[], out_specs=pl.BlockSpec(memory_space=pltpu.SMEM)),
        out_shape=jax.ShapeDtypeStruct((n_slots,), jnp.int32), name="invert",
    )(d1, d2)


def _expert_kernel(te_ref, nu_ref, tok_ref, h_hbm, wg_hbm, wu_hbm, wd_hbm, y_ref,
                   xbuf0, xbuf1, xbuf2, wg_in, wu_in, wd_in, wg_sc, wu_sc, wd_sc, sem, wsem, *, tile, first):
    t = pl.program_id(0)
    n_used = nu_ref[0]
    bufs = (xbuf0, xbuf1, xbuf2)
    n_bufs = GATHER_AHEAD + 1

    def row_copy(tt, slot, r):
        src = h_hbm.at[pl.ds(tok_ref[tt * tile + r], 1), :]
        return pltpu.make_async_copy(src, bufs[slot].at[pl.ds(r, 1), :], sem.at[slot])

    def weight_copies(expert):
        return (pltpu.make_async_copy(wg_hbm.at[first + expert], wg_in, wsem.at[0]),
                pltpu.make_async_copy(wu_hbm.at[first + expert], wu_in, wsem.at[1]),
                pltpu.make_async_copy(wd_hbm.at[first + expert], wd_in, wsem.at[2]))

    @pl.when(t == 0)
    def _():
        for cp in weight_copies(te_ref[0]):
            cp.start()
        _start_rows(lambda r: (row_copy(0, 0, r),), tile)
        _start_rows(lambda r: (row_copy(jnp.minimum(1, n_used - 1), 1, r),), tile)

    @pl.when((t < n_used) & ((t == 0) | (te_ref[t] != te_ref[jnp.maximum(t - 1, 0)])))
    def _():
        for cp in weight_copies(te_ref[t]):
            cp.wait()
        wg_sc[...] = wg_in[...].astype(BF16)
        wu_sc[...] = wu_in[...].astype(BF16)
        wd_sc[...] = wd_in[...].astype(BF16)
        run_end = lax.while_loop(lambda j: (j < n_used) & (te_ref[jnp.minimum(j, n_used - 1)] == te_ref[t]),
                                 lambda j: j + 1, t + 1)

        @pl.when(run_end < n_used)
        def _():
            for cp in weight_copies(te_ref[run_end]):
                cp.start()

    ahead = jnp.minimum(t + GATHER_AHEAD, n_used - 1)

    def tile_step(slot):
        _wait_rows(lambda r: (row_copy(t, slot, r),), tile)
        for r in range(tile):
            row_copy(ahead, (slot + GATHER_AHEAD) % n_bufs, r).start(priority=r % N_DMA_THREADS)
        x_lo, x_hi = _unpack_halves(bufs[slot][...])
        x = jnp.concatenate([x_lo.astype(BF16), x_hi.astype(BF16)], axis=1)
        gate = jnp.dot(x, wg_sc[...], preferred_element_type=F32)
        up = jnp.dot(x, wu_sc[...], preferred_element_type=F32)
        hid = (gate * jax.nn.sigmoid(gate) * up).astype(BF16)
        y_ref[...] = _pack_halves(jnp.dot(hid, wd_sc[...], preferred_element_type=F32))

        @pl.when(t == n_used - 1)
        def _():
            for k in range(1, n_bufs):
                _wait_rows(lambda r: (row_copy(ahead, (slot + k) % n_bufs, r),), tile)

    for slot in range(n_bufs):
        pl.when((t < n_used) & (t % n_bufs == slot))(functools.partial(tile_step, slot))

    @pl.when(t >= n_used)
    def _():
        y_ref[...] = jnp.zeros(y_ref.shape, y_ref.dtype)


def _experts(tile_expert, n_used, tok, h_packed, w_gate, w_up, w_down, layer, tile):
    dp = h_packed.shape[1]
    d, f = w_gate.shape[1], w_gate.shape[2]
    p = tok.shape[0]
    hbm = pl.BlockSpec(memory_space=pl.ANY)
    return pl.pallas_call(
        functools.partial(_expert_kernel, tile=tile, first=layer * N_EXPERTS),
        grid_spec=pltpu.PrefetchScalarGridSpec(
            num_scalar_prefetch=3, grid=(p // tile,),
            in_specs=[hbm, hbm, hbm, hbm],
            out_specs=pl.BlockSpec((tile, dp), lambda t, te, nu, tk: (t, 0)),
            scratch_shapes=[pltpu.VMEM((tile, dp), jnp.uint32)] * (GATHER_AHEAD + 1) + [
                pltpu.VMEM((d, f), F32), pltpu.VMEM((d, f), F32), pltpu.VMEM((f, d), F32),
                pltpu.VMEM((d, f), BF16), pltpu.VMEM((d, f), BF16), pltpu.VMEM((f, d), BF16),
                pltpu.SemaphoreType.DMA((GATHER_AHEAD + 1,)), pltpu.SemaphoreType.DMA((3,))]),
        out_shape=jax.ShapeDtypeStruct((p, dp), jnp.uint32),
        compiler_params=_params("arbitrary"), name="experts",
    )(tile_expert, n_used, tok, h_packed, w_gate, w_up, w_down)


def _combine_kernel(d1_ref, d2_ref, h_ref, wc_ref, g_ref, b_ref, y_hbm, o32_ref, o16_ref,
                    a0, b0, a1, b1, sem, *, tile, alpha, n_steps):
    i = pl.program_id(0)
    bufs = ((a0, b0), (a1, b1))

    def copies(ii, slot, r):
        n = ii * tile + r
        first, second = bufs[slot]
        return (pltpu.make_async_copy(y_hbm.at[pl.ds(d1_ref[n], 1), :], first.at[pl.ds(r, 1), :], sem.at[slot]),
                pltpu.make_async_copy(y_hbm.at[pl.ds(d2_ref[n], 1), :], second.at[pl.ds(r, 1), :], sem.at[slot]))

    @pl.when(i == 0)
    def _():
        _start_rows(lambda r: copies(i, 0, r), tile)

    nxt = jnp.minimum(i + 1, n_steps - 1)

    def tile_step(slot):
        _wait_rows(lambda r: copies(i, slot, r), tile)
        for r in range(tile):
            for cp in copies(nxt, 1 - slot, r):
                cp.start(priority=r % N_DMA_THREADS)
        first, second = bufs[slot]
        wc = wc_ref[...]
        w1, w2 = wc[:, 0:1], wc[:, 1:2]
        f_lo, f_hi = _unpack_halves(first[...])
        s_lo, s_hi = _unpack_halves(second[...])
        ffn = jnp.concatenate([w1 * f_lo + w2 * s_lo, w1 * f_hi + w2 * s_hi], axis=1)
        y = _layer_norm_rows(alpha * h_ref[...] + ffn, g_ref[...], b_ref[...])
        o32_ref[...] = y
        o16_ref[...] = y.astype(BF16)

        @pl.when(i == n_steps - 1)
        def _():
            _wait_rows(lambda r: copies(nxt, 1 - slot, r), tile)

    for slot in range(2):
        pl.when(i % 2 == slot)(functools.partial(tile_step, slot))


def _combine(d1, d2, h32, wcol, g, b, y, alpha):
    n, d = h32.shape
    tile = min(GATHER_TILE, n)
    row = pl.BlockSpec((tile, d), lambda i, a, c: (i, 0))
    vec = pl.BlockSpec((1, d), lambda i, a, c: (0, 0))
    return pl.pallas_call(
        functools.partial(_combine_kernel, tile=tile, alpha=alpha, n_steps=n // tile),
        grid_spec=pltpu.PrefetchScalarGridSpec(
            num_scalar_prefetch=2, grid=(n // tile,),
            in_specs=[row, pl.BlockSpec((tile, LANES), lambda i, a, c: (i, 0)), vec, vec,
                      pl.BlockSpec(memory_space=pl.ANY)],
            out_specs=[row, row],
            scratch_shapes=[pltpu.VMEM((tile, y.shape[1]), y.dtype)] * 4 + [pltpu.SemaphoreType.DMA((2,))]),
        out_shape=[jax.ShapeDtypeStruct((n, d), F32), jax.ShapeDtypeStruct((n, d), BF16)],
        compiler_params=_params("arbitrary"), name="combine_ln",
    )(d1, d2, h32, wcol, g.reshape(1, d), b.reshape(1, d), y)


def _rotary_tables(seq):
    half = RET_QK_DIM // 2
    inv_freq = ROPE_BASE ** (-jnp.arange(half, dtype=F32) / half)
    ang = jnp.arange(seq, dtype=jnp.int32).astype(F32)[:, None] * inv_freq[None, :]
    cos, sin = jnp.cos(ang), jnp.sin(ang)
    return jnp.concatenate([cos, cos], axis=1), jnp.concatenate([-sin, sin], axis=1)


def _col_scale(sizes_scales):
    return jnp.concatenate([jnp.full((size,), scale, F32) for size, scale in sizes_scales])


def kernel(x, ln_in_g, ln_in_b, w_in, b_forget, w_branch_fox, w_branch_ret, w_out, ln1_g, ln1_b,
           w_router_group, b_router_group, w_router_expert, b_router_expert, w_gate, w_up, w_down, ln2_g, ln2_b):
    batch, seq, d = x.shape
    n = batch * seq
    depth = w_in.shape[0]
    alpha = (2.0 * depth) ** 0.25
    fox_w = FOX_HEADS * FOX_HEAD_DIM
    ret_qk = RET_HEADS * RET_QK_DIM
    ret_v = RET_HEADS * RET_V_DIM
    d_expert = w_gate.shape[-1]
    o_ff = 3 * fox_w
    o_rq = o_ff + FOX_HEADS
    o_rv = o_rq + 2 * ret_qk
    o_rg = o_rv + ret_v
    o_gate = o_rg + ret_v
    tables = _rotary_tables(seq)
    fox_scale = _col_scale([(fox_w, FOX_HEAD_DIM ** -0.5 * LOG2E), (2 * fox_w, 1.0)])
    rqk_scale = _col_scale([(ret_qk, 1.0), (ret_qk, RET_QK_DIM ** -0.5)])
    ones_v = jnp.ones((ret_v,), F32)
    ones_g = jnp.ones((2 * d,), F32)
    n_tiles = (N_EXPERTS * (EXPERT_TILE - 1) + 2 * n) // EXPERT_TILE + 1
    n_slots = n_tiles * EXPERT_TILE

    h32, h16 = _layer_norm(x.reshape(n, d), ln_in_g, ln_in_b)
    for l in range(depth):
        w = w_in[l]
        qkv = _proj(h16, w[:, :o_ff].astype(BF16), fox_scale, "none", seq)
        aug = _forget_cumsum(h16, w[:, o_ff:o_rq], b_forget[l], batch, seq)
        rqk = _proj(h16, w[:, o_rq:o_rv].astype(BF16), rqk_scale, "rotary", seq, tables)
        rv = _proj(h16, w[:, o_rv:o_rg].astype(BF16), ones_v, "none", seq)
        rg = _proj(h16, w[:, o_rg:o_gate].astype(BF16), ones_v, "silu", seq)
        gates = _proj(h16, w[:, o_gate:].astype(BF16), ones_g, "sigmoid", seq)
        y_fox = _fox_attention(qkv, aug, batch, seq)
        y_ret = _retention(rqk, rv, rg, batch, seq)
        merged = _merge(y_fox, y_ret, w_branch_fox[l].astype(BF16), w_branch_ret[l].astype(BF16), gates)

        w_router = jnp.zeros((d, LANES), F32)
        w_router = w_router.at[:, :N_GROUPS].set(w_router_group[l])
        w_router = w_router.at[:, EXPERT_LANE0:EXPERT_LANE0 + N_EXPERTS].set(w_router_expert[l])
        b_router = jnp.zeros((1, LANES), F32)
        b_router = b_router.at[0, :N_GROUPS].set(b_router_group[l])
        b_router = b_router.at[0, EXPERT_LANE0:EXPERT_LANE0 + N_EXPERTS].set(b_router_expert[l])
        w_router_hi = lax.reduce_precision(w_router, exponent_bits=8, mantissa_bits=7)
        w_router_lo = w_router - w_router_hi
        h1, h1_packed, route, wcol, counts = _outproj_router(
            merged, w_out[l].astype(BF16), h32, ln1_g[l], ln1_b[l],
            jnp.concatenate([w_router_hi, w_router_lo], axis=1).astype(BF16), b_router, alpha)
        dest, tiles = _plan(route, counts, EXPERT_TILE, n_tiles)
        d1, d2 = dest[0], dest[1]
        tok = _invert(d1, d2, n_slots)
        y = _experts(tiles[0, :n_tiles], tiles[1, :1], tok, h1_packed,
                     w_gate.reshape(depth * N_EXPERTS, d, d_expert), w_up.reshape(depth * N_EXPERTS, d, d_expert),
                     w_down.reshape(depth * N_EXPERTS, d_expert, d), l, EXPERT_TILE)
        h32, h16 = _combine(d1, d2, h1, wcol, ln2_g[l], ln2_b[l], y, alpha)
    return h32.reshape(batch, seq, d)
```

```python
import functools

import jax
import jax.numpy as jnp
from jax import lax
from jax.experimental import pallas as pl
from jax.experimental.pallas import tpu as pltpu

F32 = jnp.float32
BF16 = jnp.bfloat16

CHUNK = 64
FOX_HEADS = 8
FOX_HEAD_DIM = 128
RET_HEADS = 8
RET_QK_DIM = 128
RET_V_DIM = 256
ROPE_BASE = 10000.0
N_GROUPS = 4
EXPERTS_PER_GROUP = 8
N_EXPERTS = N_GROUPS * EXPERTS_PER_GROUP
LOG2_EXPERTS_PER_GROUP = EXPERTS_PER_GROUP.bit_length() - 1
LN_EPS = 1e-5
RMS_EPS = 1e-6

LANES = 128
EXPERT_LANE0 = N_GROUPS
VMEM_LIMIT = 56 * 1024 * 1024

ROW_TILE = 512
MATMUL_ROWS = 1024
MERGE_ROWS = 2048
DMA_UNROLL = 32
N_DMA_THREADS = 2
GATHER_AHEAD = 3
LOG2E = 1.4426950408889634
ATTN_TILE = 256
RET_TILE = 256
EXPERT_TILE = 256
GATHER_TILE = 256


def _params(*sem):
    return pltpu.CompilerParams(dimension_semantics=sem, vmem_limit_bytes=VMEM_LIMIT)


def _pack_halves(x):
    w = x.shape[1] // 2
    lo = lax.bitcast_convert_type(x[:, :w].astype(BF16).astype(F32), jnp.uint32)
    hi = lax.bitcast_convert_type(x[:, w:].astype(BF16).astype(F32), jnp.uint32)
    return (lo >> 16) | hi


def _unpack_halves(u):
    lo = lax.bitcast_convert_type(u << 16, F32)
    hi = lax.bitcast_convert_type(u & jnp.uint32(0xFFFF0000), F32)
    return lo, hi


def _layer_norm_rows(x, g, b):
    mu = jnp.mean(x, axis=-1, keepdims=True)
    xc = x - mu
    var = jnp.mean(xc * xc, axis=-1, keepdims=True)
    return xc * lax.rsqrt(var + LN_EPS) * g + b


def _ln_kernel(x_ref, g_ref, b_ref, o32_ref, o16_ref):
    y = _layer_norm_rows(x_ref[...], g_ref[...], b_ref[...])
    o32_ref[...] = y
    o16_ref[...] = y.astype(BF16)


def _layer_norm(x, g, b):
    n, d = x.shape
    tm = min(ROW_TILE, n)
    row = pl.BlockSpec((tm, d), lambda i: (i, 0))
    vec = pl.BlockSpec((1, d), lambda i: (0, 0))
    return pl.pallas_call(
        _ln_kernel, grid=(n // tm,), in_specs=[row, vec, vec], out_specs=[row, row],
        out_shape=[jax.ShapeDtypeStruct((n, d), F32), jax.ShapeDtypeStruct((n, d), BF16)],
        compiler_params=_params("parallel"), name="ln_in",
    )(x, g.reshape(1, d), b.reshape(1, d))


def _proj_kernel(x_ref, w_ref, s_ref, *rest, act):
    acc = jnp.dot(x_ref[...], w_ref[...], preferred_element_type=F32) * s_ref[...]
    if act == "rotary":
        cos_ref, sin_ref, o_ref = rest
        cos, sin = cos_ref[...], sin_ref[...]
        for c in range(acc.shape[1] // LANES):
            blk = acc[:, c * LANES:(c + 1) * LANES]
            rot = blk * cos + pltpu.roll(blk, LANES // 2, 1) * sin
            o_ref[:, c * LANES:(c + 1) * LANES] = rot.astype(o_ref.dtype)
        return
    (o_ref,) = rest
    if act == "silu":
        acc = acc * jax.nn.sigmoid(acc)
    elif act == "sigmoid":
        acc = jax.nn.sigmoid(acc)
    o_ref[...] = acc.astype(o_ref.dtype)


def _proj(x16, w16, colscale, act, seq, tables=None, tn=1024):
    n, k = x16.shape
    m = w16.shape[1]
    tm = min(MATMUL_ROWS, seq)
    in_specs = [pl.BlockSpec((tm, k), lambda i, j: (i, 0)),
                pl.BlockSpec((k, tn), lambda i, j: (0, j)),
                pl.BlockSpec((1, tn), lambda i, j: (0, j))]
    args = [x16, w16, colscale.reshape(1, m)]
    if act == "rotary":
        per_seq = seq // tm
        tab = pl.BlockSpec((tm, LANES), lambda i, j: (i % per_seq, 0))
        in_specs += [tab, tab]
        args += list(tables)
    return pl.pallas_call(
        functools.partial(_proj_kernel, act=act), grid=(n // tm, m // tn),
        in_specs=in_specs, out_specs=pl.BlockSpec((tm, tn), lambda i, j: (i, j)),
        out_shape=jax.ShapeDtypeStruct((n, m), BF16),
        compiler_params=_params("parallel", "parallel"), name="proj_" + act,
    )(*args)


def _forget_kernel(h_ref, w_ref, b_ref, o_ref, *, heads, block):
    z = jnp.dot(h_ref[...], w_ref[...], preferred_element_type=F32) + b_ref[...]
    lf = (jnp.minimum(z, 0.0) - jnp.log1p(jnp.exp(-jnp.abs(z)))) * LOG2E
    seq = lf.shape[0]
    r_i = lax.broadcasted_iota(jnp.int32, (block, block), 0)
    c_i = lax.broadcasted_iota(jnp.int32, (block, block), 1)
    upto = (c_i <= r_i).astype(BF16)
    lane = lax.broadcasted_iota(jnp.int32, (block, LANES), 1)
    carry = jnp.zeros((1, LANES), F32)
    for blk in range(seq // block):
        rows = slice(blk * block, (blk + 1) * block)
        part = lf[rows]
        cum = carry
        for _ in range(N_SPLIT):
            piece = part.astype(BF16)
            cum = cum + jnp.dot(upto, piece, preferred_element_type=F32)
            part = part - piece.astype(F32)
        carry = cum[block - 1:block, :]
        for h in range(heads):
            col = jnp.broadcast_to(cum[:, h:h + 1], (block, LANES))
            hi = col.astype(BF16).astype(F32)
            mid = (col - hi).astype(BF16).astype(F32)
            lo = col - hi - mid
            split = jnp.where(lane == 0, hi, jnp.where(lane == 1, mid, jnp.where(lane == 2, lo, 0.0)))
            o_ref[rows, h * LANES:(h + 1) * LANES] = split.astype(BF16)


def _forget_cumsum(h16, w_forget, b_forget, batch, seq):
    d, heads = w_forget.shape
    w_pad = jnp.zeros((d, LANES), BF16).at[:, :heads].set(w_forget.astype(BF16))
    b_pad = jnp.zeros((1, LANES), F32).at[0, :heads].set(b_forget.astype(F32))
    return pl.pallas_call(
        functools.partial(_forget_kernel, heads=heads, block=min(256, seq)), grid=(batch,),
        in_specs=[pl.BlockSpec((seq, d), lambda b: (b, 0)),
                  pl.BlockSpec((d, LANES), lambda b: (0, 0)),
                  pl.BlockSpec((1, LANES), lambda b: (0, 0))],
        out_specs=pl.BlockSpec((seq, heads * LANES), lambda b: (b, 0)),
        out_shape=jax.ShapeDtypeStruct((batch * seq, heads * LANES), BF16),
        compiler_params=_params("parallel"), name="forget_cumsum",
    )(h16, w_pad, b_pad)


N_SPLIT = 3


def _fox_kernel(q_ref, k_ref, v_ref, a_ref, o_ref, acc_sc, kaug_sc, *, tile, hp):
    qi = pl.program_id(2)
    dh = FOX_HEAD_DIM

    @pl.when(qi == 0)
    def _():
        for hh in range(hp):
            kaug_sc[:, 2 * hh * dh:(2 * hh + 1) * dh] = k_ref[:, hh * dh:(hh + 1) * dh]
            kaug_sc[:, (2 * hh + 1) * dh:(2 * hh + 2) * dh] = a_ref[:, hh * dh:(hh + 1) * dh]

    lane = lax.broadcasted_iota(jnp.int32, (tile, dh), 1)
    q_extra = jnp.where(lane < N_SPLIT, -1.0, 0.0).astype(BF16)
    q_aug = [jnp.concatenate([q_ref[:, hh * dh:(hh + 1) * dh], q_extra], axis=1) for hh in range(hp)]
    acc_sc[...] = jnp.zeros(acc_sc.shape, F32)

    def block(j, stats, diagonal):
        rows = pl.ds(pl.multiple_of(j * tile, tile), tile)
        s_t = []
        for hh in range(hp):
            s = lax.dot_general(kaug_sc[rows, 2 * hh * dh:(2 * hh + 2) * dh], q_aug[hh],
                                (((1,), (1,)), ((), ())), preferred_element_type=F32)
            if diagonal:
                key = lax.broadcasted_iota(jnp.int32, s.shape, 0)
                qry = lax.broadcasted_iota(jnp.int32, s.shape, 1)
                s = jnp.where(key <= qry, s, -jnp.inf)
            s_t.append(s)
        out, scale, pv = [], [], []
        for hh in range(hp):
            m_prev, l_prev = stats[hh]
            m_new = jnp.maximum(m_prev, jnp.max(s_t[hh], axis=0, keepdims=True))
            a = jnp.exp2(m_prev - m_new)
            p = jnp.exp2(s_t[hh] - m_new)
            out.append((m_new, a * l_prev + jnp.sum(p, axis=0, keepdims=True)))
            scale.append(a)
            pv.append(lax.dot_general(v_ref[rows, hh * dh:(hh + 1) * dh], p.astype(BF16),
                                      (((0,), (0,)), ((), ())), preferred_element_type=F32))
        for hh in range(hp):
            acc_sc[hh] = scale[hh] * acc_sc[hh] + pv[hh]
        return tuple(out)

    init = tuple((jnp.full((1, tile), -jnp.inf, F32), jnp.zeros((1, tile), F32)) for _ in range(hp))
    stats = lax.fori_loop(0, qi, lambda j, st: block(j, st, False), init)
    stats = block(qi, stats, True)
    for hh in range(hp):
        o_ref[:, hh * dh:(hh + 1) * dh] = (acc_sc[hh] / stats[hh][1]).T.astype(o_ref.dtype)


def _fox_attention(qkv16, aug16, batch, seq, hp=8):
    heads, dh = FOX_HEADS, FOX_HEAD_DIM
    tile = min(ATTN_TILE, seq)
    nq = seq // tile
    groups = heads // hp
    w = hp * dh
    return pl.pallas_call(
        functools.partial(_fox_kernel, tile=tile, hp=hp), grid=(batch, groups, nq),
        in_specs=[pl.BlockSpec((tile, w), lambda b, g, i: (b * nq + i, g)),
                  pl.BlockSpec((seq, w), lambda b, g, i: (b, groups + g)),
                  pl.BlockSpec((seq, w), lambda b, g, i: (b, 2 * groups + g)),
                  pl.BlockSpec((seq, w), lambda b, g, i: (b, g))],
        out_specs=pl.BlockSpec((tile, w), lambda b, g, i: (b * nq + i, g)),
        out_shape=jax.ShapeDtypeStruct((batch * seq, heads * dh), BF16),
        scratch_shapes=[pltpu.VMEM((hp, dh, tile), F32), pltpu.VMEM((seq, 2 * w), BF16)],
        compiler_params=_params("parallel", "parallel", "arbitrary"), name="fox_attention",
    )(qkv16, qkv16, qkv16, aug16)


def _ret_kernel(q_ref, k_ref, v_ref, g_ref, dm_ref, qd_ref, kd_ref, cd_ref, o_ref, st_sc, *, hp):
    @pl.when(pl.program_id(2) == 0)
    def _():
        st_sc[...] = jnp.zeros(st_sc.shape, F32)

    dk, dv = RET_QK_DIM, RET_V_DIM
    for hh in range(hp):
        q = q_ref[:, hh * dk:(hh + 1) * dk]
        k = k_ref[:, hh * dk:(hh + 1) * dk]
        v = v_ref[:, hh * dv:(hh + 1) * dv]
        scores = lax.dot_general(q, k, (((1,), (1,)), ((), ())), preferred_element_type=F32) * dm_ref[hh]
        intra = jnp.dot(scores.astype(BF16), v, preferred_element_type=F32)
        state = st_sc[hh]
        cross = jnp.dot(q, state.astype(BF16), preferred_element_type=F32) * qd_ref[hh]
        k_dec = (k.astype(F32) * kd_ref[hh]).astype(BF16)
        st_sc[hh] = state * cd_ref[hh] + lax.dot_general(
            k_dec, v, (((0,), (0,)), ((), ())), preferred_element_type=F32)
        o = intra + cross
        o = o * lax.rsqrt(jnp.mean(o * o, axis=-1, keepdims=True) + RMS_EPS)
        o_ref[:, hh * dv:(hh + 1) * dv] = (g_ref[:, hh * dv:(hh + 1) * dv].astype(F32) * o).astype(o_ref.dtype)


def _retention_tables(tile):
    heads = RET_HEADS
    log_gamma = jnp.log1p(-(2.0 ** (-5.0 - jnp.arange(heads, dtype=F32))))
    pos = jnp.arange(tile, dtype=F32)
    chunk = jnp.arange(tile, dtype=jnp.int32) // CHUNK
    dist = pos[:, None] - pos[None, :]
    same = chunk[:, None] == chunk[None, :]
    earlier = chunk[None, :] < chunk[:, None]
    expo = jnp.where(same, jnp.abs(dist), dist)
    dmask = jnp.where((same | earlier)[None], jnp.exp(log_gamma[:, None, None] * expo[None]), 0.0)
    q_dec = jnp.broadcast_to(jnp.exp(log_gamma[:, None] * (pos + 1.0))[:, :, None], (heads, tile, RET_V_DIM))
    k_dec = jnp.broadcast_to(jnp.exp(log_gamma[:, None] * (tile - 1 - pos))[:, :, None], (heads, tile, RET_QK_DIM))
    c_dec = jnp.broadcast_to(jnp.exp(log_gamma * tile)[:, None, None], (heads, 1, RET_V_DIM))
    return dmask, q_dec, k_dec, c_dec


def _retention(rqk16, rv16, rg16, batch, seq, hp=8):
    heads, dk, dv = RET_HEADS, RET_QK_DIM, RET_V_DIM
    tile = min(RET_TILE, seq)
    nt = seq // tile
    groups = heads // hp
    dmask, q_dec, k_dec, c_dec = _retention_tables(tile)
    val = pl.BlockSpec((tile, hp * dv), lambda b, g, t: (b * nt + t, g))
    return pl.pallas_call(
        functools.partial(_ret_kernel, hp=hp), grid=(batch, groups, nt),
        in_specs=[pl.BlockSpec((tile, hp * dk), lambda b, g, t: (b * nt + t, g)),
                  pl.BlockSpec((tile, hp * dk), lambda b, g, t: (b * nt + t, groups + g)),
                  val, val,
                  pl.BlockSpec((hp, tile, tile), lambda b, g, t: (g, 0, 0)),
                  pl.BlockSpec((hp, tile, dv), lambda b, g, t: (g, 0, 0)),
                  pl.BlockSpec((hp, tile, dk), lambda b, g, t: (g, 0, 0)),
                  pl.BlockSpec((hp, 1, dv), lambda b, g, t: (g, 0, 0))],
        out_specs=val,
        out_shape=jax.ShapeDtypeStruct((batch * seq, heads * dv), BF16),
        scratch_shapes=[pltpu.VMEM((hp, dk, dv), F32)],
        compiler_params=_params("parallel", "parallel", "arbitrary"), name="retention",
    )(rqk16, rqk16, rv16, rg16, dmask, q_dec, k_dec, c_dec)


def _merge_kernel(yf_ref, yr_ref, wf_ref, wr_ref, gf_ref, gr_ref, o_ref):
    fox = jnp.dot(yf_ref[...], wf_ref[...], preferred_element_type=F32)
    ret = jnp.dot(yr_ref[...], wr_ref[...], preferred_element_type=F32)
    o_ref[...] = (gf_ref[...].astype(F32) * fox + gr_ref[...].astype(F32) * ret).astype(o_ref.dtype)


def _merge(y_fox, y_ret, w_fox16, w_ret16, gates16, tn=512):
    n, kf = y_fox.shape
    kr = y_ret.shape[1]
    d = w_fox16.shape[1]
    tm = min(MERGE_ROWS, n)
    nj = d // tn
    return pl.pallas_call(
        _merge_kernel, grid=(n // tm, nj),
        in_specs=[pl.BlockSpec((tm, kf), lambda i, j: (i, 0)),
                  pl.BlockSpec((tm, kr), lambda i, j: (i, 0)),
                  pl.BlockSpec((kf, tn), lambda i, j: (0, j)),
                  pl.BlockSpec((kr, tn), lambda i, j: (0, j)),
                  pl.BlockSpec((tm, tn), lambda i, j: (i, j)),
                  pl.BlockSpec((tm, tn), lambda i, j: (i, nj + j))],
        out_specs=pl.BlockSpec((tm, tn), lambda i, j: (i, j)),
        out_shape=jax.ShapeDtypeStruct((n, d), BF16),
        compiler_params=_params("parallel", "parallel"), name="merge",
    )(y_fox, y_ret, w_fox16, w_ret16, gates16, gates16)


def _lane_pick(mask, values):
    return jnp.sum(jnp.where(mask, values, 0.0), axis=1, keepdims=True)


def _outproj_router_kernel(m_ref, w_ref, h_ref, g_ref, b_ref, wr_ref, br_ref,
                           h1_ref, h1p_ref, route_ref, wcol_ref, cnt_ref, carry_sc, *, alpha):
    i = pl.program_id(0)

    @pl.when(i == 0)
    def _():
        carry_sc[...] = jnp.zeros(carry_sc.shape, F32)

    mix = jnp.dot(m_ref[...], w_ref[...], preferred_element_type=F32)
    hn = _layer_norm_rows(alpha * h_ref[...] + mix, g_ref[...], b_ref[...])
    h1_ref[...] = hn
    h1p_ref[...] = _pack_halves(hn)

    hn_hi = hn.astype(BF16)
    hn_lo = (hn - hn_hi.astype(F32)).astype(BF16)
    part = jnp.dot(hn_hi, wr_ref[...], preferred_element_type=F32)
    logits = (part[:, :LANES] + part[:, LANES:] + br_ref[...]
              + jnp.dot(hn_lo, wr_ref[:, :LANES], preferred_element_type=F32))
    tm = logits.shape[0]
    lane = lax.broadcasted_iota(jnp.int32, logits.shape, 1)
    neg = -jnp.inf
    gl = jnp.where(lane < N_GROUPS, logits, neg)
    gmax = jnp.max(gl, axis=1, keepdims=True)
    grp_w = 1.0 / jnp.sum(jnp.exp(gl - gmax), axis=1, keepdims=True)
    gidx = jnp.min(jnp.where(gl == gmax, lane, LANES), axis=1, keepdims=True)
    e_lane = lane - EXPERT_LANE0
    in_grp = (e_lane >= 0) & (e_lane < N_EXPERTS) & ((e_lane >> LOG2_EXPERTS_PER_GROUP) == gidx)
    el = jnp.where(in_grp, logits, neg)
    max1 = jnp.max(el, axis=1, keepdims=True)
    i1 = jnp.min(jnp.where(el == max1, lane, LANES), axis=1, keepdims=True)
    el2 = jnp.where(lane == i1, neg, el)
    max2 = jnp.max(el2, axis=1, keepdims=True)
    i2 = jnp.min(jnp.where(el2 == max2, lane, LANES), axis=1, keepdims=True)
    ratio = jnp.exp(max2 - max1)
    w1 = grp_w / (1.0 + ratio)
    w2 = grp_w * ratio / (1.0 + ratio)
    sel1 = lane == i1
    sel2 = lane == i2
    sel = sel1.astype(F32) + sel2.astype(F32)
    r_i = lax.broadcasted_iota(jnp.int32, (tm, tm), 0)
    c_i = lax.broadcasted_iota(jnp.int32, (tm, tm), 1)
    before = (c_i < r_i).astype(BF16)
    rank = carry_sc[...] + jnp.dot(before, sel.astype(BF16), preferred_element_type=F32)
    new_carry = carry_sc[...] + jnp.sum(sel, axis=0, keepdims=True)
    carry_sc[...] = new_carry
    cnt_ref[...] = jnp.broadcast_to(new_carry, cnt_ref.shape)
    r1 = _lane_pick(sel1, rank)
    r2 = _lane_pick(sel2, rank)
    e1 = (i1 - EXPERT_LANE0).astype(F32)
    e2 = (i2 - EXPERT_LANE0).astype(F32)
    rec = jnp.where(lane == 0, e1, jnp.where(lane == 1, e2, jnp.where(lane == 2, r1, jnp.where(lane == 3, r2, 0.0))))
    route_ref[...] = rec.T[0:8, :]
    wcol_ref[...] = jnp.where(lane == 0, w1, jnp.where(lane == 1, w2, 0.0))


def _outproj_router(merged16, w_out16, h32, g, b, w_router, b_router, alpha):
    n, d = merged16.shape
    tm = min(ROW_TILE, n)
    row16 = pl.BlockSpec((tm, d), lambda i: (i, 0))
    vec = pl.BlockSpec((1, d), lambda i: (0, 0))
    return pl.pallas_call(
        functools.partial(_outproj_router_kernel, alpha=alpha), grid=(n // tm,),
        in_specs=[row16, pl.BlockSpec((d, d), lambda i: (0, 0)), row16, vec, vec,
                  pl.BlockSpec((d, 2 * LANES), lambda i: (0, 0)), pl.BlockSpec((1, LANES), lambda i: (0, 0))],
        out_specs=[row16, pl.BlockSpec((tm, d // 2), lambda i: (i, 0)), pl.BlockSpec((8, tm), lambda i: (0, i)),
                   pl.BlockSpec((tm, LANES), lambda i: (i, 0)), pl.BlockSpec((8, LANES), lambda i: (0, 0))],
        out_shape=[jax.ShapeDtypeStruct((n, d), F32), jax.ShapeDtypeStruct((n, d // 2), jnp.uint32),
                   jax.ShapeDtypeStruct((8, n), F32),
                   jax.ShapeDtypeStruct((n, LANES), F32), jax.ShapeDtypeStruct((8, LANES), F32)],
        scratch_shapes=[pltpu.VMEM((1, LANES), F32)],
        compiler_params=_params("arbitrary"), name="outproj_router",
    )(merged16, w_out16, h32, g.reshape(1, d), b.reshape(1, d), w_router, b_router)


def _plan_kernel(route_ref, cnt_ref, dest_ref, tiles_ref, *, tile, log2_tile):
    cnt = cnt_ref[...].astype(jnp.int32)
    padded = ((cnt + (tile - 1)) >> log2_tile) << log2_tile
    lane = lax.broadcasted_iota(jnp.int32, padded.shape, 1)
    end = padded
    k = 1
    while k < LANES:
        end = end + jnp.where(lane >= k, pltpu.roll(end, k, 1), 0)
        k *= 2
    start = end - padded
    rows = EXPERT_LANE0 + N_EXPERTS
    rows = -(-rows // 8) * 8
    def lane_to_sublane(v):
        return jnp.broadcast_to(v[0:1, :].astype(F32), (LANES, LANES)).T[0:rows, 0:1]

    start_col = lane_to_sublane(start)
    end_col = lane_to_sublane(end)
    route = route_ref[...]
    n = route.shape[1]
    sub = lax.broadcasted_iota(jnp.int32, (rows, n), 0)
    e1 = route[0:1, :].astype(jnp.int32) + EXPERT_LANE0
    e2 = route[1:2, :].astype(jnp.int32) + EXPERT_LANE0
    d1 = jnp.sum(jnp.where(sub == e1, start_col, 0.0), axis=0, keepdims=True) + route[2:3, :]
    d2 = jnp.sum(jnp.where(sub == e2, start_col, 0.0), axis=0, keepdims=True) + route[3:4, :]
    row8 = lax.broadcasted_iota(jnp.int32, dest_ref.shape, 0)
    dest_ref[...] = jnp.where(row8 == 0, d1, jnp.where(row8 == 1, d2, 0.0)).astype(jnp.int32)
    nt = tiles_ref.shape[1]
    t_start = (lax.broadcasted_iota(jnp.int32, (rows, nt), 1) * tile).astype(F32)
    sub_t = lax.broadcasted_iota(jnp.int32, (rows, nt), 0)
    is_expert = (sub_t >= EXPERT_LANE0) & (sub_t < EXPERT_LANE0 + N_EXPERTS)
    done = jnp.sum(jnp.where(is_expert & (end_col <= t_start), 1.0, 0.0), axis=0, keepdims=True)
    expert = jnp.minimum(done, float(N_EXPERTS - 1))
    used = jnp.max(end_col, axis=0, keepdims=True) * (1.0 / tile)
    row8t = lax.broadcasted_iota(jnp.int32, tiles_ref.shape, 0)
    tiles_ref[...] = jnp.where(row8t == 0, expert, jnp.broadcast_to(used, tiles_ref.shape)).astype(jnp.int32)


def _plan(route, counts, tile, n_tiles):
    n = route.shape[1]
    nt_pad = -(-n_tiles // LANES) * LANES
    return pl.pallas_call(
        functools.partial(_plan_kernel, tile=tile, log2_tile=tile.bit_length() - 1),
        out_shape=[jax.ShapeDtypeStruct((8, n), jnp.int32), jax.ShapeDtypeStruct((8, nt_pad), jnp.int32)],
        compiler_params=pltpu.CompilerParams(vmem_limit_bytes=VMEM_LIMIT), name="plan",
    )(route, counts)


def _start_rows(copies, n_rows, thread=None):
    def start(o, c):
        for u in range(DMA_UNROLL):
            for cp in copies(o * DMA_UNROLL + u):
                cp.start(priority=u % N_DMA_THREADS if thread is None else thread)
        return c

    lax.fori_loop(0, n_rows // DMA_UNROLL, start, 0)


def _wait_rows(copies, n_rows):
    def wait(o, c):
        for u in range(DMA_UNROLL):
            for cp in copies(o * DMA_UNROLL + u):
                cp.wait()
        return c

    lax.fori_loop(0, n_rows // DMA_UNROLL, wait, 0)


def _invert_kernel(d1_ref, d2_ref, tok_ref, *, unroll):
    def clear(o, c):
        for u in range(unroll):
            tok_ref[o * unroll + u] = 0
        return c

    def put(o, c):
        for u in range(unroll):
            n = o * unroll + u
            tok_ref[d1_ref[n]] = n
            tok_ref[d2_ref[n]] = n
        return c

    lax.fori_loop(0, tok_ref.shape[0] // unroll, clear, 0)
    lax.fori_loop(0, d1_ref.shape[0] // unroll, put, 0)


def _invert(d1, d2, n_slots):
    return pl.pallas_call(
        functools.partial(_invert_kernel, unroll=DMA_UNROLL),
        grid_spec=pltpu.PrefetchScalarGridSpec(
            num_scalar_prefetch=2, grid=(),
            in_specs=[], out_specs=pl.BlockSpec(memory_space=pltpu.SMEM)),
        out_shape=jax.ShapeDtypeStruct((n_slots,), jnp.int32), name="invert",
    )(d1, d2)


def _expert_kernel(te_ref, nu_ref, tok_ref, h_hbm, wg_hbm, wu_hbm, wd_hbm, y_ref,
                   *scratch, tile, first):
    n_bufs = GATHER_AHEAD + 1
    bufs = scratch[:n_bufs]
    wg_in, wu_in, wd_in, wg_sc, wu_sc, wd_sc, sem, wsem = scratch[n_bufs:]
    t = pl.program_id(0)
    n_used = nu_ref[0]

    def row_copy(tt, slot, r):
        src = h_hbm.at[pl.ds(tok_ref[tt * tile + r], 1), :]
        return pltpu.make_async_copy(src, bufs[slot].at[pl.ds(r, 1), :], sem.at[slot])

    def weight_copies(expert):
        return (pltpu.make_async_copy(wg_hbm.at[first + expert], wg_in, wsem.at[0]),
                pltpu.make_async_copy(wu_hbm.at[first + expert], wu_in, wsem.at[1]),
                pltpu.make_async_copy(wd_hbm.at[first + expert], wd_in, wsem.at[2]))

    @pl.when(t == 0)
    def _():
        for cp in weight_copies(te_ref[0]):
            cp.start()
        for k in range(GATHER_AHEAD):
            _start_rows(lambda r, k=k: (row_copy(jnp.minimum(k, n_used - 1), k, r),), tile)

    @pl.when((t < n_used) & ((t == 0) | (te_ref[t] != te_ref[jnp.maximum(t - 1, 0)])))
    def _():
        for cp in weight_copies(te_ref[t]):
            cp.wait()
        wg_sc[...] = wg_in[...].astype(BF16)
        wu_sc[...] = wu_in[...].astype(BF16)
        wd_sc[...] = wd_in[...].astype(BF16)
        run_end = lax.while_loop(lambda j: (j < n_used) & (te_ref[jnp.minimum(j, n_used - 1)] == te_ref[t]),
                                 lambda j: j + 1, t + 1)

        @pl.when(run_end < n_used)
        def _():
            for cp in weight_copies(te_ref[run_end]):
                cp.start()

    ahead = jnp.minimum(t + GATHER_AHEAD, n_used - 1)

    def tile_step(slot):
        _wait_rows(lambda r: (row_copy(t, slot, r),), tile)
        for r in range(tile):
            row_copy(ahead, (slot + GATHER_AHEAD) % n_bufs, r).start(priority=r % N_DMA_THREADS)
        x_lo, x_hi = _unpack_halves(bufs[slot][...])
        x = jnp.concatenate([x_lo.astype(BF16), x_hi.astype(BF16)], axis=1)
        gate = jnp.dot(x, wg_sc[...], preferred_element_type=F32)
        up = jnp.dot(x, wu_sc[...], preferred_element_type=F32)
        hid = (gate * jax.nn.sigmoid(gate) * up).astype(BF16)
        y_ref[...] = _pack_halves(jnp.dot(hid, wd_sc[...], preferred_element_type=F32))

        @pl.when(t == n_used - 1)
        def _():
            for k in range(1, n_bufs):
                _wait_rows(lambda r: (row_copy(ahead, (slot + k) % n_bufs, r),), tile)

    for slot in range(n_bufs):
        pl.when((t < n_used) & (t % n_bufs == slot))(functools.partial(tile_step, slot))

    @pl.when(t >= n_used)
    def _():
        y_ref[...] = jnp.zeros(y_ref.shape, y_ref.dtype)


def _experts(tile_expert, n_used, tok, h_packed, w_gate, w_up, w_down, layer, tile):
    dp = h_packed.shape[1]
    d, f = w_gate.shape[1], w_gate.shape[2]
    p = tok.shape[0]
    hbm = pl.BlockSpec(memory_space=pl.ANY)
    return pl.pallas_call(
        functools.partial(_expert_kernel, tile=tile, first=layer * N_EXPERTS),
        grid_spec=pltpu.PrefetchScalarGridSpec(
            num_scalar_prefetch=3, grid=(p // tile,),
            in_specs=[hbm, hbm, hbm, hbm],
            out_specs=pl.BlockSpec((tile, dp), lambda t, te, nu, tk: (t, 0)),
            scratch_shapes=[pltpu.VMEM((tile, dp), jnp.uint32)] * (GATHER_AHEAD + 1) + [
                pltpu.VMEM((d, f), F32), pltpu.VMEM((d, f), F32), pltpu.VMEM((f, d), F32),
                pltpu.VMEM((d, f), BF16), pltpu.VMEM((d, f), BF16), pltpu.VMEM((f, d), BF16),
                pltpu.SemaphoreType.DMA((GATHER_AHEAD + 1,)), pltpu.SemaphoreType.DMA((3,))]),
        out_shape=jax.ShapeDtypeStruct((p, dp), jnp.uint32),
        compiler_params=_params("arbitrary"), name="experts",
    )(tile_expert, n_used, tok, h_packed, w_gate, w_up, w_down)


def _combine_kernel(d1_ref, d2_ref, h_ref, wc_ref, g_ref, b_ref, y_hbm, o32_ref, o16_ref,
                    a0, b0, a1, b1, sem, *, tile, alpha, n_steps):
    i = pl.program_id(0)
    bufs = ((a0, b0), (a1, b1))

    def copies(ii, slot, r):
        n = ii * tile + r
        first, second = bufs[slot]
        return (pltpu.make_async_copy(y_hbm.at[pl.ds(d1_ref[n], 1), :], first.at[pl.ds(r, 1), :], sem.at[slot]),
                pltpu.make_async_copy(y_hbm.at[pl.ds(d2_ref[n], 1), :], second.at[pl.ds(r, 1), :], sem.at[slot]))

    @pl.when(i == 0)
    def _():
        _start_rows(lambda r: copies(i, 0, r), tile)

    nxt = jnp.minimum(i + 1, n_steps - 1)

    def tile_step(slot):
        _wait_rows(lambda r: copies(i, slot, r), tile)
        for r in range(tile):
            for cp in copies(nxt, 1 - slot, r):
                cp.start(priority=r % N_DMA_THREADS)
        first, second = bufs[slot]
        wc = wc_ref[...]
        w1, w2 = wc[:, 0:1], wc[:, 1:2]
        f_lo, f_hi = _unpack_halves(first[...])
        s_lo, s_hi = _unpack_halves(second[...])
        ffn = jnp.concatenate([w1 * f_lo + w2 * s_lo, w1 * f_hi + w2 * s_hi], axis=1)
        y = _layer_norm_rows(alpha * h_ref[...] + ffn, g_ref[...], b_ref[...])
        o32_ref[...] = y
        o16_ref[...] = y.astype(BF16)

        @pl.when(i == n_steps - 1)
        def _():
            _wait_rows(lambda r: copies(nxt, 1 - slot, r), tile)

    for slot in range(2):
        pl.when(i % 2 == slot)(functools.partial(tile_step, slot))


def _combine(d1, d2, h32, wcol, g, b, y, alpha):
    n, d = h32.shape
    tile = min(GATHER_TILE, n)
    row = pl.BlockSpec((tile, d), lambda i, a, c: (i, 0))
    vec = pl.BlockSpec((1, d), lambda i, a, c: (0, 0))
    return pl.pallas_call(
        functools.partial(_combine_kernel, tile=tile, alpha=alpha, n_steps=n // tile),
        grid_spec=pltpu.PrefetchScalarGridSpec(
            num_scalar_prefetch=2, grid=(n // tile,),
            in_specs=[row, pl.BlockSpec((tile, LANES), lambda i, a, c: (i, 0)), vec, vec,
                      pl.BlockSpec(memory_space=pl.ANY)],
            out_specs=[row, row],
            scratch_shapes=[pltpu.VMEM((tile, y.shape[1]), y.dtype)] * 4 + [pltpu.SemaphoreType.DMA((2,))]),
        out_shape=[jax.ShapeDtypeStruct((n, d), F32), jax.ShapeDtypeStruct((n, d), BF16)],
        compiler_params=_params("arbitrary"), name="combine_ln",
    )(d1, d2, h32, wcol, g.reshape(1, d), b.reshape(1, d), y)


def _rotary_tables(seq):
    half = RET_QK_DIM // 2
    inv_freq = ROPE_BASE ** (-jnp.arange(half, dtype=F32) / half)
    ang = jnp.arange(seq, dtype=jnp.int32).astype(F32)[:, None] * inv_freq[None, :]
    cos, sin = jnp.cos(ang), jnp.sin(ang)
    return jnp.concatenate([cos, cos], axis=1), jnp.concatenate([-sin, sin], axis=1)


def _col_scale(sizes_scales):
    return jnp.concatenate([jnp.full((size,), scale, F32) for size, scale in sizes_scales])


def kernel(x, ln_in_g, ln_in_b, w_in, b_forget, w_branch_fox, w_branch_ret, w_out, ln1_g, ln1_b,
           w_router_group, b_router_group, w_router_expert, b_router_expert, w_gate, w_up, w_down, ln2_g, ln2_b):
    batch, seq, d = x.shape
    n = batch * seq
    depth = w_in.shape[0]
    alpha = (2.0 * depth) ** 0.25
    fox_w = FOX_HEADS * FOX_HEAD_DIM
    ret_qk = RET_HEADS * RET_QK_DIM
    ret_v = RET_HEADS * RET_V_DIM
    d_expert = w_gate.shape[-1]
    o_ff = 3 * fox_w
    o_rq = o_ff + FOX_HEADS
    o_rv = o_rq + 2 * ret_qk
    o_rg = o_rv + ret_v
    o_gate = o_rg + ret_v
    tables = _rotary_tables(seq)
    fox_scale = _col_scale([(fox_w, FOX_HEAD_DIM ** -0.5 * LOG2E), (2 * fox_w, 1.0)])
    rqk_scale = _col_scale([(ret_qk, 1.0), (ret_qk, RET_QK_DIM ** -0.5)])
    ones_v = jnp.ones((ret_v,), F32)
    ones_g = jnp.ones((2 * d,), F32)
    n_tiles = (N_EXPERTS * (EXPERT_TILE - 1) + 2 * n) // EXPERT_TILE + 1
    n_slots = n_tiles * EXPERT_TILE

    h32, h16 = _layer_norm(x.reshape(n, d), ln_in_g, ln_in_b)
    for l in range(depth):
        w = w_in[l]
        qkv = _proj(h16, w[:, :o_ff].astype(BF16), fox_scale, "none", seq)
        aug = _forget_cumsum(h16, w[:, o_ff:o_rq], b_forget[l], batch, seq)
        rqk = _proj(h16, w[:, o_rq:o_rv].astype(BF16), rqk_scale, "rotary", seq, tables)
        rv = _proj(h16, w[:, o_rv:o_rg].astype(BF16), ones_v, "none", seq)
        rg = _proj(h16, w[:, o_rg:o_gate].astype(BF16), ones_v, "silu", seq)
        gates = _proj(h16, w[:, o_gate:].astype(BF16), ones_g, "sigmoid", seq)
        y_fox = _fox_attention(qkv, aug, batch, seq)
        y_ret = _retention(rqk, rv, rg, batch, seq)
        merged = _merge(y_fox, y_ret, w_branch_fox[l].astype(BF16), w_branch_ret[l].astype(BF16), gates)

        w_router = jnp.zeros((d, LANES), F32)
        w_router = w_router.at[:, :N_GROUPS].set(w_router_group[l])
        w_router = w_router.at[:, EXPERT_LANE0:EXPERT_LANE0 + N_EXPERTS].set(w_router_expert[l])
        b_router = jnp.zeros((1, LANES), F32)
        b_router = b_router.at[0, :N_GROUPS].set(b_router_group[l])
        b_router = b_router.at[0, EXPERT_LANE0:EXPERT_LANE0 + N_EXPERTS].set(b_router_expert[l])
        w_router_hi = lax.reduce_precision(w_router, exponent_bits=8, mantissa_bits=7)
        w_router_lo = w_router - w_router_hi
        h1, h1_packed, route, wcol, counts = _outproj_router(
            merged, w_out[l].astype(BF16), h32, ln1_g[l], ln1_b[l],
            jnp.concatenate([w_router_hi, w_router_lo], axis=1).astype(BF16), b_router, alpha)
        dest, tiles = _plan(route, counts, EXPERT_TILE, n_tiles)
        d1, d2 = dest[0], dest[1]
        tok = _invert(d1, d2, n_slots)
        y = _experts(tiles[0, :n_tiles], tiles[1, :1], tok, h1_packed,
                     w_gate.reshape(depth * N_EXPERTS, d, d_expert), w_up.reshape(depth * N_EXPERTS, d, d_expert),
                     w_down.reshape(depth * N_EXPERTS, d_expert, d), l, EXPERT_TILE)
        h32, h16 = _combine(d1, d2, h1, wcol, ln2_g[l], ln2_b[l], y, alpha)
    return h32.reshape(batch, seq, d)
```

```python
import functools

import jax
import jax.numpy as jnp
from jax import lax
from jax.experimental import pallas as pl
from jax.experimental.pallas import tpu as pltpu

F32 = jnp.float32
BF16 = jnp.bfloat16

CHUNK = 64
FOX_HEADS = 8
FOX_HEAD_DIM = 128
RET_HEADS = 8
RET_QK_DIM = 128
RET_V_DIM = 256
ROPE_BASE = 10000.0
N_GROUPS = 4
EXPERTS_PER_GROUP = 8
N_EXPERTS = N_GROUPS * EXPERTS_PER_GROUP
LOG2_EXPERTS_PER_GROUP = EXPERTS_PER_GROUP.bit_length() - 1
LN_EPS = 1e-5
RMS_EPS = 1e-6

LANES = 128
EXPERT_LANE0 = N_GROUPS
VMEM_LIMIT = 56 * 1024 * 1024

ROW_TILE = 512
MATMUL_ROWS = 1024
MERGE_ROWS = 2048
DMA_UNROLL = 32
N_DMA_THREADS = 2
GATHER_AHEAD = 4
LOG2E = 1.4426950408889634
ATTN_TILE = 256
RET_TILE = 256
EXPERT_TILE = 256
GATHER_TILE = 256


def _params(*sem):
    return pltpu.CompilerParams(dimension_semantics=sem, vmem_limit_bytes=VMEM_LIMIT)


def _pack_halves(x):
    w = x.shape[1] // 2
    lo = lax.bitcast_convert_type(x[:, :w].astype(BF16).astype(F32), jnp.uint32)
    hi = lax.bitcast_convert_type(x[:, w:].astype(BF16).astype(F32), jnp.uint32)
    return (lo >> 16) | hi


def _unpack_halves(u):
    lo = lax.bitcast_convert_type(u << 16, F32)
    hi = lax.bitcast_convert_type(u & jnp.uint32(0xFFFF0000), F32)
    return lo, hi


def _layer_norm_rows(x, g, b):
    mu = jnp.mean(x, axis=-1, keepdims=True)
    xc = x - mu
    var = jnp.mean(xc * xc, axis=-1, keepdims=True)
    return xc * lax.rsqrt(var + LN_EPS) * g + b


def _ln_kernel(x_ref, g_ref, b_ref, o32_ref, o16_ref):
    y = _layer_norm_rows(x_ref[...], g_ref[...], b_ref[...])
    o32_ref[...] = y
    o16_ref[...] = y.astype(BF16)


def _layer_norm(x, g, b):
    n, d = x.shape
    tm = min(ROW_TILE, n)
    row = pl.BlockSpec((tm, d), lambda i: (i, 0))
    vec = pl.BlockSpec((1, d), lambda i: (0, 0))
    return pl.pallas_call(
        _ln_kernel, grid=(n // tm,), in_specs=[row, vec, vec], out_specs=[row, row],
        out_shape=[jax.ShapeDtypeStruct((n, d), F32), jax.ShapeDtypeStruct((n, d), BF16)],
        compiler_params=_params("parallel"), name="ln_in",
    )(x, g.reshape(1, d), b.reshape(1, d))


def _proj_kernel(x_ref, w_ref, s_ref, *rest, act):
    acc = jnp.dot(x_ref[...], w_ref[...], preferred_element_type=F32) * s_ref[...]
    if act == "rotary":
        cos_ref, sin_ref, o_ref = rest
        cos, sin = cos_ref[...], sin_ref[...]
        for c in range(acc.shape[1] // LANES):
            blk = acc[:, c * LANES:(c + 1) * LANES]
            rot = blk * cos + pltpu.roll(blk, LANES // 2, 1) * sin
            o_ref[:, c * LANES:(c + 1) * LANES] = rot.astype(o_ref.dtype)
        return
    (o_ref,) = rest
    if act == "silu":
        acc = acc * jax.nn.sigmoid(acc)
    elif act == "sigmoid":
        acc = jax.nn.sigmoid(acc)
    o_ref[...] = acc.astype(o_ref.dtype)


def _proj(x16, w16, colscale, act, seq, tables=None, tn=1024):
    n, k = x16.shape
    m = w16.shape[1]
    tm = min(MATMUL_ROWS, seq)
    in_specs = [pl.BlockSpec((tm, k), lambda i, j: (i, 0)),
                pl.BlockSpec((k, tn), lambda i, j: (0, j)),
                pl.BlockSpec((1, tn), lambda i, j: (0, j))]
    args = [x16, w16, colscale.reshape(1, m)]
    if act == "rotary":
        per_seq = seq // tm
        tab = pl.BlockSpec((tm, LANES), lambda i, j: (i % per_seq, 0))
        in_specs += [tab, tab]
        args += list(tables)
    return pl.pallas_call(
        functools.partial(_proj_kernel, act=act), grid=(n // tm, m // tn),
        in_specs=in_specs, out_specs=pl.BlockSpec((tm, tn), lambda i, j: (i, j)),
        out_shape=jax.ShapeDtypeStruct((n, m), BF16),
        compiler_params=_params("parallel", "parallel"), name="proj_" + act,
    )(*args)


def _forget_kernel(h_ref, w_ref, b_ref, o_ref, *, heads, block):
    z = jnp.dot(h_ref[...], w_ref[...], preferred_element_type=F32) + b_ref[...]
    lf = (jnp.minimum(z, 0.0) - jnp.log1p(jnp.exp(-jnp.abs(z)))) * LOG2E
    seq = lf.shape[0]
    r_i = lax.broadcasted_iota(jnp.int32, (block, block), 0)
    c_i = lax.broadcasted_iota(jnp.int32, (block, block), 1)
    upto = (c_i <= r_i).astype(BF16)
    lane = lax.broadcasted_iota(jnp.int32, (block, LANES), 1)
    carry = jnp.zeros((1, LANES), F32)
    for blk in range(seq // block):
        rows = slice(blk * block, (blk + 1) * block)
        part = lf[rows]
        cum = carry
        for _ in range(N_SPLIT):
            piece = part.astype(BF16)
            cum = cum + jnp.dot(upto, piece, preferred_element_type=F32)
            part = part - piece.astype(F32)
        carry = cum[block - 1:block, :]
        for h in range(heads):
            col = jnp.broadcast_to(cum[:, h:h + 1], (block, LANES))
            hi = col.astype(BF16).astype(F32)
            mid = (col - hi).astype(BF16).astype(F32)
            lo = col - hi - mid
            split = jnp.where(lane == 0, hi, jnp.where(lane == 1, mid, jnp.where(lane == 2, lo, 0.0)))
            o_ref[rows, h * LANES:(h + 1) * LANES] = split.astype(BF16)


def _forget_cumsum(h16, w_forget, b_forget, batch, seq):
    d, heads = w_forget.shape
    w_pad = jnp.zeros((d, LANES), BF16).at[:, :heads].set(w_forget.astype(BF16))
    b_pad = jnp.zeros((1, LANES), F32).at[0, :heads].set(b_forget.astype(F32))
    return pl.pallas_call(
        functools.partial(_forget_kernel, heads=heads, block=min(256, seq)), grid=(batch,),
        in_specs=[pl.BlockSpec((seq, d), lambda b: (b, 0)),
                  pl.BlockSpec((d, LANES), lambda b: (0, 0)),
                  pl.BlockSpec((1, LANES), lambda b: (0, 0))],
        out_specs=pl.BlockSpec((seq, heads * LANES), lambda b: (b, 0)),
        out_shape=jax.ShapeDtypeStruct((batch * seq, heads * LANES), BF16),
        compiler_params=_params("parallel"), name="forget_cumsum",
    )(h16, w_pad, b_pad)


N_SPLIT = 3


def _fox_kernel(q_ref, k_ref, v_ref, a_ref, o_ref, acc_sc, kaug_sc, *, tile, hp):
    qi = pl.program_id(2)
    dh = FOX_HEAD_DIM

    @pl.when(qi == 0)
    def _():
        for hh in range(hp):
            kaug_sc[:, 2 * hh * dh:(2 * hh + 1) * dh] = k_ref[:, hh * dh:(hh + 1) * dh]
            kaug_sc[:, (2 * hh + 1) * dh:(2 * hh + 2) * dh] = a_ref[:, hh * dh:(hh + 1) * dh]

    lane = lax.broadcasted_iota(jnp.int32, (tile, dh), 1)
    q_extra = jnp.where(lane < N_SPLIT, -1.0, 0.0).astype(BF16)
    q_aug = [jnp.concatenate([q_ref[:, hh * dh:(hh + 1) * dh], q_extra], axis=1) for hh in range(hp)]
    acc_sc[...] = jnp.zeros(acc_sc.shape, F32)

    def block(j, stats, diagonal):
        rows = pl.ds(pl.multiple_of(j * tile, tile), tile)
        s_t = []
        for hh in range(hp):
            s = lax.dot_general(kaug_sc[rows, 2 * hh * dh:(2 * hh + 2) * dh], q_aug[hh],
                                (((1,), (1,)), ((), ())), preferred_element_type=F32)
            if diagonal:
                key = lax.broadcasted_iota(jnp.int32, s.shape, 0)
                qry = lax.broadcasted_iota(jnp.int32, s.shape, 1)
                s = jnp.where(key <= qry, s, -jnp.inf)
            s_t.append(s)
        out, scale, pv = [], [], []
        for hh in range(hp):
            m_prev, l_prev = stats[hh]
            m_new = jnp.maximum(m_prev, jnp.max(s_t[hh], axis=0, keepdims=True))
            a = jnp.exp2(m_prev - m_new)
            p = jnp.exp2(s_t[hh] - m_new)
            out.append((m_new, a * l_prev + jnp.sum(p, axis=0, keepdims=True)))
            scale.append(a)
            pv.append(lax.dot_general(v_ref[rows, hh * dh:(hh + 1) * dh], p.astype(BF16),
                                      (((0,), (0,)), ((), ())), preferred_element_type=F32))
        for hh in range(hp):
            acc_sc[hh] = scale[hh] * acc_sc[hh] + pv[hh]
        return tuple(out)

    init = tuple((jnp.full((1, tile), -jnp.inf, F32), jnp.zeros((1, tile), F32)) for _ in range(hp))
    stats = lax.fori_loop(0, qi, lambda j, st: block(j, st, False), init)
    stats = block(qi, stats, True)
    for hh in range(hp):
        o_ref[:, hh * dh:(hh + 1) * dh] = (acc_sc[hh] / stats[hh][1]).T.astype(o_ref.dtype)


def _fox_attention(qkv16, aug16, batch, seq, hp=8):
    heads, dh = FOX_HEADS, FOX_HEAD_DIM
    tile = min(ATTN_TILE, seq)
    nq = seq // tile
    groups = heads // hp
    w = hp * dh
    return pl.pallas_call(
        functools.partial(_fox_kernel, tile=tile, hp=hp), grid=(batch, groups, nq),
        in_specs=[pl.BlockSpec((tile, w), lambda b, g, i: (b * nq + i, g)),
                  pl.BlockSpec((seq, w), lambda b, g, i: (b, groups + g)),
                  pl.BlockSpec((seq, w), lambda b, g, i: (b, 2 * groups + g)),
                  pl.BlockSpec((seq, w), lambda b, g, i: (b, g))],
        out_specs=pl.BlockSpec((tile, w), lambda b, g, i: (b * nq + i, g)),
        out_shape=jax.ShapeDtypeStruct((batch * seq, heads * dh), BF16),
        scratch_shapes=[pltpu.VMEM((hp, dh, tile), F32), pltpu.VMEM((seq, 2 * w), BF16)],
        compiler_params=_params("parallel", "parallel", "arbitrary"), name="fox_attention",
    )(qkv16, qkv16, qkv16, aug16)


def _ret_kernel(q_ref, k_ref, v_ref, g_ref, dm_ref, qd_ref, kd_ref, cd_ref, o_ref, st_sc, *, hp):
    @pl.when(pl.program_id(2) == 0)
    def _():
        st_sc[...] = jnp.zeros(st_sc.shape, F32)

    dk, dv = RET_QK_DIM, RET_V_DIM
    for hh in range(hp):
        q = q_ref[:, hh * dk:(hh + 1) * dk]
        k = k_ref[:, hh * dk:(hh + 1) * dk]
        v = v_ref[:, hh * dv:(hh + 1) * dv]
        scores = lax.dot_general(q, k, (((1,), (1,)), ((), ())), preferred_element_type=F32) * dm_ref[hh]
        intra = jnp.dot(scores.astype(BF16), v, preferred_element_type=F32)
        state = st_sc[hh]
        cross = jnp.dot(q, state.astype(BF16), preferred_element_type=F32) * qd_ref[hh]
        k_dec = (k.astype(F32) * kd_ref[hh]).astype(BF16)
        st_sc[hh] = state * cd_ref[hh] + lax.dot_general(
            k_dec, v, (((0,), (0,)), ((), ())), preferred_element_type=F32)
        o = intra + cross
        o = o * lax.rsqrt(jnp.mean(o * o, axis=-1, keepdims=True) + RMS_EPS)
        o_ref[:, hh * dv:(hh + 1) * dv] = (g_ref[:, hh * dv:(hh + 1) * dv].astype(F32) * o).astype(o_ref.dtype)


def _retention_tables(tile):
    heads = RET_HEADS
    log_gamma = jnp.log1p(-(2.0 ** (-5.0 - jnp.arange(heads, dtype=F32))))
    pos = jnp.arange(tile, dtype=F32)
    chunk = jnp.arange(tile, dtype=jnp.int32) // CHUNK
    dist = pos[:, None] - pos[None, :]
    same = chunk[:, None] == chunk[None, :]
    earlier = chunk[None, :] < chunk[:, None]
    expo = jnp.where(same, jnp.abs(dist), dist)
    dmask = jnp.where((same | earlier)[None], jnp.exp(log_gamma[:, None, None] * expo[None]), 0.0)
    q_dec = jnp.broadcast_to(jnp.exp(log_gamma[:, None] * (pos + 1.0))[:, :, None], (heads, tile, RET_V_DIM))
    k_dec = jnp.broadcast_to(jnp.exp(log_gamma[:, None] * (tile - 1 - pos))[:, :, None], (heads, tile, RET_QK_DIM))
    c_dec = jnp.broadcast_to(jnp.exp(log_gamma * tile)[:, None, None], (heads, 1, RET_V_DIM))
    return dmask, q_dec, k_dec, c_dec


def _retention(rqk16, rv16, rg16, batch, seq, hp=8):
    heads, dk, dv = RET_HEADS, RET_QK_DIM, RET_V_DIM
    tile = min(RET_TILE, seq)
    nt = seq // tile
    groups = heads // hp
    dmask, q_dec, k_dec, c_dec = _retention_tables(tile)
    val = pl.BlockSpec((tile, hp * dv), lambda b, g, t: (b * nt + t, g))
    return pl.pallas_call(
        functools.partial(_ret_kernel, hp=hp), grid=(batch, groups, nt),
        in_specs=[pl.BlockSpec((tile, hp * dk), lambda b, g, t: (b * nt + t, g)),
                  pl.BlockSpec((tile, hp * dk), lambda b, g, t: (b * nt + t, groups + g)),
                  val, val,
                  pl.BlockSpec((hp, tile, tile), lambda b, g, t: (g, 0, 0)),
                  pl.BlockSpec((hp, tile, dv), lambda b, g, t: (g, 0, 0)),
                  pl.BlockSpec((hp, tile, dk), lambda b, g, t: (g, 0, 0)),
                  pl.BlockSpec((hp, 1, dv), lambda b, g, t: (g, 0, 0))],
        out_specs=val,
        out_shape=jax.ShapeDtypeStruct((batch * seq, heads * dv), BF16),
        scratch_shapes=[pltpu.VMEM((hp, dk, dv), F32)],
        compiler_params=_params("parallel", "parallel", "arbitrary"), name="retention",
    )(rqk16, rqk16, rv16, rg16, dmask, q_dec, k_dec, c_dec)


def _merge_kernel(yf_ref, yr_ref, wf_ref, wr_ref, gf_ref, gr_ref, o_ref):
    fox = jnp.dot(yf_ref[...], wf_ref[...], preferred_element_type=F32)
    ret = jnp.dot(yr_ref[...], wr_ref[...], preferred_element_type=F32)
    o_ref[...] = (gf_ref[...].astype(F32) * fox + gr_ref[...].astype(F32) * ret).astype(o_ref.dtype)


def _merge(y_fox, y_ret, w_fox16, w_ret16, gates16, tn=512):
    n, kf = y_fox.shape
    kr = y_ret.shape[1]
    d = w_fox16.shape[1]
    tm = min(MERGE_ROWS, n)
    nj = d // tn
    return pl.pallas_call(
        _merge_kernel, grid=(n // tm, nj),
        in_specs=[pl.BlockSpec((tm, kf), lambda i, j: (i, 0)),
                  pl.BlockSpec((tm, kr), lambda i, j: (i, 0)),
                  pl.BlockSpec((kf, tn), lambda i, j: (0, j)),
                  pl.BlockSpec((kr, tn), lambda i, j: (0, j)),
                  pl.BlockSpec((tm, tn), lambda i, j: (i, j)),
                  pl.BlockSpec((tm, tn), lambda i, j: (i, nj + j))],
        out_specs=pl.BlockSpec((tm, tn), lambda i, j: (i, j)),
        out_shape=jax.ShapeDtypeStruct((n, d), BF16),
        compiler_params=_params("parallel", "parallel"), name="merge",
    )(y_fox, y_ret, w_fox16, w_ret16, gates16, gates16)


def _lane_pick(mask, values):
    return jnp.sum(jnp.where(mask, values, 0.0), axis=1, keepdims=True)


def _outproj_router_kernel(m_ref, w_ref, h_ref, g_ref, b_ref, wr_ref, br_ref,
                           h1_ref, h1p_ref, route_ref, wcol_ref, cnt_ref, carry_sc, *, alpha):
    i = pl.program_id(0)

    @pl.when(i == 0)
    def _():
        carry_sc[...] = jnp.zeros(carry_sc.shape, F32)

    mix = jnp.dot(m_ref[...], w_ref[...], preferred_element_type=F32)
    hn = _layer_norm_rows(alpha * h_ref[...] + mix, g_ref[...], b_ref[...])
    h1_ref[...] = hn
    h1p_ref[...] = _pack_halves(hn)

    hn_hi = hn.astype(BF16)
    hn_lo = (hn - hn_hi.astype(F32)).astype(BF16)
    part = jnp.dot(hn_hi, wr_ref[...], preferred_element_type=F32)
    logits = (part[:, :LANES] + part[:, LANES:] + br_ref[...]
              + jnp.dot(hn_lo, wr_ref[:, :LANES], preferred_element_type=F32))
    tm = logits.shape[0]
    lane = lax.broadcasted_iota(jnp.int32, logits.shape, 1)
    neg = -jnp.inf
    gl = jnp.where(lane < N_GROUPS, logits, neg)
    gmax = jnp.max(gl, axis=1, keepdims=True)
    grp_w = 1.0 / jnp.sum(jnp.exp(gl - gmax), axis=1, keepdims=True)
    gidx = jnp.min(jnp.where(gl == gmax, lane, LANES), axis=1, keepdims=True)
    e_lane = lane - EXPERT_LANE0
    in_grp = (e_lane >= 0) & (e_lane < N_EXPERTS) & ((e_lane >> LOG2_EXPERTS_PER_GROUP) == gidx)
    el = jnp.where(in_grp, logits, neg)
    max1 = jnp.max(el, axis=1, keepdims=True)
    i1 = jnp.min(jnp.where(el == max1, lane, LANES), axis=1, keepdims=True)
    el2 = jnp.where(lane == i1, neg, el)
    max2 = jnp.max(el2, axis=1, keepdims=True)
    i2 = jnp.min(jnp.where(el2 == max2, lane, LANES), axis=1, keepdims=True)
    ratio = jnp.exp(max2 - max1)
    w1 = grp_w / (1.0 + ratio)
    w2 = grp_w * ratio / (1.0 + ratio)
    sel1 = lane == i1
    sel2 = lane == i2
    sel = sel1.astype(F32) + sel2.astype(F32)
    r_i = lax.broadcasted_iota(jnp.int32, (tm, tm), 0)
    c_i = lax.broadcasted_iota(jnp.int32, (tm, tm), 1)
    before = (c_i < r_i).astype(BF16)
    rank = carry_sc[...] + jnp.dot(before, sel.astype(BF16), preferred_element_type=F32)
    new_carry = carry_sc[...] + jnp.sum(sel, axis=0, keepdims=True)
    carry_sc[...] = new_carry
    cnt_ref[...] = jnp.broadcast_to(new_carry, cnt_ref.shape)
    r1 = _lane_pick(sel1, rank)
    r2 = _lane_pick(sel2, rank)
    e1 = (i1 - EXPERT_LANE0).astype(F32)
    e2 = (i2 - EXPERT_LANE0).astype(F32)
    rec = jnp.where(lane == 0, e1, jnp.where(lane == 1, e2, jnp.where(lane == 2, r1, jnp.where(lane == 3, r2, 0.0))))
    route_ref[...] = rec.T[0:8, :]
    wcol_ref[...] = jnp.where(lane == 0, w1, jnp.where(lane == 1, w2, 0.0))


def _outproj_router(merged16, w_out16, h32, g, b, w_router, b_router, alpha):
    n, d = merged16.shape
    tm = min(ROW_TILE, n)
    row16 = pl.BlockSpec((tm, d), lambda i: (i, 0))
    vec = pl.BlockSpec((1, d), lambda i: (0, 0))
    return pl.pallas_call(
        functools.partial(_outproj_router_kernel, alpha=alpha), grid=(n // tm,),
        in_specs=[row16, pl.BlockSpec((d, d), lambda i: (0, 0)), row16, vec, vec,
                  pl.BlockSpec((d, 2 * LANES), lambda i: (0, 0)), pl.BlockSpec((1, LANES), lambda i: (0, 0))],
        out_specs=[row16, pl.BlockSpec((tm, d // 2), lambda i: (i, 0)), pl.BlockSpec((8, tm), lambda i: (0, i)),
                   pl.BlockSpec((tm, LANES), lambda i: (i, 0)), pl.BlockSpec((8, LANES), lambda i: (0, 0))],
        out_shape=[jax.ShapeDtypeStruct((n, d), F32), jax.ShapeDtypeStruct((n, d // 2), jnp.uint32),
                   jax.ShapeDtypeStruct((8, n), F32),
                   jax.ShapeDtypeStruct((n, LANES), F32), jax.ShapeDtypeStruct((8, LANES), F32)],
        scratch_shapes=[pltpu.VMEM((1, LANES), F32)],
        compiler_params=_params("arbitrary"), name="outproj_router",
    )(merged16, w_out16, h32, g.reshape(1, d), b.reshape(1, d), w_router, b_router)


def _plan_kernel(route_ref, cnt_ref, dest_ref, tiles_ref, *, tile, log2_tile):
    cnt = cnt_ref[...].astype(jnp.int32)
    padded = ((cnt + (tile - 1)) >> log2_tile) << log2_tile
    lane = lax.broadcasted_iota(jnp.int32, padded.shape, 1)
    end = padded
    k = 1
    while k < LANES:
        end = end + jnp.where(lane >= k, pltpu.roll(end, k, 1), 0)
        k *= 2
    start = end - padded
    rows = EXPERT_LANE0 + N_EXPERTS
    rows = -(-rows // 8) * 8
    def lane_to_sublane(v):
        return jnp.broadcast_to(v[0:1, :].astype(F32), (LANES, LANES)).T[0:rows, 0:1]

    start_col = lane_to_sublane(start)
    end_col = lane_to_sublane(end)
    route = route_ref[...]
    n = route.shape[1]
    sub = lax.broadcasted_iota(jnp.int32, (rows, n), 0)
    e1 = route[0:1, :].astype(jnp.int32) + EXPERT_LANE0
    e2 = route[1:2, :].astype(jnp.int32) + EXPERT_LANE0
    d1 = jnp.sum(jnp.where(sub == e1, start_col, 0.0), axis=0, keepdims=True) + route[2:3, :]
    d2 = jnp.sum(jnp.where(sub == e2, start_col, 0.0), axis=0, keepdims=True) + route[3:4, :]
    row8 = lax.broadcasted_iota(jnp.int32, dest_ref.shape, 0)
    dest_ref[...] = jnp.where(row8 == 0, d1, jnp.where(row8 == 1, d2, 0.0)).astype(jnp.int32)
    nt = tiles_ref.shape[1]
    t_start = (lax.broadcasted_iota(jnp.int32, (rows, nt), 1) * tile).astype(F32)
    sub_t = lax.broadcasted_iota(jnp.int32, (rows, nt), 0)
    is_expert = (sub_t >= EXPERT_LANE0) & (sub_t < EXPERT_LANE0 + N_EXPERTS)
    done = jnp.sum(jnp.where(is_expert & (end_col <= t_start), 1.0, 0.0), axis=0, keepdims=True)
    expert = jnp.minimum(done, float(N_EXPERTS - 1))
    used = jnp.max(end_col, axis=0, keepdims=True) * (1.0 / tile)
    row8t = lax.broadcasted_iota(jnp.int32, tiles_ref.shape, 0)
    tiles_ref[...] = jnp.where(row8t == 0, expert, jnp.broadcast_to(used, tiles_ref.shape)).astype(jnp.int32)


def _plan(route, counts, tile, n_tiles):
    n = route.shape[1]
    nt_pad = -(-n_tiles // LANES) * LANES
    return pl.pallas_call(
        functools.partial(_plan_kernel, tile=tile, log2_tile=tile.bit_length() - 1),
        out_shape=[jax.ShapeDtypeStruct((8, n), jnp.int32), jax.ShapeDtypeStruct((8, nt_pad), jnp.int32)],
        compiler_params=pltpu.CompilerParams(vmem_limit_bytes=VMEM_LIMIT), name="plan",
    )(route, counts)


def _start_rows(copies, n_rows, thread=None):
    def start(o, c):
        for u in range(DMA_UNROLL):
            for cp in copies(o * DMA_UNROLL + u):
                cp.start(priority=u % N_DMA_THREADS if thread is None else thread)
        return c

    lax.fori_loop(0, n_rows // DMA_UNROLL, start, 0)


def _wait_rows(copies, n_rows):
    def wait(o, c):
        for u in range(DMA_UNROLL):
            for cp in copies(o * DMA_UNROLL + u):
                cp.wait()
        return c

    lax.fori_loop(0, n_rows // DMA_UNROLL, wait, 0)


def _invert_kernel(d1_ref, d2_ref, tok_ref, *, unroll):
    def clear(o, c):
        for u in range(unroll):
            tok_ref[o * unroll + u] = 0
        return c

    def put(o, c):
        for u in range(unroll):
            n = o * unroll + u
            tok_ref[d1_ref[n]] = n
            tok_ref[d2_ref[n]] = n
        return c

    lax.fori_loop(0, tok_ref.shape[0] // unroll, clear, 0)
    lax.fori_loop(0, d1_ref.shape[0] // unroll, put, 0)


def _invert(d1, d2, n_slots):
    return pl.pallas_call(
        functools.partial(_invert_kernel, unroll=DMA_UNROLL),
        grid_spec=pltpu.PrefetchScalarGridSpec(
            num_scalar_prefetch=2, grid=(),
            in_specs=[], out_specs=pl.BlockSpec(memory_space=pltpu.SMEM)),
        out_shape=jax.ShapeDtypeStruct((n_slots,), jnp.int32), name="invert",
    )(d1, d2)


def _expert_kernel(te_ref, nu_ref, tok_ref, h_hbm, wg_hbm, wu_hbm, wd_hbm, y_ref,
                   *scratch, tile, first):
    n_bufs = GATHER_AHEAD + 1
    bufs = scratch[:n_bufs]
    wg_in, wu_in, wd_in, wg_sc, wu_sc, wd_sc, sem, wsem = scratch[n_bufs:]
    t = pl.program_id(0)
    n_used = nu_ref[0]

    def row_copy(tt, slot, r):
        src = h_hbm.at[pl.ds(tok_ref[tt * tile + r], 1), :]
        return pltpu.make_async_copy(src, bufs[slot].at[pl.ds(r, 1), :], sem.at[slot])

    def weight_copies(expert):
        return (pltpu.make_async_copy(wg_hbm.at[first + expert], wg_in, wsem.at[0]),
                pltpu.make_async_copy(wu_hbm.at[first + expert], wu_in, wsem.at[1]),
                pltpu.make_async_copy(wd_hbm.at[first + expert], wd_in, wsem.at[2]))

    @pl.when(t == 0)
    def _():
        for cp in weight_copies(te_ref[0]):
            cp.start()
        for k in range(GATHER_AHEAD):
            _start_rows(lambda r, k=k: (row_copy(jnp.minimum(k, n_used - 1), k, r),), tile)

    @pl.when((t < n_used) & ((t == 0) | (te_ref[t] != te_ref[jnp.maximum(t - 1, 0)])))
    def _():
        for cp in weight_copies(te_ref[t]):
            cp.wait()
        wg_sc[...] = wg_in[...].astype(BF16)
        wu_sc[...] = wu_in[...].astype(BF16)
        wd_sc[...] = wd_in[...].astype(BF16)
        run_end = lax.while_loop(lambda j: (j < n_used) & (te_ref[jnp.minimum(j, n_used - 1)] == te_ref[t]),
                                 lambda j: j + 1, t + 1)

        @pl.when(run_end < n_used)
        def _():
            for cp in weight_copies(te_ref[run_end]):
                cp.start()

    ahead = jnp.minimum(t + GATHER_AHEAD, n_used - 1)

    def tile_step(slot):
        _wait_rows(lambda r: (row_copy(t, slot, r),), tile)
        for r in range(tile):
            row_copy(ahead, (slot + GATHER_AHEAD) % n_bufs, r).start(priority=r % N_DMA_THREADS)
        x_lo, x_hi = _unpack_halves(bufs[slot][...])
        x = jnp.concatenate([x_lo.astype(BF16), x_hi.astype(BF16)], axis=1)
        gate = jnp.dot(x, wg_sc[...], preferred_element_type=F32)
        up = jnp.dot(x, wu_sc[...], preferred_element_type=F32)
        hid = (gate * jax.nn.sigmoid(gate) * up).astype(BF16)
        y_ref[...] = _pack_halves(jnp.dot(hid, wd_sc[...], preferred_element_type=F32))

        @pl.when(t == n_used - 1)
        def _():
            for k in range(1, n_bufs):
                _wait_rows(lambda r: (row_copy(ahead, (slot + k) % n_bufs, r),), tile)

    for slot in range(n_bufs):
        pl.when((t < n_used) & (t % n_bufs == slot))(functools.partial(tile_step, slot))

    @pl.when(t >= n_used)
    def _():
        y_ref[...] = jnp.zeros(y_ref.shape, y_ref.dtype)


def _experts(tile_expert, n_used, tok, h_packed, w_gate, w_up, w_down, layer, tile):
    dp = h_packed.shape[1]
    d, f = w_gate.shape[1], w_gate.shape[2]
    p = tok.shape[0]
    hbm = pl.BlockSpec(memory_space=pl.ANY)
    return pl.pallas_call(
        functools.partial(_expert_kernel, tile=tile, first=layer * N_EXPERTS),
        grid_spec=pltpu.PrefetchScalarGridSpec(
            num_scalar_prefetch=3, grid=(p // tile,),
            in_specs=[hbm, hbm, hbm, hbm],
            out_specs=pl.BlockSpec((tile, dp), lambda t, te, nu, tk: (t, 0)),
            scratch_shapes=[pltpu.VMEM((tile, dp), jnp.uint32)] * (GATHER_AHEAD + 1) + [
                pltpu.VMEM((d, f), F32), pltpu.VMEM((d, f), F32), pltpu.VMEM((f, d), F32),
                pltpu.VMEM((d, f), BF16), pltpu.VMEM((d, f), BF16), pltpu.VMEM((f, d), BF16),
                pltpu.SemaphoreType.DMA((GATHER_AHEAD + 1,)), pltpu.SemaphoreType.DMA((3,))]),
        out_shape=jax.ShapeDtypeStruct((p, dp), jnp.uint32),
        compiler_params=_params("arbitrary"), name="experts",
    )(tile_expert, n_used, tok, h_packed, w_gate, w_up, w_down)


def _combine_kernel(d1_ref, d2_ref, h_ref, wc_ref, g_ref, b_ref, y_hbm, o32_ref, o16_ref,
                    a0, b0, a1, b1, sem, *, tile, alpha, n_steps):
    i = pl.program_id(0)
    bufs = ((a0, b0), (a1, b1))

    def copies(ii, slot, r):
        n = ii * tile + r
        first, second = bufs[slot]
        return (pltpu.make_async_copy(y_hbm.at[pl.ds(d1_ref[n], 1), :], first.at[pl.ds(r, 1), :], sem.at[slot]),
                pltpu.make_async_copy(y_hbm.at[pl.ds(d2_ref[n], 1), :], second.at[pl.ds(r, 1), :], sem.at[slot]))

    @pl.when(i == 0)
    def _():
        _start_rows(lambda r: copies(i, 0, r), tile)

    nxt = jnp.minimum(i + 1, n_steps - 1)

    def tile_step(slot):
        _wait_rows(lambda r: copies(i, slot, r), tile)
        for r in range(tile):
            for cp in copies(nxt, 1 - slot, r):
                cp.start(priority=r % N_DMA_THREADS)
        first, second = bufs[slot]
        wc = wc_ref[...]
        w1, w2 = wc[:, 0:1], wc[:, 1:2]
        f_lo, f_hi = _unpack_halves(first[...])
        s_lo, s_hi = _unpack_halves(second[...])
        ffn = jnp.concatenate([w1 * f_lo + w2 * s_lo, w1 * f_hi + w2 * s_hi], axis=1)
        y = _layer_norm_rows(alpha * h_ref[...] + ffn, g_ref[...], b_ref[...])
        o32_ref[...] = y
        o16_ref[...] = y.astype(BF16)

        @pl.when(i == n_steps - 1)
        def _():
            _wait_rows(lambda r: copies(nxt, 1 - slot, r), tile)

    for slot in range(2):
        pl.when(i % 2 == slot)(functools.partial(tile_step, slot))


def _combine(d1, d2, h32, wcol, g, b, y, alpha):
    n, d = h32.shape
    tile = min(GATHER_TILE, n)
    row = pl.BlockSpec((tile, d), lambda i, a, c: (i, 0))
    vec = pl.BlockSpec((1, d), lambda i, a, c: (0, 0))
    return pl.pallas_call(
        functools.partial(_combine_kernel, tile=tile, alpha=alpha, n_steps=n // tile),
        grid_spec=pltpu.PrefetchScalarGridSpec(
            num_scalar_prefetch=2, grid=(n // tile,),
            in_specs=[row, pl.BlockSpec((tile, LANES), lambda i, a, c: (i, 0)), vec, vec,
                      pl.BlockSpec(memory_space=pl.ANY)],
            out_specs=[row, row],
            scratch_shapes=[pltpu.VMEM((tile, y.shape[1]), y.dtype)] * 4 + [pltpu.SemaphoreType.DMA((2,))]),
        out_shape=[jax.ShapeDtypeStruct((n, d), F32), jax.ShapeDtypeStruct((n, d), BF16)],
        compiler_params=_params("arbitrary"), name="combine_ln",
    )(d1, d2, h32, wcol, g.reshape(1, d), b.reshape(1, d), y)


def _rotary_tables(seq):
    half = RET_QK_DIM // 2
    inv_freq = ROPE_BASE ** (-jnp.arange(half, dtype=F32) / half)
    ang = jnp.arange(seq, dtype=jnp.int32).astype(F32)[:, None] * inv_freq[None, :]
    cos, sin = jnp.cos(ang), jnp.sin(ang)
    return jnp.concatenate([cos, cos], axis=1), jnp.concatenate([-sin, sin], axis=1)


def _col_scale(sizes_scales):
    return jnp.concatenate([jnp.full((size,), scale, F32) for size, scale in sizes_scales])


def kernel(x, ln_in_g, ln_in_b, w_in, b_forget, w_branch_fox, w_branch_ret, w_out, ln1_g, ln1_b,
           w_router_group, b_router_group, w_router_expert, b_router_expert, w_gate, w_up, w_down, ln2_g, ln2_b):
    batch, seq, d = x.shape
    n = batch * seq
    depth = w_in.shape[0]
    alpha = (2.0 * depth) ** 0.25
    fox_w = FOX_HEADS * FOX_HEAD_DIM
    ret_qk = RET_HEADS * RET_QK_DIM
    ret_v = RET_HEADS * RET_V_DIM
    d_expert = w_gate.shape[-1]
    o_ff = 3 * fox_w
    o_rq = o_ff + FOX_HEADS
    o_rv = o_rq + 2 * ret_qk
    o_rg = o_rv + ret_v
    o_gate = o_rg + ret_v
    tables = _rotary_tables(seq)
    fox_scale = _col_scale([(fox_w, FOX_HEAD_DIM ** -0.5 * LOG2E), (2 * fox_w, 1.0)])
    rqk_scale = _col_scale([(ret_qk, 1.0), (ret_qk, RET_QK_DIM ** -0.5)])
    ones_v = jnp.ones((ret_v,), F32)
    ones_g = jnp.ones((2 * d,), F32)
    n_tiles = (N_EXPERTS * (EXPERT_TILE - 1) + 2 * n) // EXPERT_TILE + 1
    n_slots = n_tiles * EXPERT_TILE

    h32, h16 = _layer_norm(x.reshape(n, d), ln_in_g, ln_in_b)
    for l in range(depth):
        w = w_in[l]
        qkv = _proj(h16, w[:, :o_ff].astype(BF16), fox_scale, "none", seq)
        aug = _forget_cumsum(h16, w[:, o_ff:o_rq], b_forget[l], batch, seq)
        rqk = _proj(h16, w[:, o_rq:o_rv].astype(BF16), rqk_scale, "rotary", seq, tables)
        rv = _proj(h16, w[:, o_rv:o_rg].astype(BF16), ones_v, "none", seq)
        rg = _proj(h16, w[:, o_rg:o_gate].astype(BF16), ones_v, "silu", seq)
        gates = _proj(h16, w[:, o_gate:].astype(BF16), ones_g, "sigmoid", seq)
        y_fox = _fox_attention(qkv, aug, batch, seq)
        y_ret = _retention(rqk, rv, rg, batch, seq)
        merged = _merge(y_fox, y_ret, w_branch_fox[l].astype(BF16), w_branch_ret[l].astype(BF16), gates)

        w_router = jnp.zeros((d, LANES), F32)
        w_router = w_router.at[:, :N_GROUPS].set(w_router_group[l])
        w_router = w_router.at[:, EXPERT_LANE0:EXPERT_LANE0 + N_EXPERTS].set(w_router_expert[l])
        b_router = jnp.zeros((1, LANES), F32)
        b_router = b_router.at[0, :N_GROUPS].set(b_router_group[l])
        b_router = b_router.at[0, EXPERT_LANE0:EXPERT_LANE0 + N_EXPERTS].set(b_router_expert[l])
        w_router_hi = lax.reduce_precision(w_router, exponent_bits=8, mantissa_bits=7)
        w_router_lo = w_router - w_router_hi
        h1, h1_packed, route, wcol, counts = _outproj_router(
            merged, w_out[l].astype(BF16), h32, ln1_g[l], ln1_b[l],
            jnp.concatenate([w_router_hi, w_router_lo], axis=1).astype(BF16), b_router, alpha)
        dest, tiles = _plan(route, counts, EXPERT_TILE, n_tiles)
        d1, d2 = dest[0], dest[1]
        tok = _invert(d1, d2, n_slots)
        y = _experts(tiles[0, :n_tiles], tiles[1, :1], tok, h1_packed,
                     w_gate.reshape(depth * N_EXPERTS, d, d_expert), w_up.reshape(depth * N_EXPERTS, d, d_expert),
                     w_down.reshape(depth * N_EXPERTS, d_expert, d), l, EXPERT_TILE)
        h32, h16 = _combine(d1, d2, h1, wcol, ln2_g[l], ln2_b[l], y, alpha)
    return h32.reshape(batch, seq, d)
```
